```python
import math, functools
import jax, jax.numpy as jnp
from jax import lax
import numpy as np

D_MODEL = 1024
BATCH = 2
SEQ = 8192
DEPTH = 4
DEC_BATCH = 128
DEC_SEQ = 8
PAST_LEN = 2048
PAGE_SIZE = 128

N_A_LAYERS = DEPTH // 2
N_B_LAYERS = DEPTH - N_A_LAYERS
N_DENSE = (DEPTH + 1) // 2
N_MOE = DEPTH // 2
HEAD_DIM = 64
N_HEADS = 12
N_KV_GROUPS = 3
NSA_W = N_HEADS * HEAD_DIM
CMP_STRIDE = 16
CMP_BLOCK = 2 * CMP_STRIDE
CMP_HIDDEN = 256
SLC_BLOCK = 64
N_SEL = 16
WINDOW = 512
Q_BLOCK = 128
ALIBI_MAX_BIAS = 8.0
N_MEM = 256
MEM_HEADS = 4
MEM_HEAD_DIM = 64
MEM_W = MEM_HEADS * MEM_HEAD_DIM
C_CONV = D_MODEL - MEM_W
CONV_K = 31
D_FF = 2816
N_EXPERTS = 8
TOP_K = 2
D_EXPERT = 3584
EPS = 1e-6

kernel_name = 'yoco_conformer_nsa_memory_decoder_step'


def rmsnorm(x, g):
    xf = x.astype(jnp.float32)
    y = xf * lax.rsqrt(jnp.mean(xf * xf, axis=-1, keepdims=True) + EPS)
    return (y * g.astype(jnp.float32)).astype(x.dtype)


def layernorm(x, g, b):
    xf = x.astype(jnp.float32)
    mu = jnp.mean(xf, axis=-1, keepdims=True)
    var = jnp.mean(jnp.square(xf - mu), axis=-1, keepdims=True)
    y = (xf - mu) * lax.rsqrt(var + EPS)
    return (y * g.astype(jnp.float32) + b.astype(jnp.float32)).astype(x.dtype)


def masked_softmax(s, mask, axis):
    s = jnp.where(mask, s.astype(jnp.float32), -jnp.inf)
    m = jnp.max(s, axis=axis, keepdims=True)
    m = jnp.where(jnp.isfinite(m), m, 0.0)
    e = jnp.where(mask, jnp.exp(s - m), 0.0)
    d = jnp.sum(e, axis=axis, keepdims=True)
    return e / jnp.where(d > 0, d, 1.0)


def alibi_slopes(n):
    return 2.0 ** (-ALIBI_MAX_BIAS * jnp.arange(1, n + 1, dtype=jnp.float32) / n)


def swiglu(x, wg, wu, wd):
    return (jax.nn.silu(x @ wg) * (x @ wu)) @ wd


def moe_ffn(x, router, wg, wu, wd):
    xf = x.reshape(-1, x.shape[-1])
    logits = (xf @ router).astype(jnp.float32)
    top_v, top_i = lax.top_k(logits, TOP_K)
    w = jax.nn.softmax(top_v, axis=-1)
    gate = jnp.sum(jax.nn.one_hot(top_i, N_EXPERTS, dtype=jnp.float32) * w[..., None], axis=1)
    y = jnp.zeros_like(xf)
    for e in range(N_EXPERTS):
        y = y + gate[:, e:e + 1].astype(x.dtype) * swiglu(xf, wg[e], wu[e], wd[e])
    return y.reshape(x.shape)


def conv_mixer(u, buf, conv_w, conv_b, ln_g, ln_b):
    a, gt = jnp.split(u, 2, axis=-1)
    v = a * jax.nn.sigmoid(gt)
    vp = jnp.concatenate([buf.astype(v.dtype), v], axis=1)
    h = lax.conv_general_dilated(vp, conv_w[:, None, :].astype(v.dtype), window_strides=(1,),
                                 padding='VALID', dimension_numbers=('NWC', 'WIO', 'NWC'),
                                 feature_group_count=C_CONV) + conv_b
    h = jax.nn.silu(layernorm(h, ln_g, ln_b))
    return h, vp[:, -(CONV_K - 1):]


def mem_attend(qm, mem_kv):
    b, t = qm.shape[:2]
    s = jnp.einsum('bthd,bmhd->bhtm', qm, mem_kv[:, :, 0]).astype(jnp.float32) * MEM_HEAD_DIM ** -0.5
    p = jax.nn.softmax(s, axis=-1)
    o = jnp.einsum('bhtm,bmhd->bthd', p.astype(qm.dtype), mem_kv[:, :, 1])
    return o.reshape(b, t, MEM_W)


def compress(rows, pos_emb, w1, w2):
    b, l = rows.shape[:2]
    n_chunks = -(-l // CMP_STRIDE)
    rows = jnp.pad(rows, ((0, 0), (0, n_chunks * CMP_STRIDE - l), (0, 0), (0, 0)))
    ch = rows.reshape(b, n_chunks, CMP_STRIDE, N_KV_GROUPS, HEAD_DIM)
    blk = jnp.concatenate([ch[:, :-1], ch[:, 1:]], axis=2) + pos_emb[None, None, :, None, :]
    blk = jnp.transpose(blk, (0, 1, 3, 2, 4)).reshape(b, n_chunks - 1, N_KV_GROUPS, CMP_BLOCK * HEAD_DIM)
    return jax.nn.silu(blk @ w1) @ w2


def split_blocks(rows):
    b, l = rows.shape[:2]
    nb = -(-l // SLC_BLOCK)
    rows = jnp.pad(rows, ((0, 0), (0, nb * SLC_BLOCK - l), (0, 0), (0, 0), (0, 0)))
    rows = rows.reshape(b, nb, SLC_BLOCK, 2, N_KV_GROUPS, HEAD_DIM)
    return rows[:, :, :, 0], rows[:, :, :, 1]


def nsa_core(q, g, qpos, ck, cv, cpos, sk, sv, wk, wv, wpos, slopes):
    r = N_HEADS // N_KV_GROUPS
    tq = q.shape[0]
    qg = (q * HEAD_DIM ** -0.5).reshape(tq, N_KV_GROUPS, r, HEAD_DIM)
    qf = qpos.astype(jnp.float32)
    sl = slopes[:, :, None, None]
    s_c = jnp.einsum('tgrd,ngd->grtn', qg, ck).astype(jnp.float32) \
        - sl * jnp.abs(qf[:, None] - cpos.astype(jnp.float32)[None, :])
    p_c = masked_softmax(s_c, (cpos[None, :] <= qpos[:, None])[None, None], axis=-1)
    o_c = jnp.einsum('grtn,ngd->tgrd', p_c.astype(cv.dtype), cv)
    nb = sk.shape[0]
    nc = ck.shape[0]
    per_sel = SLC_BLOCK // CMP_STRIDE
    inside = (SLC_BLOCK - CMP_BLOCK) // CMP_STRIDE + 1
    imp = jnp.pad(p_c.sum(axis=1), ((0, 0), (0, 0), (0, per_sel * nb - nc)))
    imp = imp.reshape(N_KV_GROUPS, tq, nb, per_sel)[..., :inside].sum(-1)
    j = jnp.arange(nb)
    cur = qpos // SLC_BLOCK
    valid = j[None, :] <= cur[:, None]
    forced = valid & ((j[None, :] == 0) | (j[None, :] == cur[:, None]) | (j[None, :] == cur[:, None] - 1))
    imp = jnp.where(forced[None], jnp.inf, jnp.where(valid[None], imp, -jnp.inf))
    n_sel = min(N_SEL, nb)
    top_v, top_i = lax.top_k(imp, n_sel)
    ks = jax.vmap(lambda kb, ix: kb[ix])(jnp.transpose(sk, (2, 0, 1, 3)), top_i)
    vs = jax.vmap(lambda vb, ix: vb[ix])(jnp.transpose(sv, (2, 0, 1, 3)), top_i)
    kpos = top_i[..., None] * SLC_BLOCK + jnp.arange(SLC_BLOCK)
    m_s = (top_v > -jnp.inf)[..., None] & (kpos <= qpos[None, :, None, None])
    s_s = jnp.einsum('tgrd,gtnkd->grtnk', qg, ks).astype(jnp.float32) \
        - slopes[:, :, None, None, None] * jnp.abs(qf[None, :, None, None] - kpos.astype(jnp.float32))[:, None]
    p_s = masked_softmax(s_s, m_s[:, None], axis=(-2, -1))
    o_s = jnp.einsum('grtnk,gtnkd->tgrd', p_s.astype(vs.dtype), vs)
    s_w = jnp.einsum('tgrd,kgd->grtk', qg, wk).astype(jnp.float32) \
        - sl * jnp.abs(qf[:, None] - wpos.astype(jnp.float32)[None, :])
    m_w = (wpos[None, :] >= 0) & (wpos[None, :] <= qpos[:, None]) & (qpos[:, None] - wpos[None, :] < WINDOW)
    p_w = masked_softmax(s_w, m_w[None, None], axis=-1)
    o_w = jnp.einsum('grtk,kgd->tgrd', p_w.astype(wv.dtype), wv)
    gate = jax.nn.sigmoid(g.astype(jnp.float32)).reshape(tq, N_KV_GROUPS, r, 3)
    out = gate[..., 0:1] * o_c + gate[..., 1:2] * o_s + gate[..., 2:3] * o_w
    return out.reshape(tq, NSA_W).astype(q.dtype)


def nsa_prompt(q, g, ck, cv, sk, sv, wk, wv, slopes):
    b, t = q.shape[:2]
    nqb = t // Q_BLOCK
    cpos = jnp.arange(ck.shape[1]) * CMP_STRIDE + CMP_BLOCK - 1
    wkp = jnp.pad(wk, ((0, 0), (WINDOW, 0), (0, 0), (0, 0)))
    wvp = jnp.pad(wv, ((0, 0), (WINDOW, 0), (0, 0), (0, 0)))
    qb = jnp.transpose(q.reshape(b, nqb, Q_BLOCK, N_HEADS, HEAD_DIM), (1, 0, 2, 3, 4))
    gb = jnp.transpose(g.reshape(b, nqb, Q_BLOCK, N_HEADS, 3), (1, 0, 2, 3, 4))
    core = jax.vmap(nsa_core, in_axes=(0, 0, None, 0, 0, None, 0, 0, 0, 0, None, None))

    def one_block(args):
        qi, gi, blk = args
        s0 = blk * Q_BLOCK
        qpos = s0 + jnp.arange(Q_BLOCK)
        wpos = s0 - WINDOW + jnp.arange(WINDOW + Q_BLOCK)
        wki = lax.dynamic_slice_in_dim(wkp, s0, WINDOW + Q_BLOCK, axis=1)
        wvi = lax.dynamic_slice_in_dim(wvp, s0, WINDOW + Q_BLOCK, axis=1)
        return core(qi, gi, qpos, ck, cv, cpos, sk, sv, wki, wvi, wpos, slopes)

    out = lax.map(one_block, (qb, gb, jnp.arange(nqb)))
    return jnp.transpose(out, (1, 0, 2, 3)).reshape(b, t, NSA_W)


def nsa_sample(q, g, ck, cv, sk, sv, wk, wv, qpos, wpos, slopes):
    cpos = jnp.arange(ck.shape[1]) * CMP_STRIDE + CMP_BLOCK - 1

    def one_seq(args):
        qi, gi, cki, cvi, ski, svi, wki, wvi = args
        return nsa_core(qi, gi, qpos, cki, cvi, cpos, ski, svi, wki, wvi, wpos, slopes)

    return lax.map(one_seq, (q, g, ck, cv, sk, sv, wk, wv))


def setup_inputs(seed: int = 0) -> dict:
    key = jax.random.key(seed)
    ks = iter(jax.random.split(key, 48))

    def nrm(shape, scale):
        return scale * jax.random.normal(next(ks), shape, jnp.float32)

    n_pages = PAST_LEN // PAGE_SIZE
    n_used = DEC_BATCH * n_pages
    n_pool = n_used + max(1, n_used // 4)
    win_buf = min(WINDOW, PAST_LEN)
    d_in_a = 2 * C_CONV + MEM_W
    d_in_b = NSA_W + 3 * N_HEADS + MEM_W
    kv_w = 6 * N_KV_GROUPS * HEAD_DIM
    page_table = jax.random.permutation(next(ks), n_pool)[:n_used].reshape(DEC_BATCH, n_pages).astype(jnp.int32)
    return {
        'x_prompt': nrm((BATCH, SEQ, D_MODEL), 1.0),
        'x_sample': nrm((DEC_BATCH, DEC_SEQ, D_MODEL), 1.0),
        'state_conv': nrm((N_A_LAYERS, DEC_BATCH, CONV_K - 1, C_CONV), 0.5),
        'cache_mem_kv': nrm((DEPTH, DEC_BATCH, N_MEM, 2, MEM_HEADS, MEM_HEAD_DIM), 1.0),
        'cache_cmp_kv': nrm((n_pool, PAGE_SIZE, 2, N_KV_GROUPS, HEAD_DIM), 1.0),
        'cache_slc_kv': nrm((n_pool, PAGE_SIZE, 2, N_KV_GROUPS, HEAD_DIM), 1.0),
        'cache_win_kv': nrm((DEC_BATCH, win_buf, 2, N_KV_GROUPS, HEAD_DIM), 1.0),
        'page_table': page_table,
        'mem_prompt': nrm((BATCH, N_MEM, D_MODEL), 1.0),
        'norm1': 1.0 + nrm((DEPTH, D_MODEL), 0.1),
        'norm2': 1.0 + nrm((DEPTH, D_MODEL), 0.1),
        'norm_final': 1.0 + nrm((D_MODEL,), 0.1),
        'w_in_a': nrm((N_A_LAYERS, D_MODEL, d_in_a), D_MODEL ** -0.5),
        'conv_w': nrm((N_A_LAYERS, CONV_K, C_CONV), CONV_K ** -0.5),
        'conv_b': nrm((N_A_LAYERS, C_CONV), 0.02),
        'conv_ln_g': 1.0 + nrm((N_A_LAYERS, C_CONV), 0.1),
        'conv_ln_b': nrm((N_A_LAYERS, C_CONV), 0.02),
        'w_in_b': nrm((N_B_LAYERS, D_MODEL, d_in_b), D_MODEL ** -0.5),
        'kv_norm': 1.0 + nrm((D_MODEL,), 0.1),
        'w_kv_shared': nrm((D_MODEL, kv_w), D_MODEL ** -0.5),
        'cmp_pos': nrm((2, CMP_BLOCK, HEAD_DIM), 0.1),
        'cmp_w1': nrm((2, CMP_BLOCK * HEAD_DIM, CMP_HIDDEN), (CMP_BLOCK * HEAD_DIM) ** -0.5),
        'cmp_w2': nrm((2, CMP_HIDDEN, HEAD_DIM), CMP_HIDDEN ** -0.5),
        'w_mem_kv': nrm((DEPTH, D_MODEL, 2 * MEM_W), D_MODEL ** -0.5),
        'w_out': nrm((DEPTH, D_MODEL, D_MODEL), 0.5 * D_MODEL ** -0.5),
        'ffn_w_gate': nrm((N_DENSE, D_MODEL, D_FF), D_MODEL ** -0.5),
        'ffn_w_up': nrm((N_DENSE, D_MODEL, D_FF), D_MODEL ** -0.5),
        'ffn_w_down': nrm((N_DENSE, D_FF, D_MODEL), 0.5 * D_FF ** -0.5),
        'moe_router': nrm((N_MOE, D_MODEL, N_EXPERTS), D_MODEL ** -0.5),
        'moe_w_gate': nrm((N_MOE, N_EXPERTS, D_MODEL, D_EXPERT), D_MODEL ** -0.5),
        'moe_w_up': nrm((N_MOE, N_EXPERTS, D_MODEL, D_EXPERT), D_MODEL ** -0.5),
        'moe_w_down': nrm((N_MOE, N_EXPERTS, D_EXPERT, D_MODEL), 0.5 * D_EXPERT ** -0.5),
    }


def reference(x_prompt, x_sample, state_conv, cache_mem_kv, cache_cmp_kv, cache_slc_kv, cache_win_kv,
              page_table, mem_prompt, norm1, norm2, norm_final, w_in_a, conv_w, conv_b, conv_ln_g,
              conv_ln_b, w_in_b, kv_norm, w_kv_shared, cmp_pos, cmp_w1, cmp_w2, w_mem_kv, w_out,
              ffn_w_gate, ffn_w_up, ffn_w_down, moe_router, moe_w_gate, moe_w_up, moe_w_down):
    slopes = alibi_slopes(N_HEADS).reshape(N_KV_GROUPS, N_HEADS // N_KV_GROUPS)

    def project_shared(xs):
        b, t = xs.shape[:2]
        rows = (rmsnorm(xs, kv_norm) @ w_kv_shared).reshape(b, t, 3, 2, N_KV_GROUPS, HEAD_DIM)
        return rows[:, :, 0], rows[:, :, 1], rows[:, :, 2]

    def key_side(cmp_rows, slc_rows):
        ck = compress(cmp_rows[:, :, 0], cmp_pos[0], cmp_w1[0], cmp_w2[0])
        cv = compress(cmp_rows[:, :, 1], cmp_pos[1], cmp_w1[1], cmp_w2[1])
        sk, sv = split_blocks(slc_rows)
        return ck, cv, sk, sv

    def trunk(x, conv_bufs, mem_kvs, shared_fn, nsa_fn):
        b, t = x.shape[:2]
        new_conv = []
        ctx = None
        rows = None
        for l in range(DEPTH):
            h = rmsnorm(x, norm1[l])
            if l < N_A_LAYERS:
                u = h @ w_in_a[l]
                mix, buf = conv_mixer(u[..., :2 * C_CONV], conv_bufs[l], conv_w[l], conv_b[l],
                                      conv_ln_g[l], conv_ln_b[l])
                new_conv.append(buf)
                qm = u[..., 2 * C_CONV:]
            else:
                if ctx is None:
                    ctx, rows = shared_fn(x)
                u = h @ w_in_b[l - N_A_LAYERS]
                q = u[..., :NSA_W].reshape(b, t, N_HEADS, HEAD_DIM)
                g = u[..., NSA_W:NSA_W + 3 * N_HEADS].reshape(b, t, N_HEADS, 3)
                qm = u[..., NSA_W + 3 * N_HEADS:]
                mix = nsa_fn(q, g, ctx)
            mo = mem_attend(qm.reshape(b, t, MEM_HEADS, MEM_HEAD_DIM), mem_kvs[l])
            x = x + jnp.concatenate([mix, mo], axis=-1) @ w_out[l]
            h2 = rmsnorm(x, norm2[l])
            if l % 2 == 0:
                x = x + swiglu(h2, ffn_w_gate[l // 2], ffn_w_up[l // 2], ffn_w_down[l // 2])
            else:
                x = x + moe_ffn(h2, moe_router[l // 2], moe_w_gate[l // 2], moe_w_up[l // 2], moe_w_down[l // 2])
        return rmsnorm(x, norm_final), jnp.stack(new_conv), rows

    bp = x_prompt.shape[0]
    mem_kv_prompt = jnp.einsum('bmd,lde->lbme', mem_prompt, w_mem_kv).reshape(
        DEPTH, bp, N_MEM, 2, MEM_HEADS, MEM_HEAD_DIM)

    def prompt_shared(xs):
        cmp_rows, slc_rows, win_rows = project_shared(xs)
        ck, cv, sk, sv = key_side(cmp_rows, slc_rows)
        keep = min(WINDOW, xs.shape[1])
        return (ck, cv, sk, sv, win_rows[:, :, 0], win_rows[:, :, 1]), (cmp_rows, slc_rows, win_rows[:, -keep:])

    def prompt_nsa(q, g, ctx):
        ck, cv, sk, sv, wk, wv = ctx
        return nsa_prompt(q, g, ck, cv, sk, sv, wk, wv, slopes)

    conv0 = jnp.zeros((N_A_LAYERS, bp, CONV_K - 1, C_CONV), x_prompt.dtype)
    y_prompt, conv_prompt, rows_p = trunk(x_prompt, conv0, mem_kv_prompt, prompt_shared, prompt_nsa)
    cmp_kv_prompt, slc_kv_prompt, win_kv_prompt = rows_p

    n_pages = page_table.shape[1]
    past = n_pages * PAGE_SIZE
    win_buf = cache_win_kv.shape[1]

    def gather_pages(pool):
        pg = pool[page_table]
        return pg.reshape(pg.shape[0], past, 2, N_KV_GROUPS, HEAD_DIM)

    def sample_shared(xs):
        cmp_new, slc_new, win_new = project_shared(xs)
        ck, cv, sk, sv = key_side(jnp.concatenate([gather_pages(cache_cmp_kv), cmp_new], axis=1),
                                  jnp.concatenate([gather_pages(cache_slc_kv), slc_new], axis=1))
        win_all = jnp.concatenate([cache_win_kv.astype(win_new.dtype), win_new], axis=1)
        keep = min(WINDOW, past + xs.shape[1])
        return (ck, cv, sk, sv, win_all[:, :, 0], win_all[:, :, 1]), (cmp_new, slc_new, win_all[:, -keep:])

    def sample_nsa(q, g, ctx):
        ck, cv, sk, sv, wk, wv = ctx
        t = q.shape[1]
        qpos = past + jnp.arange(t)
        wpos = past - win_buf + jnp.arange(win_buf + t)
        return nsa_sample(q, g, ck, cv, sk, sv, wk, wv, qpos, wpos, slopes)

    y_sample, conv_sample, rows_s = trunk(x_sample, state_conv, cache_mem_kv, sample_shared, sample_nsa)
    cmp_kv_sample, slc_kv_sample, win_kv_sample = rows_s

    return (y_prompt, y_sample, conv_prompt, conv_sample, mem_kv_prompt, cmp_kv_prompt, cmp_kv_sample,
            slc_kv_prompt, slc_kv_sample, win_kv_prompt, win_kv_sample)
```

```python
import functools

import jax
import jax.numpy as jnp
from jax import lax
from jax.experimental import pallas as pl
from jax.experimental.pallas import tpu as pltpu

F32 = jnp.float32
BF16 = jnp.bfloat16

HEAD_DIM = 64
N_KV_GROUPS = 3
HEADS_PER_GROUP = 4
N_HEADS = N_KV_GROUPS * HEADS_PER_GROUP
NSA_W = N_HEADS * HEAD_DIM
GATE_W = 3 * N_HEADS
CMP_STRIDE = 16
CMP_BLOCK = 32
SLC_BLOCK = 64
SLC_SHIFT = 6
N_SEL = 16
WINDOW = 512
Q_BLOCK = 128
ALIBI_MAX_BIAS = 8.0
MEM_HEADS = 4
MEM_HEAD_DIM = 64
MEM_W = MEM_HEADS * MEM_HEAD_DIM
C_CONV = 768
CONV_K = 31
CONV_STATE = CONV_K - 1
N_EXPERTS = 8
EPS = 1e-6
PAGE_SIZE = 128
KV_ROW = 2 * N_KV_GROUPS * HEAD_DIM
CHUNK_ROW = CMP_STRIDE * KV_ROW
LANES = 128
NEG = -1e30
M_INIT = -1e29
VMEM_LIMIT = 56 * 1024 * 1024


def _dot(a, b):
    return jnp.dot(a, b, preferred_element_type=F32)


def _dot_nt(a, b):
    return lax.dot_general(a, b, (((1,), (1,)), ((), ())), preferred_element_type=F32)


def _split_hi_lo(x):
    hi = x.astype(BF16)
    lo = (x - hi.astype(F32)).astype(BF16)
    return hi, lo


def _rms(x, g):
    return x * lax.rsqrt(jnp.mean(x * x, axis=-1, keepdims=True) + EPS) * g


def _params(**kw):
    return pltpu.CompilerParams(vmem_limit_bytes=VMEM_LIMIT, **kw)


def _row_tile(n, pref):
    tm = pref
    while n % tm:
        tm //= 2
    assert tm >= 8, (n, pref)
    return tm


def _norm_proj_kernel(x_ref, g_ref, w_ref, *o_refs, splits):
    h = _rms(x_ref[...], g_ref[...])
    y = _dot(h.astype(BF16), w_ref[...])
    off = 0
    for o_ref, n in zip(o_refs, splits):
        o_ref[...] = y[:, off:off + n]
        off += n


def norm_proj(x, g, w, splits):
    n, d = x.shape
    tm = _row_tile(n, 512)
    return pl.pallas_call(
        functools.partial(_norm_proj_kernel, splits=splits),
        grid=(n // tm,),
        in_specs=[pl.BlockSpec((tm, d), lambda i: (i, 0)),
                  pl.BlockSpec((1, d), lambda i: (0, 0)),
                  pl.BlockSpec(w.shape, lambda i: (0, 0))],
        out_specs=[pl.BlockSpec((tm, s), lambda i: (i, 0)) for s in splits],
        out_shape=[jax.ShapeDtypeStruct((n, s), F32) for s in splits],
        compiler_params=_params(),
        name="norm_proj",
    )(x, g.reshape(1, d), w)


def _mem_kv_kernel(x_ref, w_ref, o_ref):
    o_ref[...] = _dot(x_ref[...].astype(BF16), w_ref[...].astype(BF16))


def mem_kv_proj(x, w):
    depth, d, e = w.shape
    n = x.shape[0]
    return pl.pallas_call(
        _mem_kv_kernel,
        grid=(depth,),
        in_specs=[pl.BlockSpec((n, d), lambda l: (0, 0)),
                  pl.BlockSpec((None, d, e), lambda l: (l, 0, 0))],
        out_specs=pl.BlockSpec((None, n, e), lambda l: (l, 0, 0)),
        out_shape=jax.ShapeDtypeStruct((depth, n, e), F32),
        compiler_params=_params(),
        name="mem_kv_proj",
    )(x, w)


CONV_PAD = 32


def _conv_kernel(u_ref, st_ref, w_ref, b_ref, lg_ref, lb_ref, mix_ref, ns_ref, vp_ref, *, tt, rc):
    t = pl.program_id(1)
    lo = CONV_PAD - CONV_STATE

    @pl.when(t == 0)
    def _():
        vp_ref[lo:CONV_PAD, :] = st_ref[...]

    u = u_ref[...]
    vp_ref[CONV_PAD:CONV_PAD + tt, :] = u[:, :C_CONV] * jax.nn.sigmoid(u[:, C_CONV:])
    for c in range(tt // rc):
        acc = jnp.zeros((rc, C_CONV), F32)
        for k in range(CONV_K):
            acc = acc + w_ref[k:k + 1, :] * vp_ref[pl.ds(lo + k + c * rc, rc), :]
        h = acc + b_ref[...]
        mu = jnp.mean(h, axis=-1, keepdims=True)
        hc = h - mu
        var = jnp.mean(hc * hc, axis=-1, keepdims=True)
        y = hc * lax.rsqrt(var + EPS) * lg_ref[...] + lb_ref[...]
        mix_ref[c * rc:(c + 1) * rc, :] = y * jax.nn.sigmoid(y)
    new_state = vp_ref[pl.ds(lo + tt, CONV_STATE), :]
    ns_ref[...] = new_state
    vp_ref[lo:CONV_PAD, :] = new_state


def conv_mixer(u, state, w, b, lg, lb, nseq, t_len, tt):
    rc = min(tt, 32)
    nt = t_len // tt
    vec = lambda a: a.reshape(1, C_CONV)
    return pl.pallas_call(
        functools.partial(_conv_kernel, tt=tt, rc=rc),
        grid=(nseq, nt),
        in_specs=[pl.BlockSpec((tt, 2 * C_CONV), lambda s, t: (s * nt + t, 0)),
                  pl.BlockSpec((None, CONV_STATE, C_CONV), lambda s, t: (s, 0, 0)),
                  pl.BlockSpec((CONV_K, C_CONV), lambda s, t: (0, 0)),
                  pl.BlockSpec((1, C_CONV), lambda s, t: (0, 0)),
                  pl.BlockSpec((1, C_CONV), lambda s, t: (0, 0)),
                  pl.BlockSpec((1, C_CONV), lambda s, t: (0, 0))],
        out_specs=[pl.BlockSpec((tt, C_CONV), lambda s, t: (s * nt + t, 0)),
                   pl.BlockSpec((None, CONV_STATE, C_CONV), lambda s, t: (s, 0, 0))],
        out_shape=[jax.ShapeDtypeStruct((nseq * t_len, C_CONV), F32),
                   jax.ShapeDtypeStruct((nseq, CONV_STATE, C_CONV), F32)],
        scratch_shapes=[pltpu.VMEM((CONV_PAD + tt, C_CONV), F32)],
        compiler_params=_params(),
        name="conv_mixer",
    )(u, state, w, vec(b), vec(lg), vec(lb))


def _mem_attn_kernel(q_ref, kv_ref, o_ref):
    q = q_ref[...]
    kv = kv_ref[...].astype(BF16)
    for h in range(MEM_HEADS):
        c = h * MEM_HEAD_DIM
        qh = q[:, c:c + MEM_HEAD_DIM].astype(BF16)
        s = _dot_nt(qh, kv[:, c:c + MEM_HEAD_DIM]) * (MEM_HEAD_DIM ** -0.5)
        e = jnp.exp(s - jnp.max(s, axis=-1, keepdims=True))
        p = e / jnp.sum(e, axis=-1, keepdims=True)
        o_ref[:, c:c + MEM_HEAD_DIM] = _dot(p.astype(BF16), kv[:, MEM_W + c:MEM_W + c + MEM_HEAD_DIM])


def mem_attend(qm, kv, nseq, t_len, tq):
    nt = t_len // tq
    n_mem = kv.shape[1]
    return pl.pallas_call(
        _mem_attn_kernel,
        grid=(nseq, nt),
        in_specs=[pl.BlockSpec((tq, MEM_W), lambda s, t: (s * nt + t, 0)),
                  pl.BlockSpec((None, n_mem, 2 * MEM_W), lambda s, t: (s, 0, 0))],
        out_specs=pl.BlockSpec((tq, MEM_W), lambda s, t: (s * nt + t, 0)),
        out_shape=jax.ShapeDtypeStruct((nseq * t_len, MEM_W), F32),
        compiler_params=_params(),
        name="mem_attend",
    )(qm, kv)


def _outproj_kernel(x_ref, mix_ref, mo_ref, w_ref, g_ref, *rest, moe):
    if moe:
        r_ref, xn_ref, h_ref, gate_ref = rest
    else:
        xn_ref, h_ref = rest
    d_mix = mix_ref.shape[1]
    y = _dot(mix_ref[...].astype(BF16), w_ref[0:d_mix, :]) + _dot(mo_ref[...].astype(BF16), w_ref[d_mix:, :])
    xn = x_ref[...] + y
    xn_ref[...] = xn
    h = _rms(xn, g_ref[...])
    h_ref[...] = h.astype(BF16)
    if moe:
        h_hi, h_lo = _split_hi_lo(h)
        r_hi, r_lo = _split_hi_lo(r_ref[...])
        logits = _dot(h_hi, r_hi) + _dot(h_lo, r_hi) + _dot(h_hi, r_lo)
        lane = lax.broadcasted_iota(jnp.int32, logits.shape, 1)
        logits = jnp.where(lane < N_EXPERTS, logits, -jnp.inf)
        m1 = jnp.max(logits, axis=-1, keepdims=True)
        i1 = jnp.min(jnp.where(logits == m1, lane, LANES), axis=-1, keepdims=True)
        rest_l = jnp.where(lane == i1, -jnp.inf, logits)
        m2 = jnp.max(rest_l, axis=-1, keepdims=True)
        i2 = jnp.min(jnp.where(rest_l == m2, lane, LANES), axis=-1, keepdims=True)
        e2 = jnp.exp(m2 - m1)
        den = 1.0 + e2
        gate_ref[...] = jnp.where(lane == i1, 1.0 / den, 0.0) + jnp.where(lane == i2, e2 / den, 0.0)


def outproj_norm(x, mix, mo, w, g, router=None):
    n, d = x.shape
    tm = _row_tile(n, 512)
    moe = router is not None
    row = lambda c: pl.BlockSpec((tm, c), lambda i: (i, 0))
    full = lambda a: pl.BlockSpec(a.shape, lambda i: (0, 0))
    ins = [x, mix, mo, w, g.reshape(1, d)]
    in_specs = [row(d), row(mix.shape[1]), row(mo.shape[1]), full(w), pl.BlockSpec((1, d), lambda i: (0, 0))]
    out_specs = [row(d), row(d)]
    out_shape = [jax.ShapeDtypeStruct((n, d), F32), jax.ShapeDtypeStruct((n, d), BF16)]
    if moe:
        rp = jnp.pad(router, ((0, 0), (0, LANES - router.shape[1])))
        ins.append(rp)
        in_specs.append(full(rp))
        out_specs.append(row(LANES))
        out_shape.append(jax.ShapeDtypeStruct((n, LANES), F32))
    return pl.pallas_call(
        functools.partial(_outproj_kernel, moe=moe),
        grid=(n // tm,),
        in_specs=in_specs, out_specs=out_specs, out_shape=out_shape,
        compiler_params=_params(),
        name="outproj_norm",
    )(*ins)


def _ffn_kernel(h_ref, x_ref, wg_ref, wu_ref, wd_ref, o_ref):
    @pl.when(pl.program_id(1) == 0)
    def _():
        o_ref[...] = x_ref[...]

    h = h_ref[...]
    a = _dot(h, wg_ref[...])
    u = _dot(h, wu_ref[...])
    act = (a * jax.nn.sigmoid(a) * u).astype(BF16)
    o_ref[...] += _dot(act, wd_ref[...])


def ffn_dense(h, x, wg, wu, wd, nf=2):
    n, d = x.shape
    tm = _row_tile(n, 512)
    tf = wg.shape[1] // nf
    return pl.pallas_call(
        _ffn_kernel,
        grid=(n // tm, nf),
        in_specs=[pl.BlockSpec((tm, d), lambda i, j: (i, 0)),
                  pl.BlockSpec((tm, d), lambda i, j: (i, 0)),
                  pl.BlockSpec((d, tf), lambda i, j: (0, j)),
                  pl.BlockSpec((d, tf), lambda i, j: (0, j)),
                  pl.BlockSpec((tf, d), lambda i, j: (j, 0))],
        out_specs=pl.BlockSpec((tm, d), lambda i, j: (i, 0)),
        out_shape=jax.ShapeDtypeStruct((n, d), F32),
        compiler_params=_params(),
        name="ffn_dense",
    )(h, x, wg, wu, wd)


def _moe_kernel(h_ref, x_ref, gate_ref, wg_ref, wu_ref, wd_ref, o_ref):
    e = pl.program_id(1)

    @pl.when((e == 0) & (pl.program_id(2) == 0))
    def _():
        o_ref[...] = x_ref[...]

    gate = gate_ref[...]
    lane = lax.broadcasted_iota(jnp.int32, gate.shape, 1)
    ge = jnp.sum(jnp.where(lane == e, gate, 0.0), axis=-1, keepdims=True)
    h = h_ref[...]
    a = _dot(h, wg_ref[...])
    u = _dot(h, wu_ref[...])
    act = (a * jax.nn.sigmoid(a) * u).astype(BF16)
    o_ref[...] += ge * _dot(act, wd_ref[...])


def moe_ffn(h, x, gate, wg, wu, wd, nf=4):
    n, d = x.shape
    tm = _row_tile(n, 1024)
    n_exp = wg.shape[0]
    tf = wg.shape[2] // nf
    return pl.pallas_call(
        _moe_kernel,
        grid=(n // tm, n_exp, nf),
        in_specs=[pl.BlockSpec((tm, d), lambda i, e, j: (i, 0)),
                  pl.BlockSpec((tm, d), lambda i, e, j: (i, 0)),
                  pl.BlockSpec((tm, LANES), lambda i, e, j: (i, 0)),
                  pl.BlockSpec((None, d, tf), lambda i, e, j: (e, 0, j)),
                  pl.BlockSpec((None, d, tf), lambda i, e, j: (e, 0, j)),
                  pl.BlockSpec((None, tf, d), lambda i, e, j: (e, j, 0))],
        out_specs=pl.BlockSpec((tm, d), lambda i, e, j: (i, 0)),
        out_shape=jax.ShapeDtypeStruct((n, d), F32),
        compiler_params=_params(),
        name="moe_ffn",
    )(h, x, gate, wg, wu, wd)


def _final_norm_kernel(x_ref, g_ref, o_ref):
    o_ref[...] = _rms(x_ref[...], g_ref[...])


def final_norm(x, g):
    n, d = x.shape
    tm = _row_tile(n, 1024)
    return pl.pallas_call(
        _final_norm_kernel,
        grid=(n // tm,),
        in_specs=[pl.BlockSpec((tm, d), lambda i: (i, 0)), pl.BlockSpec((1, d), lambda i: (0, 0))],
        out_specs=pl.BlockSpec((tm, d), lambda i: (i, 0)),
        out_shape=jax.ShapeDtypeStruct((n, d), F32),
        compiler_params=_params(),
        name="final_norm",
    )(x, g.reshape(1, d))


def _split_kv_kernel(s_ref, w_ref, sk_ref, sv_ref, wk_ref, wv_ref):
    for src, k_ref, v_ref in ((s_ref, sk_ref, sv_ref), (w_ref, wk_ref, wv_ref)):
        x = src[...].astype(BF16)
        for g in range(N_KV_GROUPS):
            k_ref[g] = x[:, g * HEAD_DIM:(g + 1) * HEAD_DIM]
            v_ref[g] = x[:, (N_KV_GROUPS + g) * HEAD_DIM:(N_KV_GROUPS + g + 1) * HEAD_DIM]


def split_kv(slc_rows, win_rows, nb, t_len, tm=512):
    nt = t_len // tm
    o_spec = pl.BlockSpec((None, N_KV_GROUPS, tm, HEAD_DIM), lambda b, t: (b, 0, t, 0))
    o_shape = jax.ShapeDtypeStruct((nb, N_KV_GROUPS, t_len, HEAD_DIM), BF16)
    return pl.pallas_call(
        _split_kv_kernel,
        grid=(nb, nt),
        in_specs=[pl.BlockSpec((tm, KV_ROW), lambda b, t: (b * nt + t, 0))] * 2,
        out_specs=[o_spec] * 4,
        out_shape=[o_shape] * 4,
        compiler_params=_params(),
        name="split_kv",
    )(slc_rows, win_rows)


def _cmp_first_layer(get_cols, kv, pe_ref, w1_ref):
    acc_a = None
    acc_b = None
    for p in range(CMP_STRIDE):
        xs = jnp.concatenate(
            [get_cols(p * KV_ROW + (kv * N_KV_GROUPS + g) * HEAD_DIM) for g in range(N_KV_GROUPS)], axis=0)
        q = p + CMP_STRIDE
        da = _dot((xs + pe_ref[kv, p:p + 1, :]).astype(BF16), w1_ref[kv, p * HEAD_DIM:(p + 1) * HEAD_DIM, :])
        db = _dot((xs + pe_ref[kv, q:q + 1, :]).astype(BF16), w1_ref[kv, q * HEAD_DIM:(q + 1) * HEAD_DIM, :])
        acc_a = da if acc_a is None else acc_a + da
        acc_b = db if acc_b is None else acc_b + db
    return acc_a, acc_b


def _cmp_second_layer(kv, nch, a_ref, b_ref, w2_ref, outs):
    for g in range(N_KV_GROUPS):
        hid = a_ref[g, 0:nch, :] + b_ref[g, pl.ds(1, nch), :]
        hid = hid * jax.nn.sigmoid(hid)
        outs[kv][g] = _dot(hid.astype(BF16), w2_ref[kv]).astype(BF16)


def _compress_prompt_kernel(x_ref, pe_ref, w1_ref, w2_ref, ck_ref, cv_ref, a_ref, b_ref, *, nch, tc):
    for kv in range(2):
        for ct in range(nch // tc):
            rows = slice(ct * tc, (ct + 1) * tc)
            acc_a, acc_b = _cmp_first_layer(lambda c0: x_ref[rows, c0:c0 + HEAD_DIM], kv, pe_ref, w1_ref)
            for g in range(N_KV_GROUPS):
                a_ref[g, rows, :] = acc_a[g * tc:(g + 1) * tc]
                b_ref[g, rows, :] = acc_b[g * tc:(g + 1) * tc]
        b_ref[:, nch:nch + 8, :] = jnp.zeros((N_KV_GROUPS, 8, b_ref.shape[2]), F32)
        _cmp_second_layer(kv, nch, a_ref, b_ref, w2_ref, (ck_ref, cv_ref))


def compress_prompt(cmp_rows, pe, w1, w2, nb, t_len):
    nch = t_len // CMP_STRIDE
    tc = min(nch, 128)
    hid = w1.shape[2]
    xc = cmp_rows.reshape(nb, nch, CHUNK_ROW)
    o_spec = pl.BlockSpec((None, N_KV_GROUPS, nch, HEAD_DIM), lambda b: (b, 0, 0, 0))
    o_shape = jax.ShapeDtypeStruct((nb, N_KV_GROUPS, nch, HEAD_DIM), BF16)
    full = lambda a: pl.BlockSpec(a.shape, lambda b: (0,) * a.ndim)
    return pl.pallas_call(
        functools.partial(_compress_prompt_kernel, nch=nch, tc=tc),
        grid=(nb,),
        in_specs=[pl.BlockSpec((None, nch, CHUNK_ROW), lambda b: (b, 0, 0)), full(pe), full(w1), full(w2)],
        out_specs=[o_spec, o_spec],
        out_shape=[o_shape, o_shape],
        scratch_shapes=[pltpu.VMEM((N_KV_GROUPS, nch + 8, hid), F32)] * 2,
        compiler_params=_params(),
        name="compress_prompt",
    )(xc, pe, w1, w2)


def _compress_sample_kernel(pt_ref, *refs, n_pages):
    page_refs = refs[:n_pages]
    new_ref, pe_ref, w1_ref, w2_ref, ck_ref, cv_ref, a_ref, b_ref = refs[n_pages:]
    nch = n_pages * (PAGE_SIZE // CMP_STRIDE)
    for kv in range(2):
        acc_a, acc_b = _cmp_first_layer(
            lambda c0: jnp.concatenate([pr[:, c0:c0 + HEAD_DIM] for pr in page_refs], axis=0), kv, pe_ref, w1_ref)
        _, new_b = _cmp_first_layer(
            lambda c0: jnp.broadcast_to(new_ref[:, c0:c0 + HEAD_DIM], (8, HEAD_DIM)), kv, pe_ref, w1_ref)
        for g in range(N_KV_GROUPS):
            a_ref[g, 0:nch, :] = acc_a[g * nch:(g + 1) * nch]
            b_ref[g, 0:nch, :] = acc_b[g * nch:(g + 1) * nch]
            b_ref[g, nch:nch + 8, :] = new_b[g * 8:(g + 1) * 8]
        _cmp_second_layer(kv, nch, a_ref, b_ref, w2_ref, (ck_ref, cv_ref))


def compress_sample(pool, page_table, cmp_new, pe, w1, w2):
    nseq, n_pages = page_table.shape
    dec = cmp_new.shape[1]
    cpp = PAGE_SIZE // CMP_STRIDE
    nch = n_pages * cpp
    hid = w1.shape[2]
    pool_c = pool.reshape(pool.shape[0], cpp, CHUNK_ROW)
    new_c = jnp.pad(cmp_new.reshape(nseq, 1, dec * KV_ROW), ((0, 0), (0, 0), (0, CHUNK_ROW - dec * KV_ROW)))
    page_spec = lambda p: pl.BlockSpec((None, cpp, CHUNK_ROW), lambda b, pt: (pt[b, p], 0, 0))
    full = lambda a: pl.BlockSpec(a.shape, lambda b, pt: (0,) * a.ndim)
    o_spec = pl.BlockSpec((None, N_KV_GROUPS, nch, HEAD_DIM), lambda b, pt: (b, 0, 0, 0))
    o_shape = jax.ShapeDtypeStruct((nseq, N_KV_GROUPS, nch, HEAD_DIM), BF16)
    return pl.pallas_call(
        functools.partial(_compress_sample_kernel, n_pages=n_pages),
        grid_spec=pltpu.PrefetchScalarGridSpec(
            num_scalar_prefetch=1,
            grid=(nseq,),
            in_specs=[page_spec(p) for p in range(n_pages)]
            + [pl.BlockSpec((None, 1, CHUNK_ROW), lambda b, pt: (b, 0, 0)), full(pe), full(w1), full(w2)],
            out_specs=[o_spec, o_spec],
            scratch_shapes=[pltpu.VMEM((N_KV_GROUPS, nch + 8, hid), F32)] * 2,
        ),
        out_shape=[o_shape, o_shape],
        compiler_params=_params(),
        name="compress_sample",
    )(page_table, *([pool_c] * n_pages), new_c, pe, w1, w2)


def _masked_softmax(s, mask):
    m = jnp.max(s, axis=-1, keepdims=True)
    e = jnp.where(mask, jnp.exp(s - m), 0.0)
    d = jnp.sum(e, axis=-1, keepdims=True)
    return e / jnp.where(d > 0, d, 1.0)


def _select_blocks(imp, cur):
    j = lax.broadcasted_iota(jnp.int32, imp.shape, 1)
    valid = j <= cur
    forced = valid & ((j == 0) | (j == cur) | (j == cur - 1))
    v = jnp.where(forced, jnp.inf, jnp.where(valid, imp, -jnp.inf))

    def pick_one(_, carry):
        v, sel = carry
        m = jnp.max(v, axis=-1, keepdims=True)
        jm = jnp.min(jnp.where(v == m, j, LANES), axis=-1, keepdims=True)
        pick = j == jm
        return jnp.where(pick, -jnp.inf, v), jnp.where(pick, 1.0, sel)

    _, sel = lax.fori_loop(0, N_SEL, pick_one, (v, jnp.zeros(imp.shape, F32)))
    return (sel > 0.0) & valid


def _stack_heads(q):
    return jnp.concatenate([q[:, r * HEAD_DIM:(r + 1) * HEAD_DIM] for r in range(HEADS_PER_GROUP)], axis=0)


def _importance(p_c, tq, mimp_ref):
    pcs = p_c[0:tq]
    for r in range(1, HEADS_PER_GROUP):
        pcs = pcs + p_c[r * tq:(r + 1) * tq]
    hi, lo = _split_hi_lo(pcs)
    return _dot(hi, mimp_ref[...]) + _dot(lo, mimp_ref[...])


def _gated_merge(gsig, g, o_c, o_s, o_w, tq, store):
    for r in range(HEADS_PER_GROUP):
        c = 3 * (g * HEADS_PER_GROUP + r)
        rows = slice(r * tq, (r + 1) * tq)
        store(r, gsig[:, c:c + 1] * o_c[rows] + gsig[:, c + 1:c + 2] * o_s[rows] + gsig[:, c + 2:c + 3] * o_w[rows])


SLC_TILE = 512


def _nsa_prompt_kernel(q_ref, gl_ref, qe_ref, slp_ref, ck_ref, cv_ref, sk_ref, sv_ref, wk_ref, wv_ref,
                       ek_ref, mimp_ref, o_ref, *, t_len):
    i = pl.program_id(2)
    tq = Q_BLOCK
    rows = HEADS_PER_GROUP * tq
    ncp = t_len // CMP_STRIDE
    s0 = i * tq
    q4 = (_stack_heads(q_ref[...]) * (HEAD_DIM ** -0.5)).astype(BF16)
    slope = slp_ref[...]
    qpos = s0 + (lax.broadcasted_iota(jnp.int32, (rows, 1), 0) & (tq - 1))

    cpos = lax.broadcasted_iota(jnp.int32, (1, ncp), 1) * CMP_STRIDE + (CMP_BLOCK - 1)
    mask_c = cpos <= qpos
    s_c = _dot_nt(q4, ck_ref[...]) - slope * (qpos - cpos).astype(F32)
    p_c = _masked_softmax(jnp.where(mask_c, s_c, NEG), mask_c)
    o_c = _dot(p_c.astype(BF16), cv_ref[...])

    imp = _importance(p_c, tq, mimp_ref)
    cur = (s0 + lax.broadcasted_iota(jnp.int32, (tq, 1), 0)) >> SLC_SHIFT
    sel = _select_blocks(imp, cur)
    jb =lax.broadcasted_iota(jnp.int32, sel.shape, 1)

    selbias = jnp.where(sel & (jb < 2 * i), 0.0, NEG).astype(BF16)
    qa = jnp.concatenate([jnp.concatenate([selbias] * HEADS_PER_GROUP, axis=0), qe_ref[...]], axis=1)

    def flash_update(carry, s, v):
        m, l, acc = carry
        m_new = jnp.maximum(m, jnp.max(s, axis=-1, keepdims=True))
        a = jnp.exp(m - m_new)
        p = jnp.exp(s - m_new)
        return m_new, a * l + jnp.sum(p, axis=-1, keepdims=True), a * acc + _dot(p.astype(BF16), v)

    def bulk(kt, carry):
        ks = pl.ds(pl.multiple_of(kt * SLC_TILE, SLC_TILE), SLC_TILE)
        s = _dot_nt(q4, sk_ref[ks, :]) + _dot_nt(qa, ek_ref[ks, :])
        return flash_update(carry, s, sv_ref[ks, :])

    carry = (jnp.full((rows, 1), M_INIT, F32), jnp.zeros((rows, 1), F32), jnp.zeros((rows, HEAD_DIM), F32))
    carry = lax.fori_loop(0, (s0 + SLC_TILE - 1) // SLC_TILE, bulk, carry)
    kd = pl.ds(pl.multiple_of(s0, tq), tq)
    kpos = s0 + lax.broadcasted_iota(jnp.int32, (1, tq), 1)
    s_d = _dot_nt(q4, sk_ref[kd, :]) + slope * kpos.astype(F32)
    _, l_s, acc_s = flash_update(carry, jnp.where(kpos <= qpos, s_d, NEG), sv_ref[kd, :])
    o_s = acc_s / l_s

    wlen = WINDOW + tq
    ws = jnp.maximum(s0 - WINDOW, 0)
    kw = pl.ds(pl.multiple_of(ws, tq), wlen)
    dist = qpos - (ws + lax.broadcasted_iota(jnp.int32, (1, wlen), 1))
    mask_w = (dist >= 0) & (dist < WINDOW)
    s_w = _dot_nt(q4, wk_ref[kw, :]) - slope * dist.astype(F32)
    p_w = _masked_softmax(jnp.where(mask_w, s_w, NEG), mask_w)
    o_w = _dot(p_w.astype(BF16), wv_ref[kw, :])

    def store(r, val):
        o_ref[:, r * HEAD_DIM:(r + 1) * HEAD_DIM] = val

    _gated_merge(jax.nn.sigmoid(gl_ref[...]), 0, o_c, o_s, o_w, tq, store)


def nsa_prompt(q, gl_g, ck, cv, sk, sv, wk, wv, consts, nb, t_len):
    nqb = t_len // Q_BLOCK
    rows = HEADS_PER_GROUP * Q_BLOCK
    ncp = t_len // CMP_STRIDE
    gw = 3 * HEADS_PER_GROUP
    qe, slp, ek, mimp = consts
    kv_spec = lambda n: pl.BlockSpec((None, None, n, HEAD_DIM), lambda b, g, i: (b, g, 0, 0))
    return pl.pallas_call(
        functools.partial(_nsa_prompt_kernel, t_len=t_len),
        grid=(nb, N_KV_GROUPS, nqb),
        in_specs=[pl.BlockSpec((Q_BLOCK, HEADS_PER_GROUP * HEAD_DIM), lambda b, g, i: (b * nqb + i, g)),
                  pl.BlockSpec((None, Q_BLOCK, gw), lambda b, g, i: (g, b * nqb + i, 0)),
                  pl.BlockSpec((None, rows, LANES), lambda b, g, i: (g, 0, 0)),
                  pl.BlockSpec((None, rows, 1), lambda b, g, i: (g, 0, 0)),
                  kv_spec(ncp), kv_spec(ncp), kv_spec(t_len), kv_spec(t_len), kv_spec(t_len), kv_spec(t_len),
                  pl.BlockSpec(ek.shape, lambda b, g, i: (0, 0)),
                  pl.BlockSpec(mimp.shape, lambda b, g, i: (0, 0))],
        out_specs=pl.BlockSpec((Q_BLOCK, HEADS_PER_GROUP * HEAD_DIM), lambda b, g, i: (b * nqb + i, g)),
        out_shape=jax.ShapeDtypeStruct((nb * t_len, NSA_W), F32),
        compiler_params=_params(),
        name="nsa_prompt",
    )(q, gl_g, qe, slp, ck, cv, sk, sv, wk, wv, ek, mimp)


def _nsa_sample_kernel(pt_ref, *refs, n_pages, dec, win_buf):
    page_refs = refs[:n_pages]
    (q_ref, gl_ref, slp_ref, ck_ref, cv_ref, snew_ref, wc_ref, wnew_ref, es_ref, mimp_ref, o_ref) = refs[n_pages:]
    tq = dec
    rows = HEADS_PER_GROUP * tq
    past = n_pages * PAGE_SIZE
    ncs = ck_ref.shape[1]
    t_row = lax.broadcasted_iota(jnp.int32, (rows, 1), 0) & (tq - 1)
    qpos = past + t_row
    gsig = jax.nn.sigmoid(gl_ref[...])
    q_all = q_ref[...]
    snew = snew_ref[...]
    wc = wc_ref[...]
    wnew = wnew_ref[...]
    tnew = lax.broadcasted_iota(jnp.int32, (1, tq), 1)
    mask_new = tnew <= t_row
    dist_new = (t_row - tnew).astype(F32)

    for g in range(N_KV_GROUPS):
        kc = slice(g * HEAD_DIM, (g + 1) * HEAD_DIM)
        vc = slice((N_KV_GROUPS + g) * HEAD_DIM, (N_KV_GROUPS + g + 1) * HEAD_DIM)
        q4 = (_stack_heads(q_all[:, g * HEADS_PER_GROUP * HEAD_DIM:(g + 1) * HEADS_PER_GROUP * HEAD_DIM])
              * (HEAD_DIM ** -0.5)).astype(BF16)
        slope = slp_ref[g]

        cpos = lax.broadcasted_iota(jnp.int32, (1, ncs), 1) * CMP_STRIDE + (CMP_BLOCK - 1)
        mask_c = cpos <= qpos
        s_c = _dot_nt(q4, ck_ref[g]) - slope * (qpos - cpos).astype(F32)
        p_c = _masked_softmax(jnp.where(mask_c, s_c, NEG), mask_c)
        o_c = _dot(p_c.astype(BF16), cv_ref[g])

        imp = _importance(p_c, tq, mimp_ref)
        cur = (past + lax.broadcasted_iota(jnp.int32, (tq, 1), 0)) >> SLC_SHIFT
        sel = _select_blocks(imp, cur)
        selbias = jnp.where(sel, 0.0, NEG).astype(BF16)
        sb4 = jnp.concatenate([selbias] * HEADS_PER_GROUP, axis=0)

        k_past = jnp.concatenate([pr[:, kc] for pr in page_refs], axis=0).astype(BF16)
        v_past = jnp.concatenate([pr[:, vc] for pr in page_refs], axis=0).astype(BF16)
        kpos = lax.broadcasted_iota(jnp.int32, (1, past), 1)
        s_p = _dot_nt(q4, k_past) + _dot_nt(sb4, es_ref[...]) - slope * (qpos - kpos).astype(F32)
        s_n = _dot_nt(q4, snew[:, kc].astype(BF16)) - slope * dist_new
        s_n = jnp.where(mask_new, s_n, NEG)
        m = jnp.maximum(jnp.max(s_p, axis=-1, keepdims=True), jnp.max(s_n, axis=-1, keepdims=True))
        e_p = jnp.exp(s_p - m)
        e_n = jnp.exp(s_n - m)
        l = jnp.sum(e_p, axis=-1, keepdims=True) + jnp.sum(e_n, axis=-1, keepdims=True)
        o_s = (_dot(e_p.astype(BF16), v_past) + _dot(e_n.astype(BF16), snew[:, vc].astype(BF16))) / l

        wdist = qpos - (past - win_buf + lax.broadcasted_iota(jnp.int32, (1, win_buf), 1))
        mask_w = wdist < WINDOW
        s_w = _dot_nt(q4, wc[:, kc].astype(BF16)) - slope * wdist.astype(F32)
        s_w = jnp.where(mask_w, s_w, NEG)
        s_wn = _dot_nt(q4, wnew[:, kc].astype(BF16)) - slope * dist_new
        s_wn = jnp.where(mask_new, s_wn, NEG)
        m = jnp.maximum(jnp.max(s_w, axis=-1, keepdims=True), jnp.max(s_wn, axis=-1, keepdims=True))
        e_w = jnp.exp(s_w - m)
        e_wn = jnp.exp(s_wn - m)
        l = jnp.sum(e_w, axis=-1, keepdims=True) + jnp.sum(e_wn, axis=-1, keepdims=True)
        o_w = (_dot(e_w.astype(BF16), wc[:, vc].astype(BF16)) + _dot(e_wn.astype(BF16), wnew[:, vc].astype(BF16))) / l

        def store(r, val, g=g):
            c = (g * HEADS_PER_GROUP + r) * HEAD_DIM
            o_ref[:, c:c + HEAD_DIM] = val

        _gated_merge(gsig, g, o_c, o_s, o_w, tq, store)


def nsa_sample(q, gl, ck, cv, pool, page_table, slc_rows, win_cache, win_rows, consts, row_off, dec):
    nseq, n_pages = page_table.shape
    win_buf = win_cache.shape[1]
    slp, es, mimp = consts
    full = lambda a: pl.BlockSpec(a.shape, lambda b, pt: (0,) * a.ndim)
    row = lambda c: pl.BlockSpec((dec, c), lambda b, pt: (row_off + b, 0))
    seq4 = lambda a: pl.BlockSpec((None,) + a.shape[1:], lambda b, pt: (b, 0, 0, 0))
    page_spec = lambda p: pl.BlockSpec((None, PAGE_SIZE, KV_ROW), lambda b, pt: (pt[b, p], 0, 0))
    return pl.pallas_call(
        functools.partial(_nsa_sample_kernel, n_pages=n_pages, dec=dec, win_buf=win_buf),
        grid_spec=pltpu.PrefetchScalarGridSpec(
            num_scalar_prefetch=1,
            grid=(nseq,),
            in_specs=[page_spec(p) for p in range(n_pages)]
            + [row(NSA_W), row(GATE_W), full(slp), seq4(ck), seq4(cv), row(KV_ROW),
               pl.BlockSpec((None, win_buf, KV_ROW), lambda b, pt: (b, 0, 0)), row(KV_ROW), full(es), full(mimp)],
            out_specs=pl.BlockSpec((dec, NSA_W), lambda b, pt: (b, 0)),
        ),
        out_shape=jax.ShapeDtypeStruct((nseq * dec, NSA_W), F32),
        compiler_params=_params(),
        name="nsa_sample",
    )(page_table, *([pool] * n_pages), q, gl, slp, ck, cv, slc_rows, win_cache, win_rows, es, mimp)


def _alibi_slopes():
    return 2.0 ** (-ALIBI_MAX_BIAS * jnp.arange(1, N_HEADS + 1, dtype=F32) / N_HEADS)


def _slope_rows(tq):
    return jnp.repeat(_alibi_slopes().reshape(N_KV_GROUPS, HEADS_PER_GROUP), tq, axis=1)[..., None]


def _importance_matrix(n_cmp):
    n = jnp.arange(n_cmp)[:, None]
    j = jnp.arange(LANES)[None, :]
    per_sel = SLC_BLOCK // CMP_STRIDE
    inside = (SLC_BLOCK - CMP_BLOCK) // CMP_STRIDE + 1
    return ((n // per_sel == j) & (n % per_sel < inside)).astype(BF16)


def _block_onehot(n_keys):
    k = jnp.arange(n_keys)[:, None]
    return (k // SLC_BLOCK == jnp.arange(LANES)[None, :]).astype(BF16)


def _prompt_tables(t_len):
    tq = Q_BLOCK
    slp = _slope_rows(tq)
    s1 = slp.astype(BF16)
    s2 = (slp - s1.astype(F32)).astype(BF16)
    s3 = (slp - s1.astype(F32) - s2.astype(F32)).astype(BF16)
    qe = jnp.concatenate([s1, s2, s3, s1, s2, s3], axis=-1)
    qe = jnp.pad(qe, ((0, 0), (0, 0), (0, LANES - qe.shape[-1])))
    k = jnp.arange(t_len)[:, None]
    hi = ((k // SLC_BLOCK) * SLC_BLOCK).astype(BF16)
    lo = (k % SLC_BLOCK).astype(BF16)
    ke = jnp.concatenate([hi, hi, hi, lo, lo, lo], axis=-1)
    ke = jnp.pad(ke, ((0, 0), (0, LANES - ke.shape[-1])))
    ek = jnp.concatenate([_block_onehot(t_len), ke], axis=-1)
    return qe, slp, ek, _importance_matrix(t_len // CMP_STRIDE)


def kernel(x_prompt, x_sample, state_conv, cache_mem_kv, cache_cmp_kv, cache_slc_kv, cache_win_kv, page_table,
           mem_prompt, norm1, norm2, norm_final, w_in_a, conv_w, conv_b, conv_ln_g, conv_ln_b, w_in_b, kv_norm,
           w_kv_shared, cmp_pos, cmp_w1, cmp_w2, w_mem_kv, w_out, ffn_w_gate, ffn_w_up, ffn_w_down, moe_router,
           moe_w_gate, moe_w_up, moe_w_down):
    bp, t_len, d = x_prompt.shape
    db, dec, _ = x_sample.shape
    depth = norm1.shape[0]
    n_a = w_in_a.shape[0]
    n_mem = mem_prompt.shape[1]
    n_p = bp * t_len
    n_s = db * dec
    n_pages = page_table.shape[1]
    past = n_pages * PAGE_SIZE
    win_buf = cache_win_kv.shape[1]
    bf = lambda a: a.astype(BF16)

    x = jnp.concatenate([x_prompt.reshape(n_p, d), x_sample.reshape(n_s, d)], axis=0)

    mem_kv_p = mem_kv_proj(mem_prompt.reshape(bp * n_mem, d), w_mem_kv)
    mem_kv_p = mem_kv_p.reshape(depth, bp, n_mem, 2 * MEM_W)
    mem_kv_s = cache_mem_kv.reshape(depth, db, n_mem, 2 * MEM_W)

    conv_p, conv_s = [], []
    zero_state = jnp.zeros((bp, CONV_STATE, C_CONV), F32)
    cmp_rows = slc_rows = win_rows = None
    nsa_ctx = None

    for l in range(depth):
        if l < n_a:
            uc, qm = norm_proj(x, norm1[l], bf(w_in_a[l]), (2 * C_CONV, MEM_W))
            mix_p, st_p = conv_mixer(uc[:n_p], zero_state, conv_w[l], conv_b[l], conv_ln_g[l], conv_ln_b[l],
                                     bp, t_len, Q_BLOCK)
            mix_s, st_s = conv_mixer(uc[n_p:], state_conv[l], conv_w[l], conv_b[l], conv_ln_g[l], conv_ln_b[l],
                                     db, dec, dec)
            conv_p.append(st_p)
            conv_s.append(st_s)
        else:
            if nsa_ctx is None:
                cmp_rows, slc_rows, win_rows = norm_proj(x, kv_norm, bf(w_kv_shared), (KV_ROW,) * 3)
                w1 = bf(cmp_w1)
                w2 = bf(cmp_w2)
                ck_p, cv_p = compress_prompt(cmp_rows[:n_p], cmp_pos, w1, w2, bp, t_len)
                ck_s, cv_s = compress_sample(cache_cmp_kv.reshape(-1, PAGE_SIZE, KV_ROW), page_table,
                                             cmp_rows[n_p:].reshape(db, dec, KV_ROW), cmp_pos, w1, w2)
                sk, sv, wk, wv = split_kv(slc_rows[:n_p], win_rows[:n_p], bp, t_len)
                nsa_ctx = dict(
                    prompt=_prompt_tables(t_len),
                    sample=(_slope_rows(dec), _block_onehot(past), _importance_matrix(ck_s.shape[2])),
                    pool=cache_slc_kv.reshape(-1, PAGE_SIZE, KV_ROW),
                    win_cache=cache_win_kv.reshape(db, win_buf, KV_ROW))
            w_in = w_in_b[l - n_a]
            w_in = jnp.concatenate([w_in[:, :NSA_W], w_in[:, NSA_W + GATE_W:], w_in[:, NSA_W:NSA_W + GATE_W]], axis=1)
            q, qm, gl = norm_proj(x, norm1[l], bf(w_in), (NSA_W, MEM_W, GATE_W))
            gw = 3 * HEADS_PER_GROUP
            gl_g = jnp.transpose(gl[:n_p].reshape(n_p, N_KV_GROUPS, gw), (1, 0, 2))
            mix_p = nsa_prompt(q, gl_g, ck_p, cv_p, sk, sv, wk, wv, nsa_ctx["prompt"], bp, t_len)
            mix_s = nsa_sample(q, gl, ck_s, cv_s, nsa_ctx["pool"], page_table, slc_rows, nsa_ctx["win_cache"],
                               win_rows, nsa_ctx["sample"], n_p // dec, dec)
        mo_p = mem_attend(qm[:n_p], mem_kv_p[l], bp, t_len, 512)
        mo_s = mem_attend(qm[n_p:], mem_kv_s[l], db, dec, dec)
        mix = jnp.concatenate([mix_p, mix_s], axis=0)
        mo = jnp.concatenate([mo_p, mo_s], axis=0)
        if l % 2 == 0:
            xn, h2 = outproj_norm(x, mix, mo, bf(w_out[l]), norm2[l])
            x = ffn_dense(h2, xn, bf(ffn_w_gate[l // 2]), bf(ffn_w_up[l // 2]), bf(ffn_w_down[l // 2]))
        else:
            xn, h2, gate = outproj_norm(x, mix, mo, bf(w_out[l]), norm2[l], router=moe_router[l // 2])
            x = moe_ffn(h2, xn, gate, bf(moe_w_gate[l // 2]), bf(moe_w_up[l // 2]), bf(moe_w_down[l // 2]))

    y = final_norm(x, norm_final)
    kv5 = lambda a, b, t: a.reshape(b, t, 2, N_KV_GROUPS, HEAD_DIM)
    keep_p = min(WINDOW, t_len)
    win_p = kv5(win_rows[:n_p], bp, t_len)[:, t_len - keep_p:]
    win_all = jnp.concatenate([cache_win_kv, kv5(win_rows[n_p:], db, dec)], axis=1)
    keep_s = min(WINDOW, past + dec)
    return (y[:n_p].reshape(bp, t_len, d), y[n_p:].reshape(db, dec, d),
            jnp.stack(conv_p), jnp.stack(conv_s),
            mem_kv_p.reshape(depth, bp, n_mem, 2, MEM_HEADS, MEM_HEAD_DIM),
            kv5(cmp_rows[:n_p], bp, t_len), kv5(cmp_rows[n_p:], db, dec),
            kv5(slc_rows[:n_p], bp, t_len), kv5(slc_rows[n_p:], db, dec),
            win_p, win_all[:, win_all.shape[1] - keep_s:])
```

```python
import functools

import jax
import jax.numpy as jnp
from jax import lax
from jax.experimental import pallas as pl
from jax.experimental.pallas import tpu as pltpu

F32 = jnp.float32
BF16 = jnp.bfloat16

HEAD_DIM = 64
N_KV_GROUPS = 3
HEADS_PER_GROUP = 4
N_HEADS = N_KV_GROUPS * HEADS_PER_GROUP
NSA_W = N_HEADS * HEAD_DIM
GATE_W = 3 * N_HEADS
CMP_STRIDE = 16
CMP_BLOCK = 32
SLC_BLOCK = 64
SLC_SHIFT = 6
N_SEL = 16
WINDOW = 512
Q_BLOCK = 128
ALIBI_MAX_BIAS = 8.0
MEM_HEADS = 4
MEM_HEAD_DIM = 64
MEM_W = MEM_HEADS * MEM_HEAD_DIM
C_CONV = 768
CONV_K = 31
CONV_STATE = CONV_K - 1
N_EXPERTS = 8
EPS = 1e-6
PAGE_SIZE = 128
KV_ROW = 2 * N_KV_GROUPS * HEAD_DIM
CHUNK_ROW = CMP_STRIDE * KV_ROW
LANES = 128
NEG = -1e30
M_INIT = -1e29
VMEM_LIMIT = 56 * 1024 * 1024


def _dot(a, b):
    return jnp.dot(a, b, preferred_element_type=F32)


def _dot_nt(a, b):
    return lax.dot_general(a, b, (((1,), (1,)), ((), ())), preferred_element_type=F32)


def _split_hi_lo(x):
    hi = x.astype(BF16)
    lo = (x - hi.astype(F32)).astype(BF16)
    return hi, lo


def _rms(x, g):
    return x * lax.rsqrt(jnp.mean(x * x, axis=-1, keepdims=True) + EPS) * g


def _params(**kw):
    return pltpu.CompilerParams(vmem_limit_bytes=VMEM_LIMIT, **kw)


def _row_tile(n, pref):
    tm = pref
    while n % tm:
        tm //= 2
    assert tm >= 8, (n, pref)
    return tm


def _norm_proj_kernel(x_ref, g_ref, w_ref, *o_refs, splits):
    h = _rms(x_ref[...], g_ref[...])
    y = _dot(h.astype(BF16), w_ref[...])
    off = 0
    for o_ref, n in zip(o_refs, splits):
        o_ref[...] = y[:, off:off + n]
        off += n


def norm_proj(x, g, w, splits):
    n, d = x.shape
    tm = _row_tile(n, 512)
    return pl.pallas_call(
        functools.partial(_norm_proj_kernel, splits=splits),
        grid=(n // tm,),
        in_specs=[pl.BlockSpec((tm, d), lambda i: (i, 0)),
                  pl.BlockSpec((1, d), lambda i: (0, 0)),
                  pl.BlockSpec(w.shape, lambda i: (0, 0))],
        out_specs=[pl.BlockSpec((tm, s), lambda i: (i, 0)) for s in splits],
        out_shape=[jax.ShapeDtypeStruct((n, s), F32) for s in splits],
        compiler_params=_params(),
        name="norm_proj",
    )(x, g.reshape(1, d), w)


def _mem_kv_kernel(x_ref, w_ref, o_ref):
    o_ref[...] = _dot(x_ref[...].astype(BF16), w_ref[...].astype(BF16))


def mem_kv_proj(x, w):
    depth, d, e = w.shape
    n = x.shape[0]
    return pl.pallas_call(
        _mem_kv_kernel,
        grid=(depth,),
        in_specs=[pl.BlockSpec((n, d), lambda l: (0, 0)),
                  pl.BlockSpec((None, d, e), lambda l: (l, 0, 0))],
        out_specs=pl.BlockSpec((None, n, e), lambda l: (l, 0, 0)),
        out_shape=jax.ShapeDtypeStruct((depth, n, e), F32),
        compiler_params=_params(),
        name="mem_kv_proj",
    )(x, w)


CONV_PAD = 32


def _conv_kernel(u_ref, st_ref, w_ref, b_ref, lg_ref, lb_ref, mix_ref, ns_ref, vp_ref, *, tt, rc):
    t = pl.program_id(1)
    lo = CONV_PAD - CONV_STATE

    @pl.when(t == 0)
    def _():
        vp_ref[lo:CONV_PAD, :] = st_ref[...]

    u = u_ref[...]
    vp_ref[CONV_PAD:CONV_PAD + tt, :] = u[:, :C_CONV] * jax.nn.sigmoid(u[:, C_CONV:])
    for c in range(tt // rc):
        acc = jnp.zeros((rc, C_CONV), F32)
        for k in range(CONV_K):
            acc = acc + w_ref[k:k + 1, :] * vp_ref[pl.ds(lo + k + c * rc, rc), :]
        h = acc + b_ref[...]
        mu = jnp.mean(h, axis=-1, keepdims=True)
        hc = h - mu
        var = jnp.mean(hc * hc, axis=-1, keepdims=True)
        y = hc * lax.rsqrt(var + EPS) * lg_ref[...] + lb_ref[...]
        mix_ref[c * rc:(c + 1) * rc, :] = y * jax.nn.sigmoid(y)
    new_state = vp_ref[pl.ds(lo + tt, CONV_STATE), :]
    ns_ref[...] = new_state
    vp_ref[lo:CONV_PAD, :] = new_state


def conv_mixer(u, state, w, b, lg, lb, nseq, t_len, tt):
    rc = min(tt, 32)
    nt = t_len // tt
    vec = lambda a: a.reshape(1, C_CONV)
    return pl.pallas_call(
        functools.partial(_conv_kernel, tt=tt, rc=rc),
        grid=(nseq, nt),
        in_specs=[pl.BlockSpec((tt, 2 * C_CONV), lambda s, t: (s * nt + t, 0)),
                  pl.BlockSpec((None, CONV_STATE, C_CONV), lambda s, t: (s, 0, 0)),
                  pl.BlockSpec((CONV_K, C_CONV), lambda s, t: (0, 0)),
                  pl.BlockSpec((1, C_CONV), lambda s, t: (0, 0)),
                  pl.BlockSpec((1, C_CONV), lambda s, t: (0, 0)),
                  pl.BlockSpec((1, C_CONV), lambda s, t: (0, 0))],
        out_specs=[pl.BlockSpec((tt, C_CONV), lambda s, t: (s * nt + t, 0)),
                   pl.BlockSpec((None, CONV_STATE, C_CONV), lambda s, t: (s, 0, 0))],
        out_shape=[jax.ShapeDtypeStruct((nseq * t_len, C_CONV), F32),
                   jax.ShapeDtypeStruct((nseq, CONV_STATE, C_CONV), F32)],
        scratch_shapes=[pltpu.VMEM((CONV_PAD + tt, C_CONV), F32)],
        compiler_params=_params(),
        name="conv_mixer",
    )(u, state, w, vec(b), vec(lg), vec(lb))


def _mem_attn_kernel(q_ref, kv_ref, o_ref):
    q = q_ref[...]
    kv = kv_ref[...].astype(BF16)
    for h in range(MEM_HEADS):
        c = h * MEM_HEAD_DIM
        qh = q[:, c:c + MEM_HEAD_DIM].astype(BF16)
        s = _dot_nt(qh, kv[:, c:c + MEM_HEAD_DIM]) * (MEM_HEAD_DIM ** -0.5)
        e = jnp.exp(s - jnp.max(s, axis=-1, keepdims=True))
        p = e / jnp.sum(e, axis=-1, keepdims=True)
        o_ref[:, c:c + MEM_HEAD_DIM] = _dot(p.astype(BF16), kv[:, MEM_W + c:MEM_W + c + MEM_HEAD_DIM])


def mem_attend(qm, kv, nseq, t_len, tq):
    nt = t_len // tq
    n_mem = kv.shape[1]
    return pl.pallas_call(
        _mem_attn_kernel,
        grid=(nseq, nt),
        in_specs=[pl.BlockSpec((tq, MEM_W), lambda s, t: (s * nt + t, 0)),
                  pl.BlockSpec((None, n_mem, 2 * MEM_W), lambda s, t: (s, 0, 0))],
        out_specs=pl.BlockSpec((tq, MEM_W), lambda s, t: (s * nt + t, 0)),
        out_shape=jax.ShapeDtypeStruct((nseq * t_len, MEM_W), F32),
        compiler_params=_params(),
        name="mem_attend",
    )(qm, kv)


def _outproj_kernel(x_ref, mix_ref, mo_ref, w_ref, g_ref, *rest, moe):
    if moe:
        r_ref, xn_ref, h_ref, gate_ref = rest
    else:
        xn_ref, h_ref = rest
    d_mix = mix_ref.shape[1]
    y = _dot(mix_ref[...].astype(BF16), w_ref[0:d_mix, :]) + _dot(mo_ref[...].astype(BF16), w_ref[d_mix:, :])
    xn = x_ref[...] + y
    xn_ref[...] = xn
    h = _rms(xn, g_ref[...])
    h_ref[...] = h.astype(BF16)
    if moe:
        h_hi, h_lo = _split_hi_lo(h)
        r_hi, r_lo = _split_hi_lo(r_ref[...])
        logits = _dot(h_hi, r_hi) + _dot(h_lo, r_hi) + _dot(h_hi, r_lo)
        lane = lax.broadcasted_iota(jnp.int32, logits.shape, 1)
        logits = jnp.where(lane < N_EXPERTS, logits, -jnp.inf)
        m1 = jnp.max(logits, axis=-1, keepdims=True)
        i1 = jnp.min(jnp.where(logits == m1, lane, LANES), axis=-1, keepdims=True)
        rest_l = jnp.where(lane == i1, -jnp.inf, logits)
        m2 = jnp.max(rest_l, axis=-1, keepdims=True)
        i2 = jnp.min(jnp.where(rest_l == m2, lane, LANES), axis=-1, keepdims=True)
        e2 = jnp.exp(m2 - m1)
        den = 1.0 + e2
        gate_ref[...] = jnp.where(lane == i1, 1.0 / den, 0.0) + jnp.where(lane == i2, e2 / den, 0.0)


def outproj_norm(x, mix, mo, w, g, router=None):
    n, d = x.shape
    tm = _row_tile(n, 512)
    moe = router is not None
    row = lambda c: pl.BlockSpec((tm, c), lambda i: (i, 0))
    full = lambda a: pl.BlockSpec(a.shape, lambda i: (0, 0))
    ins = [x, mix, mo, w, g.reshape(1, d)]
    in_specs = [row(d), row(mix.shape[1]), row(mo.shape[1]), full(w), pl.BlockSpec((1, d), lambda i: (0, 0))]
    out_specs = [row(d), row(d)]
    out_shape = [jax.ShapeDtypeStruct((n, d), F32), jax.ShapeDtypeStruct((n, d), BF16)]
    if moe:
        rp = jnp.pad(router, ((0, 0), (0, LANES - router.shape[1])))
        ins.append(rp)
        in_specs.append(full(rp))
        out_specs.append(row(LANES))
        out_shape.append(jax.ShapeDtypeStruct((n, LANES), F32))
    return pl.pallas_call(
        functools.partial(_outproj_kernel, moe=moe),
        grid=(n // tm,),
        in_specs=in_specs, out_specs=out_specs, out_shape=out_shape,
        compiler_params=_params(),
        name="outproj_norm",
    )(*ins)


def _ffn_kernel(h_ref, x_ref, wg_ref, wu_ref, wd_ref, o_ref):
    @pl.when(pl.program_id(1) == 0)
    def _():
        o_ref[...] = x_ref[...]

    h = h_ref[...]
    a = _dot(h, wg_ref[...])
    u = _dot(h, wu_ref[...])
    act = (a * jax.nn.sigmoid(a) * u).astype(BF16)
    o_ref[...] += _dot(act, wd_ref[...])


def ffn_dense(h, x, wg, wu, wd, nf=2):
    n, d = x.shape
    tm = _row_tile(n, 512)
    tf = wg.shape[1] // nf
    return pl.pallas_call(
        _ffn_kernel,
        grid=(n // tm, nf),
        in_specs=[pl.BlockSpec((tm, d), lambda i, j: (i, 0)),
                  pl.BlockSpec((tm, d), lambda i, j: (i, 0)),
                  pl.BlockSpec((d, tf), lambda i, j: (0, j)),
                  pl.BlockSpec((d, tf), lambda i, j: (0, j)),
                  pl.BlockSpec((tf, d), lambda i, j: (j, 0))],
        out_specs=pl.BlockSpec((tm, d), lambda i, j: (i, 0)),
        out_shape=jax.ShapeDtypeStruct((n, d), F32),
        compiler_params=_params(),
        name="ffn_dense",
    )(h, x, wg, wu, wd)


MOE_RB = 128
MOE_RB_SHIFT = 7


def _moe_kernel(cnt_ref, h_ref, x_ref, gate_ref, gt_ref, wg_ref, wu_ref, wd_ref, o_ref, xs_ref, y_ref):
    i, e, j = pl.program_id(0), pl.program_id(1), pl.program_id(2)
    tm = h_ref.shape[0]
    nblk = (cnt_ref[i, e] + (MOE_RB - 1)) >> MOE_RB_SHIFT

    @pl.when((e == 0) & (j == 0))
    def _():
        o_ref[...] = x_ref[...]

    def block_rows(k):
        return pl.ds(pl.multiple_of(k * MOE_RB, MOE_RB), MOE_RB)

    @pl.when(j == 0)
    def _():
        gt = gt_ref[...]
        grow = jnp.sum(jnp.where(lax.broadcasted_iota(jnp.int32, gt.shape, 0) == e, gt, 0.0), axis=0, keepdims=True)
        mrow = grow != 0.0
        before = lax.broadcasted_iota(jnp.int32, (tm, tm), 0) < lax.broadcasted_iota(jnp.int32, (tm, tm), 1)
        slot = _dot(jnp.broadcast_to(mrow.astype(BF16), (8, tm)), before.astype(BF16))[0:1]

        def pack(k, _):
            rid = (k * MOE_RB + lax.broadcasted_iota(jnp.int32, (MOE_RB, 1), 0)).astype(F32)
            onehot = (mrow & (slot == rid)).astype(BF16)
            xs_ref[block_rows(k), :] = _dot(onehot, h_ref[...]).astype(BF16)
            y_ref[block_rows(k), :] = jnp.zeros((MOE_RB, y_ref.shape[1]), F32)
            return 0

        lax.fori_loop(0, nblk, pack, 0)

    def expert(k, _):
        xs = xs_ref[block_rows(k), :]
        a = _dot(xs, wg_ref[...])
        u = _dot(xs, wu_ref[...])
        act = (a * jax.nn.sigmoid(a) * u).astype(BF16)
        y_ref[block_rows(k), :] += _dot(act, wd_ref[...])
        return 0

    lax.fori_loop(0, nblk, expert, 0)

    @pl.when(j == pl.num_programs(2) - 1)
    def _():
        gate = gate_ref[...]
        ge = jnp.sum(jnp.where(lax.broadcasted_iota(jnp.int32, gate.shape, 1) == e, gate, 0.0), axis=1, keepdims=True)
        mcol = ge != 0.0
        after = lax.broadcasted_iota(jnp.int32, (tm, tm), 1) < lax.broadcasted_iota(jnp.int32, (tm, tm), 0)
        slot = _dot(after.astype(BF16), jnp.broadcast_to(mcol.astype(BF16), (tm, LANES)))[:, 0:1]

        def unpack(k, _):
            cid = (k * MOE_RB + lax.broadcasted_iota(jnp.int32, (1, MOE_RB), 1)).astype(F32)
            onehot_t = (mcol & (slot == cid)).astype(BF16)
            y_hi, y_lo = _split_hi_lo(y_ref[block_rows(k), :])
            o_ref[...] += ge * (_dot(onehot_t, y_hi) + _dot(onehot_t, y_lo))
            return 0

        lax.fori_loop(0, nblk, unpack, 0)


def moe_ffn(h, x, gate, wg, wu, wd, tf=512):
    n, d = x.shape
    tm = _row_tile(n, 1024)
    nt = n // tm
    n_exp = wg.shape[0]
    nf = wg.shape[2] // tf
    assert nf * tf == wg.shape[2]
    routed = gate[:, :n_exp] != 0.0
    counts = jnp.sum(routed.reshape(nt, tm, n_exp), axis=1, dtype=jnp.int32)
    gate_t = jnp.transpose(gate[:, :n_exp])
    return pl.pallas_call(
        _moe_kernel,
        grid_spec=pltpu.PrefetchScalarGridSpec(
            num_scalar_prefetch=1,
            grid=(nt, n_exp, nf),
            in_specs=[pl.BlockSpec((tm, d), lambda i, e, j, c: (i, 0)),
                      pl.BlockSpec((tm, d), lambda i, e, j, c: (i, 0)),
                      pl.BlockSpec((tm, LANES), lambda i, e, j, c: (i, 0)),
                      pl.BlockSpec((n_exp, tm), lambda i, e, j, c: (0, i)),
                      pl.BlockSpec((None, d, tf), lambda i, e, j, c: (e, 0, j)),
                      pl.BlockSpec((None, d, tf), lambda i, e, j, c: (e, 0, j)),
                      pl.BlockSpec((None, tf, d), lambda i, e, j, c: (e, j, 0))],
            out_specs=pl.BlockSpec((tm, d), lambda i, e, j, c: (i, 0)),
            scratch_shapes=[pltpu.VMEM((tm, d), BF16), pltpu.VMEM((tm, d), F32)],
        ),
        out_shape=jax.ShapeDtypeStruct((n, d), F32),
        compiler_params=_params(),
        name="moe_ffn",
    )(counts, h, x, gate, gate_t, wg, wu, wd)


def _final_norm_kernel(x_ref, g_ref, o_ref):
    o_ref[...] = _rms(x_ref[...], g_ref[...])


def final_norm(x, g):
    n, d = x.shape
    tm = _row_tile(n, 1024)
    return pl.pallas_call(
        _final_norm_kernel,
        grid=(n // tm,),
        in_specs=[pl.BlockSpec((tm, d), lambda i: (i, 0)), pl.BlockSpec((1, d), lambda i: (0, 0))],
        out_specs=pl.BlockSpec((tm, d), lambda i: (i, 0)),
        out_shape=jax.ShapeDtypeStruct((n, d), F32),
        compiler_params=_params(),
        name="final_norm",
    )(x, g.reshape(1, d))


SLC_TILE = 512
WIN_TILE = 128


def _kv_prep_kernel(s_ref, w_ref, tab_ref, ska_ref, svt_ref, wka_ref, wvt_ref):
    tab = tab_ref[...]
    onehot, pieces = tab[:, :LANES], tab[:, LANES:]
    s = s_ref[...]
    w = w_ref[...]
    for g in range(N_KV_GROUPS):
        kc = slice(g * HEAD_DIM, (g + 1) * HEAD_DIM)
        vc = slice((N_KV_GROUPS + g) * HEAD_DIM, (N_KV_GROUPS + g + 1) * HEAD_DIM)
        ska_ref[g] = jnp.concatenate([onehot, s[:, kc].astype(BF16), pieces], axis=1)
        svt_ref[g] = s[:, vc].T.astype(BF16)
        wka_ref[g] = jnp.concatenate([w[:, kc].astype(BF16), pieces], axis=1)
        wv_t = w[:, vc].T.astype(BF16)
        for j in range(SLC_TILE // WIN_TILE):
            wvt_ref[g, j] = wv_t[:, j * WIN_TILE:(j + 1) * WIN_TILE]


def kv_prep(slc_rows, win_rows, tab, nb, t_len):
    tm = SLC_TILE
    nt = t_len // tm
    wpt = tm // WIN_TILE
    g3 = N_KV_GROUPS
    return pl.pallas_call(
        _kv_prep_kernel,
        grid=(nb, nt),
        in_specs=[pl.BlockSpec((tm, KV_ROW), lambda b, t: (b * nt + t, 0)),
                  pl.BlockSpec((tm, KV_ROW), lambda b, t: (b * nt + t, 0)),
                  pl.BlockSpec((tm, tab.shape[1]), lambda b, t: (t, 0))],
        out_specs=[pl.BlockSpec((None, g3, tm, 2 * LANES), lambda b, t: (b, 0, t, 0)),
                   pl.BlockSpec((None, g3, None, HEAD_DIM, tm), lambda b, t: (b, 0, t, 0, 0)),
                   pl.BlockSpec((None, g3, tm, LANES), lambda b, t: (b, 0, t, 0)),
                   pl.BlockSpec((None, g3, wpt, HEAD_DIM, WIN_TILE), lambda b, t: (b, 0, t, 0, 0))],
        out_shape=[jax.ShapeDtypeStruct((nb, g3, t_len, 2 * LANES), BF16),
                   jax.ShapeDtypeStruct((nb, g3, nt, HEAD_DIM, tm), BF16),
                   jax.ShapeDtypeStruct((nb, g3, t_len, LANES), BF16),
                   jax.ShapeDtypeStruct((nb, g3, t_len // WIN_TILE, HEAD_DIM, WIN_TILE), BF16)],
        compiler_params=_params(),
        name="kv_prep",
    )(slc_rows, win_rows, tab)


def _cmp_first_layer(get_cols, kv, pe_ref, w1_ref):
    acc_a = None
    acc_b = None
    for p in range(CMP_STRIDE):
        xs = jnp.concatenate(
            [get_cols(p * KV_ROW + (kv * N_KV_GROUPS + g) * HEAD_DIM) for g in range(N_KV_GROUPS)], axis=0)
        q = p + CMP_STRIDE
        da = _dot((xs + pe_ref[kv, p:p + 1, :]).astype(BF16), w1_ref[kv, p * HEAD_DIM:(p + 1) * HEAD_DIM, :])
        db = _dot((xs + pe_ref[kv, q:q + 1, :]).astype(BF16), w1_ref[kv, q * HEAD_DIM:(q + 1) * HEAD_DIM, :])
        acc_a = da if acc_a is None else acc_a + da
        acc_b = db if acc_b is None else acc_b + db
    return acc_a, acc_b


def _cmp_second_layer(kv, nch, a_ref, b_ref, w2_ref, emit):
    for g in range(N_KV_GROUPS):
        hid = a_ref[g, 0:nch, :] + b_ref[g, pl.ds(1, nch), :]
        hid = hid * jax.nn.sigmoid(hid)
        emit(kv, g, _dot(hid.astype(BF16), w2_ref[kv]))


def _compress_prompt_kernel(x_ref, pe_ref, w1_ref, w2_ref, cke_ref, cka_ref, cvt_ref, a_ref, b_ref, *, nch, tc):
    def emit(kv, g, out):
        if kv == 0:
            cka_ref[g] = jnp.concatenate([out.astype(BF16), cke_ref[...]], axis=1)
        else:
            cvt_ref[g] = out.T.astype(BF16)

    for kv in range(2):
        for ct in range(nch // tc):
            rows = slice(ct * tc, (ct + 1) * tc)
            acc_a, acc_b = _cmp_first_layer(lambda c0: x_ref[rows, c0:c0 + HEAD_DIM], kv, pe_ref, w1_ref)
            for g in range(N_KV_GROUPS):
                a_ref[g, rows, :] = acc_a[g * tc:(g + 1) * tc]
                b_ref[g, rows, :] = acc_b[g * tc:(g + 1) * tc]
        b_ref[:, nch:nch + 8, :] = jnp.zeros((N_KV_GROUPS, 8, b_ref.shape[2]), F32)
        _cmp_second_layer(kv, nch, a_ref, b_ref, w2_ref, emit)


def compress_prompt(cmp_rows, pe, w1, w2, cke, nb, t_len):
    nch = t_len // CMP_STRIDE
    tc = min(nch, 128)
    hid = w1.shape[2]
    xc = cmp_rows.reshape(nb, nch, CHUNK_ROW)
    full = lambda a: pl.BlockSpec(a.shape, lambda b: (0,) * a.ndim)
    return pl.pallas_call(
        functools.partial(_compress_prompt_kernel, nch=nch, tc=tc),
        grid=(nb,),
        in_specs=[pl.BlockSpec((None, nch, CHUNK_ROW), lambda b: (b, 0, 0)), full(pe), full(w1), full(w2), full(cke)],
        out_specs=[pl.BlockSpec((None, N_KV_GROUPS, nch, LANES), lambda b: (b, 0, 0, 0)),
                   pl.BlockSpec((None, N_KV_GROUPS, HEAD_DIM, nch), lambda b: (b, 0, 0, 0))],
        out_shape=[jax.ShapeDtypeStruct((nb, N_KV_GROUPS, nch, LANES), BF16),
                   jax.ShapeDtypeStruct((nb, N_KV_GROUPS, HEAD_DIM, nch), BF16)],
        scratch_shapes=[pltpu.VMEM((N_KV_GROUPS, nch + 8, hid), F32)] * 2,
        compiler_params=_params(),
        name="compress_prompt",
    )(xc, pe, w1, w2, cke)


def _compress_sample_kernel(pt_ref, *refs, n_pages):
    page_refs = refs[:n_pages]
    new_ref, pe_ref, w1_ref, w2_ref, ck_ref, cv_ref, a_ref, b_ref = refs[n_pages:]
    nch = n_pages * (PAGE_SIZE // CMP_STRIDE)
    for kv in range(2):
        acc_a, acc_b = _cmp_first_layer(
            lambda c0: jnp.concatenate([pr[:, c0:c0 + HEAD_DIM] for pr in page_refs], axis=0), kv, pe_ref, w1_ref)
        _, new_b = _cmp_first_layer(
            lambda c0: jnp.broadcast_to(new_ref[:, c0:c0 + HEAD_DIM], (8, HEAD_DIM)), kv, pe_ref, w1_ref)
        for g in range(N_KV_GROUPS):
            a_ref[g, 0:nch, :] = acc_a[g * nch:(g + 1) * nch]
            b_ref[g, 0:nch, :] = acc_b[g * nch:(g + 1) * nch]
            b_ref[g, nch:nch + 8, :] = new_b[g * 8:(g + 1) * 8]

        def emit(kv, g, out):
            (ck_ref, cv_ref)[kv][g] = out.astype(BF16)

        _cmp_second_layer(kv, nch, a_ref, b_ref, w2_ref, emit)


def compress_sample(pool, page_table, cmp_new, pe, w1, w2):
    nseq, n_pages = page_table.shape
    dec = cmp_new.shape[1]
    cpp = PAGE_SIZE // CMP_STRIDE
    nch = n_pages * cpp
    hid = w1.shape[2]
    pool_c = pool.reshape(pool.shape[0], cpp, CHUNK_ROW)
    new_c = jnp.pad(cmp_new.reshape(nseq, 1, dec * KV_ROW), ((0, 0), (0, 0), (0, CHUNK_ROW - dec * KV_ROW)))
    page_spec = lambda p: pl.BlockSpec((None, cpp, CHUNK_ROW), lambda b, pt: (pt[b, p], 0, 0))
    full = lambda a: pl.BlockSpec(a.shape, lambda b, pt: (0,) * a.ndim)
    o_spec = pl.BlockSpec((None, N_KV_GROUPS, nch, HEAD_DIM), lambda b, pt: (b, 0, 0, 0))
    o_shape = jax.ShapeDtypeStruct((nseq, N_KV_GROUPS, nch, HEAD_DIM), BF16)
    return pl.pallas_call(
        functools.partial(_compress_sample_kernel, n_pages=n_pages),
        grid_spec=pltpu.PrefetchScalarGridSpec(
            num_scalar_prefetch=1,
            grid=(nseq,),
            in_specs=[page_spec(p) for p in range(n_pages)]
            + [pl.BlockSpec((None, 1, CHUNK_ROW), lambda b, pt: (b, 0, 0)), full(pe), full(w1), full(w2)],
            out_specs=[o_spec, o_spec],
            scratch_shapes=[pltpu.VMEM((N_KV_GROUPS, nch + 8, hid), F32)] * 2,
        ),
        out_shape=[o_shape, o_shape],
        compiler_params=_params(),
        name="compress_sample",
    )(page_table, *([pool_c] * n_pages), new_c, pe, w1, w2)


def _masked_softmax(s, mask, axis=-1):
    m = jnp.max(s, axis=axis, keepdims=True)
    e = jnp.where(mask, jnp.exp(s - m), 0.0)
    d = jnp.sum(e, axis=axis, keepdims=True)
    return e / jnp.where(d > 0, d, 1.0)


def _select_blocks(imp, cur, axis=-1):
    axis = axis % imp.ndim
    j = lax.broadcasted_iota(jnp.int32, imp.shape, axis)
    valid = j <= cur
    forced = valid & ((j == 0) | (j == cur) | (j == cur - 1))
    v = jnp.where(forced, jnp.inf, jnp.where(valid, imp, -jnp.inf))
    sel = jnp.zeros(imp.shape, F32)
    for _ in range(N_SEL):
        m = jnp.max(v, axis=axis, keepdims=True)
        jm = jnp.min(jnp.where(v == m, j, LANES), axis=axis, keepdims=True)
        pick = j == jm
        v = jnp.where(pick, -jnp.inf, v)
        sel = jnp.where(pick, 1.0, sel)
    return (sel > 0.0) & valid


def _stack_heads(q):
    return jnp.concatenate([q[:, r * HEAD_DIM:(r + 1) * HEAD_DIM] for r in range(HEADS_PER_GROUP)], axis=0)


def _importance(p_c, tq, mimp_ref):
    pcs = p_c[0:tq]
    for r in range(1, HEADS_PER_GROUP):
        pcs = pcs + p_c[r * tq:(r + 1) * tq]
    hi, lo = _split_hi_lo(pcs)
    return _dot(hi, mimp_ref[...]) + _dot(lo, mimp_ref[...])


def _gated_merge(gsig, g, o_c, o_s, o_w, tq, store):
    for r in range(HEADS_PER_GROUP):
        c = 3 * (g * HEADS_PER_GROUP + r)
        rows = slice(r * tq, (r + 1) * tq)
        store(r, gsig[:, c:c + 1] * o_c[rows] + gsig[:, c + 1:c + 2] * o_s[rows] + gsig[:, c + 2:c + 3] * o_w[rows])


def _nsa_prompt_kernel(q_ref, glt_ref, qe_ref, cka_ref, cvt_ref, ska_ref, svt_ref, wka_ref, wvt_ref,
                       mimpt_ref, o_ref, *, t_len):
    i = pl.program_id(2)
    tq = Q_BLOCK
    rows = HEADS_PER_GROUP * tq
    ncp = t_len // CMP_STRIDE
    s0 = i * tq
    q4 = (_stack_heads(q_ref[...]) * (HEAD_DIM ** -0.5)).astype(BF16)
    qb = jnp.concatenate([q4, qe_ref[...]], axis=1)
    qpos = s0 + (lax.broadcasted_iota(jnp.int32, (1, rows), 1) & (tq - 1))

    cpos = lax.broadcasted_iota(jnp.int32, (ncp, 1), 0) * CMP_STRIDE + (CMP_BLOCK - 1)
    mask_c = cpos <= qpos
    p_c = _masked_softmax(jnp.where(mask_c, _dot_nt(cka_ref[...], qb), NEG), mask_c, axis=0)
    o_c = _dot(cvt_ref[...], p_c.astype(BF16))

    pcs = p_c[:, 0:tq]
    for r in range(1, HEADS_PER_GROUP):
        pcs = pcs + p_c[:, r * tq:(r + 1) * tq]
    hi, lo = _split_hi_lo(pcs)
    imp = _dot(mimpt_ref[...], hi) + _dot(mimpt_ref[...], lo)
    sel = _select_blocks(imp, qpos[:, 0:tq] >> SLC_SHIFT, axis=0)
    selbias = jnp.where(sel, 0.0, NEG).T.astype(BF16)
    qa = jnp.concatenate([jnp.concatenate([selbias] * HEADS_PER_GROUP, axis=0), qb], axis=1)

    def scores(kt):
        return _dot_nt(ska_ref[pl.ds(pl.multiple_of(kt * SLC_TILE, SLC_TILE), SLC_TILE), :], qa)

    def flash_update(stats, s, vt):
        m, l, acc = stats
        m_new = jnp.maximum(m, jnp.max(s, axis=0, keepdims=True))
        a = jnp.exp(m - m_new)
        p = jnp.exp(s - m_new)
        return m_new, a * l + jnp.sum(p, axis=0, keepdims=True), a * acc + _dot(vt, p.astype(BF16))

    def bulk(kt, carry):
        s_cur, stats = carry
        s_next = scores(kt + 1)
        return s_next, flash_update(stats, s_cur, svt_ref[kt])

    stats = (jnp.full((1, rows), M_INIT, F32), jnp.zeros((1, rows), F32), jnp.zeros((HEAD_DIM, rows), F32))
    kt_last = s0 // SLC_TILE
    s_last, stats = lax.fori_loop(0, kt_last, bulk, (scores(0), stats))
    kpos = kt_last * SLC_TILE + lax.broadcasted_iota(jnp.int32, (SLC_TILE, 1), 0)
    _, l_s, acc_s = flash_update(stats, jnp.where(kpos <= qpos, s_last, NEG), svt_ref[kt_last])
    o_s = acc_s / l_s

    nwt = (WINDOW + tq) // WIN_TILE
    wb = jnp.maximum(i - WINDOW // WIN_TILE, 0)
    kw = pl.ds(pl.multiple_of(wb * WIN_TILE, WIN_TILE), nwt * WIN_TILE)
    dist = qpos - (wb * WIN_TILE + lax.broadcasted_iota(jnp.int32, (nwt * WIN_TILE, 1), 0))
    mask_w = (dist >= 0) & (dist < WINDOW)
    p_w = _masked_softmax(jnp.where(mask_w, _dot_nt(wka_ref[kw, :], qb), NEG), mask_w, axis=0).astype(BF16)
    o_w = _dot(wvt_ref[wb], p_w[0:WIN_TILE])
    for w in range(1, nwt):
        o_w = o_w + _dot(wvt_ref[wb + w], p_w[w * WIN_TILE:(w + 1) * WIN_TILE])

    gsig = jax.nn.sigmoid(glt_ref[...])
    lane_gate = lambda c: jnp.concatenate([gsig[3 * r + c:3 * r + c + 1, :] for r in range(HEADS_PER_GROUP)], axis=1)
    out_t = lane_gate(0) * o_c + lane_gate(1) * o_s + lane_gate(2) * o_w
    for r in range(HEADS_PER_GROUP):
        o_ref[:, r * HEAD_DIM:(r + 1) * HEAD_DIM] = out_t[:, r * tq:(r + 1) * tq].T.astype(o_ref.dtype)


def nsa_prompt(q, gl_t, cka, cvt, ska, svt, wka, wvt, consts, nb, t_len):
    nqb = t_len // Q_BLOCK
    rows = HEADS_PER_GROUP * Q_BLOCK
    gw = 3 * HEADS_PER_GROUP
    qe, mimp_t = consts
    per_bg = lambda a: pl.BlockSpec((None, None) + a.shape[2:], lambda b, g, i: (b, g) + (0,) * (a.ndim - 2))
    return pl.pallas_call(
        functools.partial(_nsa_prompt_kernel, t_len=t_len),
        grid=(nb, N_KV_GROUPS, nqb),
        in_specs=[pl.BlockSpec((Q_BLOCK, HEADS_PER_GROUP * HEAD_DIM), lambda b, g, i: (b * nqb + i, g)),
                  pl.BlockSpec((None, gw, Q_BLOCK), lambda b, g, i: (g, 0, b * nqb + i)),
                  pl.BlockSpec((None, rows, HEAD_DIM), lambda b, g, i: (g, 0, 0)),
                  per_bg(cka), per_bg(cvt), per_bg(ska), per_bg(svt), per_bg(wka), per_bg(wvt),
                  pl.BlockSpec(mimp_t.shape, lambda b, g, i: (0, 0))],
        out_specs=pl.BlockSpec((Q_BLOCK, HEADS_PER_GROUP * HEAD_DIM), lambda b, g, i: (b * nqb + i, g)),
        out_shape=jax.ShapeDtypeStruct((nb * t_len, NSA_W), BF16),
        compiler_params=_params(),
        name="nsa_prompt",
    )(q, gl_t, qe, cka, cvt, ska, svt, wka, wvt, mimp_t)


def _nsa_sample_kernel(pt_ref, *refs, n_pages, dec, win_buf):
    page_refs = refs[:n_pages]
    (q_ref, gl_ref, slp_ref, ck_ref, cv_ref, snew_ref, wc_ref, wnew_ref, es_ref, mimp_ref, o_ref) = refs[n_pages:]
    tq = dec
    rows = HEADS_PER_GROUP * tq
    past = n_pages * PAGE_SIZE
    ncs = ck_ref.shape[1]
    t_row = lax.broadcasted_iota(jnp.int32, (rows, 1), 0) & (tq - 1)
    qpos = past + t_row
    gsig = jax.nn.sigmoid(gl_ref[...])
    q_all = q_ref[...]
    snew = snew_ref[...]
    wc = wc_ref[...]
    wnew = wnew_ref[...]
    tnew = lax.broadcasted_iota(jnp.int32, (1, tq), 1)
    mask_new = tnew <= t_row
    dist_new = (t_row - tnew).astype(F32)

    q4s, o_cs, imps = [], [], []
    for g in range(N_KV_GROUPS):
        q4 = (_stack_heads(q_all[:, g * HEADS_PER_GROUP * HEAD_DIM:(g + 1) * HEADS_PER_GROUP * HEAD_DIM])
              * (HEAD_DIM ** -0.5)).astype(BF16)
        cpos = lax.broadcasted_iota(jnp.int32, (1, ncs), 1) * CMP_STRIDE + (CMP_BLOCK - 1)
        mask_c = cpos <= qpos
        s_c = _dot_nt(q4, ck_ref[g]) - slp_ref[g] * (qpos - cpos).astype(F32)
        p_c = _masked_softmax(jnp.where(mask_c, s_c, NEG), mask_c)
        q4s.append(q4)
        o_cs.append(_dot(p_c.astype(BF16), cv_ref[g]))
        imps.append(_importance(p_c, tq, mimp_ref))
    cur = (past + (lax.broadcasted_iota(jnp.int32, (N_KV_GROUPS * tq, 1), 0) & (tq - 1))) >> SLC_SHIFT
    selbias_all = jnp.where(_select_blocks(jnp.concatenate(imps, axis=0), cur), 0.0, NEG)

    for g in range(N_KV_GROUPS):
        kc = slice(g * HEAD_DIM, (g + 1) * HEAD_DIM)
        vc = slice((N_KV_GROUPS + g) * HEAD_DIM, (N_KV_GROUPS + g + 1) * HEAD_DIM)
        q4, o_c = q4s[g], o_cs[g]
        slope = slp_ref[g]
        sb4 = jnp.concatenate([selbias_all[g * tq:(g + 1) * tq].astype(BF16)] * HEADS_PER_GROUP, axis=0)

        k_past = jnp.concatenate([pr[:, kc] for pr in page_refs], axis=0).astype(BF16)
        v_past = jnp.concatenate([pr[:, vc] for pr in page_refs], axis=0).astype(BF16)
        kpos = lax.broadcasted_iota(jnp.int32, (1, past), 1)
        s_p = _dot_nt(q4, k_past) + _dot_nt(sb4, es_ref[...]) - slope * (qpos - kpos).astype(F32)
        s_n = _dot_nt(q4, snew[:, kc].astype(BF16)) - slope * dist_new
        s_n = jnp.where(mask_new, s_n, NEG)
        m = jnp.maximum(jnp.max(s_p, axis=-1, keepdims=True), jnp.max(s_n, axis=-1, keepdims=True))
        e_p = jnp.exp(s_p - m)
        e_n = jnp.exp(s_n - m)
        l = jnp.sum(e_p, axis=-1, keepdims=True) + jnp.sum(e_n, axis=-1, keepdims=True)
        o_s = (_dot(e_p.astype(BF16), v_past) + _dot(e_n.astype(BF16), snew[:, vc].astype(BF16))) / l

        wdist = qpos - (past - win_buf + lax.broadcasted_iota(jnp.int32, (1, win_buf), 1))
        mask_w = wdist < WINDOW
        s_w = _dot_nt(q4, wc[:, kc].astype(BF16)) - slope * wdist.astype(F32)
        s_w = jnp.where(mask_w, s_w, NEG)
        s_wn = _dot_nt(q4, wnew[:, kc].astype(BF16)) - slope * dist_new
        s_wn = jnp.where(mask_new, s_wn, NEG)
        m = jnp.maximum(jnp.max(s_w, axis=-1, keepdims=True), jnp.max(s_wn, axis=-1, keepdims=True))
        e_w = jnp.exp(s_w - m)
        e_wn = jnp.exp(s_wn - m)
        l = jnp.sum(e_w, axis=-1, keepdims=True) + jnp.sum(e_wn, axis=-1, keepdims=True)
        o_w = (_dot(e_w.astype(BF16), wc[:, vc].astype(BF16)) + _dot(e_wn.astype(BF16), wnew[:, vc].astype(BF16))) / l

        def store(r, val, g=g):
            c = (g * HEADS_PER_GROUP + r) * HEAD_DIM
            o_ref[:, c:c + HEAD_DIM] = val.astype(o_ref.dtype)

        _gated_merge(gsig, g, o_c, o_s, o_w, tq, store)


def nsa_sample(q, gl, ck, cv, pool, page_table, slc_rows, win_cache, win_rows, consts, row_off, dec):
    nseq, n_pages = page_table.shape
    win_buf = win_cache.shape[1]
    slp, es, mimp = consts
    full = lambda a: pl.BlockSpec(a.shape, lambda b, pt: (0,) * a.ndim)
    row = lambda c: pl.BlockSpec((dec, c), lambda b, pt: (row_off + b, 0))
    seq4 = lambda a: pl.BlockSpec((None,) + a.shape[1:], lambda b, pt: (b, 0, 0, 0))
    page_spec = lambda p: pl.BlockSpec((None, PAGE_SIZE, KV_ROW), lambda b, pt: (pt[b, p], 0, 0))
    return pl.pallas_call(
        functools.partial(_nsa_sample_kernel, n_pages=n_pages, dec=dec, win_buf=win_buf),
        grid_spec=pltpu.PrefetchScalarGridSpec(
            num_scalar_prefetch=1,
            grid=(nseq,),
            in_specs=[page_spec(p) for p in range(n_pages)]
            + [row(NSA_W), row(GATE_W), full(slp), seq4(ck), seq4(cv), row(KV_ROW),
               pl.BlockSpec((None, win_buf, KV_ROW), lambda b, pt: (b, 0, 0)), row(KV_ROW), full(es), full(mimp)],
            out_specs=pl.BlockSpec((dec, NSA_W), lambda b, pt: (b, 0)),
        ),
        out_shape=jax.ShapeDtypeStruct((nseq * dec, NSA_W), F32),
        compiler_params=_params(),
        name="nsa_sample",
    )(page_table, *([pool] * n_pages), q, gl, slp, ck, cv, slc_rows, win_cache, win_rows, es, mimp)


def _alibi_slopes():
    return 2.0 ** (-ALIBI_MAX_BIAS * jnp.arange(1, N_HEADS + 1, dtype=F32) / N_HEADS)


def _slope_rows(tq):
    return jnp.repeat(_alibi_slopes().reshape(N_KV_GROUPS, HEADS_PER_GROUP), tq, axis=1)[..., None]


def _importance_matrix(n_cmp):
    n = jnp.arange(n_cmp)[:, None]
    j = jnp.arange(LANES)[None, :]
    per_sel = SLC_BLOCK // CMP_STRIDE
    inside = (SLC_BLOCK - CMP_BLOCK) // CMP_STRIDE + 1
    return ((n // per_sel == j) & (n % per_sel < inside)).astype(BF16)


def _block_onehot(n_keys):
    k = jnp.arange(n_keys)[:, None]
    return (k // SLC_BLOCK == jnp.arange(LANES)[None, :]).astype(BF16)


def _position_pieces(pos):
    pos = pos[:, None]
    hi = ((pos // SLC_BLOCK) * SLC_BLOCK).astype(BF16)
    lo = (pos % SLC_BLOCK).astype(BF16)
    pieces = jnp.concatenate([hi, hi, hi, lo, lo, lo], axis=-1)
    return jnp.pad(pieces, ((0, 0), (0, HEAD_DIM - pieces.shape[-1])))


def _prompt_tables(t_len):
    slp = _slope_rows(Q_BLOCK)
    s1 = slp.astype(BF16)
    s2 = (slp - s1.astype(F32)).astype(BF16)
    s3 = (slp - s1.astype(F32) - s2.astype(F32)).astype(BF16)
    qe = jnp.concatenate([s1, s2, s3, s1, s2, s3], axis=-1)
    qe = jnp.pad(qe, ((0, 0), (0, 0), (0, HEAD_DIM - qe.shape[-1])))
    key_tab = jnp.concatenate([_block_onehot(t_len), _position_pieces(jnp.arange(t_len))], axis=-1)
    ncp = t_len // CMP_STRIDE
    cmp_tab = _position_pieces(jnp.arange(ncp) * CMP_STRIDE + (CMP_BLOCK - 1))
    return (qe, jnp.transpose(_importance_matrix(ncp))), key_tab, cmp_tab


def kernel(x_prompt, x_sample, state_conv, cache_mem_kv, cache_cmp_kv, cache_slc_kv, cache_win_kv, page_table,
           mem_prompt, norm1, norm2, norm_final, w_in_a, conv_w, conv_b, conv_ln_g, conv_ln_b, w_in_b, kv_norm,
           w_kv_shared, cmp_pos, cmp_w1, cmp_w2, w_mem_kv, w_out, ffn_w_gate, ffn_w_up, ffn_w_down, moe_router,
           moe_w_gate, moe_w_up, moe_w_down):
    bp, t_len, d = x_prompt.shape
    db, dec, _ = x_sample.shape
    depth = norm1.shape[0]
    n_a = w_in_a.shape[0]
    n_mem = mem_prompt.shape[1]
    n_p = bp * t_len
    n_s = db * dec
    n_pages = page_table.shape[1]
    past = n_pages * PAGE_SIZE
    win_buf = cache_win_kv.shape[1]
    bf = lambda a: a.astype(BF16)

    x = jnp.concatenate([x_prompt.reshape(n_p, d), x_sample.reshape(n_s, d)], axis=0)

    mem_kv_p = mem_kv_proj(mem_prompt.reshape(bp * n_mem, d), w_mem_kv)
    mem_kv_p = mem_kv_p.reshape(depth, bp, n_mem, 2 * MEM_W)
    mem_kv_s = cache_mem_kv.reshape(depth, db, n_mem, 2 * MEM_W)

    conv_p, conv_s = [], []
    zero_state = jnp.zeros((bp, CONV_STATE, C_CONV), F32)
    cmp_rows = slc_rows = win_rows = None
    nsa_ctx = None

    for l in range(depth):
        if l < n_a:
            uc, qm = norm_proj(x, norm1[l], bf(w_in_a[l]), (2 * C_CONV, MEM_W))
            mix_p, st_p = conv_mixer(uc[:n_p], zero_state, conv_w[l], conv_b[l], conv_ln_g[l], conv_ln_b[l],
                                     bp, t_len, Q_BLOCK)
            mix_s, st_s = conv_mixer(uc[n_p:], state_conv[l], conv_w[l], conv_b[l], conv_ln_g[l], conv_ln_b[l],
                                     db, dec, dec)
            conv_p.append(st_p)
            conv_s.append(st_s)
        else:
            if nsa_ctx is None:
                cmp_rows, slc_rows, win_rows = norm_proj(x, kv_norm, bf(w_kv_shared), (KV_ROW,) * 3)
                w1 = bf(cmp_w1)
                w2 = bf(cmp_w2)
                prompt_consts, key_tab, cmp_tab = _prompt_tables(t_len)
                cka, cvt = compress_prompt(cmp_rows[:n_p], cmp_pos, w1, w2, cmp_tab, bp, t_len)
                ck_s, cv_s = compress_sample(cache_cmp_kv.reshape(-1, PAGE_SIZE, KV_ROW), page_table,
                                             cmp_rows[n_p:].reshape(db, dec, KV_ROW), cmp_pos, w1, w2)
                ska, svt, wka, wvt = kv_prep(slc_rows, win_rows, key_tab, bp, t_len)
                nsa_ctx = dict(
                    prompt=prompt_consts,
                    sample=(_slope_rows(dec), _block_onehot(past), _importance_matrix(ck_s.shape[2])),
                    pool=cache_slc_kv.reshape(-1, PAGE_SIZE, KV_ROW),
                    win_cache=cache_win_kv.reshape(db, win_buf, KV_ROW))
            w_in = w_in_b[l - n_a]
            w_in = jnp.concatenate([w_in[:, :NSA_W], w_in[:, NSA_W + GATE_W:], w_in[:, NSA_W:NSA_W + GATE_W]], axis=1)
            q, qm, gl = norm_proj(x, norm1[l], bf(w_in), (NSA_W, MEM_W, GATE_W))
            gw = 3 * HEADS_PER_GROUP
            gl_t = jnp.transpose(gl[:n_p].reshape(n_p, N_KV_GROUPS, gw), (1, 2, 0))
            mix_p = nsa_prompt(q, gl_t, cka, cvt, ska, svt, wka, wvt, nsa_ctx["prompt"], bp, t_len)
            mix_s = nsa_sample(q, gl, ck_s, cv_s, nsa_ctx["pool"], page_table, slc_rows, nsa_ctx["win_cache"],
                               win_rows, nsa_ctx["sample"], n_p // dec, dec).astype(BF16)
        mo_p = mem_attend(qm[:n_p], mem_kv_p[l], bp, t_len, 512)
        mo_s = mem_attend(qm[n_p:], mem_kv_s[l], db, dec, dec)
        mix = jnp.concatenate([mix_p, mix_s], axis=0)
        mo = jnp.concatenate([mo_p, mo_s], axis=0)
        if l % 2 == 0:
            xn, h2 = outproj_norm(x, mix, mo, bf(w_out[l]), norm2[l])
            x = ffn_dense(h2, xn, bf(ffn_w_gate[l // 2]), bf(ffn_w_up[l // 2]), bf(ffn_w_down[l // 2]))
        else:
            xn, h2, gate = outproj_norm(x, mix, mo, bf(w_out[l]), norm2[l], router=moe_router[l // 2])
            x = moe_ffn(h2, xn, gate, bf(moe_w_gate[l // 2]), bf(moe_w_up[l // 2]), bf(moe_w_down[l // 2]))

    y = final_norm(x, norm_final)
    kv5 = lambda a, b, t: a.reshape(b, t, 2, N_KV_GROUPS, HEAD_DIM)
    keep_p = min(WINDOW, t_len)
    win_p = kv5(win_rows[:n_p], bp, t_len)[:, t_len - keep_p:]
    win_all = jnp.concatenate([cache_win_kv, kv5(win_rows[n_p:], db, dec)], axis=1)
    keep_s = min(WINDOW, past + dec)
    return (y[:n_p].reshape(bp, t_len, d), y[n_p:].reshape(db, dec, d),
            jnp.stack(conv_p), jnp.stack(conv_s),
            mem_kv_p.reshape(depth, bp, n_mem, 2, MEM_HEADS, MEM_HEAD_DIM),
            kv5(cmp_rows[:n_p], bp, t_len), kv5(cmp_rows[n_p:], db, dec),
            kv5(slc_rows[:n_p], bp, t_len), kv5(slc_rows[n_p:], db, dec),
            win_p, win_all[:, win_all.shape[1] - keep_s:])
```

```python
import functools

import jax
import jax.numpy as jnp
from jax import lax
from jax.experimental import pallas as pl
from jax.experimental.pallas import tpu as pltpu

F32 = jnp.float32
BF16 = jnp.bfloat16

HEAD_DIM = 64
N_KV_GROUPS = 3
HEADS_PER_GROUP = 4
N_HEADS = N_KV_GROUPS * HEADS_PER_GROUP
NSA_W = N_HEADS * HEAD_DIM
GATE_W = 3 * N_HEADS
CMP_STRIDE = 16
CMP_BLOCK = 32
SLC_BLOCK = 64
SLC_SHIFT = 6
N_SEL = 16
WINDOW = 512
Q_BLOCK = 128
ALIBI_MAX_BIAS = 8.0
MEM_HEADS = 4
MEM_HEAD_DIM = 64
MEM_W = MEM_HEADS * MEM_HEAD_DIM
MEM_SEQS_PER_STEP = 8
C_CONV = 768
CONV_K = 31
CONV_STATE = CONV_K - 1
N_EXPERTS = 8
EPS = 1e-6
PAGE_SIZE = 128
KV_ROW = 2 * N_KV_GROUPS * HEAD_DIM
CHUNK_ROW = CMP_STRIDE * KV_ROW
LANES = 128
NEG = -1e30
M_INIT = -1e29
VMEM_LIMIT = 56 * 1024 * 1024


def _dot(a, b):
    return jnp.dot(a, b, preferred_element_type=F32)


def _dot_nt(a, b):
    return lax.dot_general(a, b, (((1,), (1,)), ((), ())), preferred_element_type=F32)


def _split_hi_lo(x):
    hi = x.astype(BF16)
    lo = (x - hi.astype(F32)).astype(BF16)
    return hi, lo


def _rms(x, g):
    return x * lax.rsqrt(jnp.mean(x * x, axis=-1, keepdims=True) + EPS) * g


def _params(**kw):
    return pltpu.CompilerParams(vmem_limit_bytes=VMEM_LIMIT, **kw)


def _log2(n):
    assert n > 0 and n & (n - 1) == 0, n
    return n.bit_length() - 1


def _row_tile(n, pref):
    tm = pref
    while n % tm:
        tm //= 2
    assert tm >= 8, (n, pref)
    return tm


def _norm_proj_kernel(x_ref, g_ref, w_ref, *o_refs, splits):
    h = _rms(x_ref[...], g_ref[...])
    y = _dot(h.astype(BF16), w_ref[...])
    off = 0
    for o_ref, n in zip(o_refs, splits):
        o_ref[...] = y[:, off:off + n]
        off += n


def norm_proj(x, g, w, splits):
    n, d = x.shape
    tm = _row_tile(n, 512)
    return pl.pallas_call(
        functools.partial(_norm_proj_kernel, splits=splits),
        grid=(n // tm,),
        in_specs=[pl.BlockSpec((tm, d), lambda i: (i, 0)),
                  pl.BlockSpec((1, d), lambda i: (0, 0)),
                  pl.BlockSpec(w.shape, lambda i: (0, 0))],
        out_specs=[pl.BlockSpec((tm, s), lambda i: (i, 0)) for s in splits],
        out_shape=[jax.ShapeDtypeStruct((n, s), F32) for s in splits],
        compiler_params=_params(),
        name="norm_proj",
    )(x, g.reshape(1, d), w)


def _mem_kv_kernel(x_ref, w_ref, o_ref):
    o_ref[...] = _dot(x_ref[...].astype(BF16), w_ref[...].astype(BF16))


def mem_kv_proj(x, w):
    depth, d, e = w.shape
    n = x.shape[0]
    return pl.pallas_call(
        _mem_kv_kernel,
        grid=(depth,),
        in_specs=[pl.BlockSpec((n, d), lambda l: (0, 0)),
                  pl.BlockSpec((None, d, e), lambda l: (l, 0, 0))],
        out_specs=pl.BlockSpec((None, n, e), lambda l: (l, 0, 0)),
        out_shape=jax.ShapeDtypeStruct((depth, n, e), F32),
        compiler_params=_params(),
        name="mem_kv_proj",
    )(x, w)


CONV_PAD = 32


def _conv_kernel(u_ref, st_ref, w_ref, b_ref, lg_ref, lb_ref, mix_ref, ns_ref, vp_ref, *, tt, rc):
    t = pl.program_id(1)
    lo = CONV_PAD - CONV_STATE

    @pl.when(t == 0)
    def _():
        vp_ref[lo:CONV_PAD, :] = st_ref[...]

    u = u_ref[...]
    vp_ref[CONV_PAD:CONV_PAD + tt, :] = u[:, :C_CONV] * jax.nn.sigmoid(u[:, C_CONV:])
    for c in range(tt // rc):
        acc = jnp.zeros((rc, C_CONV), F32)
        for k in range(CONV_K):
            acc = acc + w_ref[k:k + 1, :] * vp_ref[pl.ds(lo + k + c * rc, rc), :]
        h = acc + b_ref[...]
        mu = jnp.mean(h, axis=-1, keepdims=True)
        hc = h - mu
        var = jnp.mean(hc * hc, axis=-1, keepdims=True)
        y = hc * lax.rsqrt(var + EPS) * lg_ref[...] + lb_ref[...]
        mix_ref[c * rc:(c + 1) * rc, :] = y * jax.nn.sigmoid(y)
    new_state = vp_ref[pl.ds(lo + tt, CONV_STATE), :]
    ns_ref[...] = new_state
    vp_ref[lo:CONV_PAD, :] = new_state


def conv_mixer(u, state, w, b, lg, lb, nseq, t_len, tt, row0=0):
    rc = min(tt, 32)
    nt = t_len // tt
    blk0 = row0 // tt
    vec = lambda a: a.reshape(1, C_CONV)
    return pl.pallas_call(
        functools.partial(_conv_kernel, tt=tt, rc=rc),
        grid=(nseq, nt),
        in_specs=[pl.BlockSpec((tt, 2 * C_CONV), lambda s, t: (blk0 + s * nt + t, 0)),
                  pl.BlockSpec((None, CONV_STATE, C_CONV), lambda s, t: (s, 0, 0)),
                  pl.BlockSpec((CONV_K, C_CONV), lambda s, t: (0, 0)),
                  pl.BlockSpec((1, C_CONV), lambda s, t: (0, 0)),
                  pl.BlockSpec((1, C_CONV), lambda s, t: (0, 0)),
                  pl.BlockSpec((1, C_CONV), lambda s, t: (0, 0))],
        out_specs=[pl.BlockSpec((tt, C_CONV), lambda s, t: (s * nt + t, 0)),
                   pl.BlockSpec((None, CONV_STATE, C_CONV), lambda s, t: (s, 0, 0))],
        out_shape=[jax.ShapeDtypeStruct((nseq * t_len, C_CONV), F32),
                   jax.ShapeDtypeStruct((nseq, CONV_STATE, C_CONV), F32)],
        scratch_shapes=[pltpu.VMEM((CONV_PAD + tt, C_CONV), F32)],
        compiler_params=_params(),
        name="conv_mixer",
    )(u, state, w, vec(b), vec(lg), vec(lb))


def _mem_attn_kernel(q_ref, kv_ref, o_ref, *, spb, tq):
    n_mem = kv_ref.shape[1]
    q = q_ref[...]
    kv = kv_ref[...].reshape(spb * n_mem, 2 * MEM_W).astype(BF16)
    own = None
    if spb > 1:
        q_seq = lax.broadcasted_iota(jnp.int32, (spb * tq, 1), 0) >> _log2(tq)
        k_seq = lax.broadcasted_iota(jnp.int32, (1, spb * n_mem), 1) >> _log2(n_mem)
        own = q_seq == k_seq
    for h in range(MEM_HEADS):
        c = h * MEM_HEAD_DIM
        qh = q[:, c:c + MEM_HEAD_DIM].astype(BF16)
        s = _dot_nt(qh, kv[:, c:c + MEM_HEAD_DIM]) * (MEM_HEAD_DIM ** -0.5)
        if own is not None:
            s = jnp.where(own, s, NEG)
        e = jnp.exp(s - jnp.max(s, axis=-1, keepdims=True))
        p = e / jnp.sum(e, axis=-1, keepdims=True)
        o_ref[:, c:c + MEM_HEAD_DIM] = _dot(p.astype(BF16), kv[:, MEM_W + c:MEM_W + c + MEM_HEAD_DIM])


def mem_attend(qm, kv, nseq, t_len, tq, spb=1, row0=0):
    nt = t_len // tq
    assert spb == 1 or nt == 1
    n_mem = kv.shape[1]
    rows = spb * tq
    blk0 = row0 // rows
    return pl.pallas_call(
        functools.partial(_mem_attn_kernel, spb=spb, tq=tq),
        grid=(nseq // spb, nt),
        in_specs=[pl.BlockSpec((rows, MEM_W), lambda s, t: (blk0 + s * nt + t, 0)),
                  pl.BlockSpec((spb, n_mem, 2 * MEM_W), lambda s, t: (s, 0, 0))],
        out_specs=pl.BlockSpec((rows, MEM_W), lambda s, t: (s * nt + t, 0)),
        out_shape=jax.ShapeDtypeStruct((nseq * t_len, MEM_W), F32),
        compiler_params=_params(),
        name="mem_attend",
    )(qm, kv)


def _outproj_kernel(x_ref, mixp_ref, mixs_ref, mop_ref, mos_ref, w_ref, g_ref, *rest, moe, prompt_tiles):
    if moe:
        r_ref, xn_ref, h_ref, gate_ref = rest
    else:
        xn_ref, h_ref = rest
    is_prompt = pl.program_id(0) < prompt_tiles
    mix = jnp.where(is_prompt, mixp_ref[...].astype(BF16), mixs_ref[...].astype(BF16))
    mo = jnp.where(is_prompt, mop_ref[...].astype(BF16), mos_ref[...].astype(BF16))
    d_mix = mix.shape[1]
    y = _dot(mix, w_ref[0:d_mix, :]) + _dot(mo, w_ref[d_mix:, :])
    xn = x_ref[...] + y
    xn_ref[...] = xn
    h = _rms(xn, g_ref[...])
    h_ref[...] = h.astype(BF16)
    if moe:
        h_hi, h_lo = _split_hi_lo(h)
        r_hi, r_lo = _split_hi_lo(r_ref[...])
        logits = _dot(h_hi, r_hi) + _dot(h_lo, r_hi) + _dot(h_hi, r_lo)
        lane = lax.broadcasted_iota(jnp.int32, logits.shape, 1)
        logits = jnp.where(lane < N_EXPERTS, logits, -jnp.inf)
        m1 = jnp.max(logits, axis=-1, keepdims=True)
        i1 = jnp.min(jnp.where(logits == m1, lane, LANES), axis=-1, keepdims=True)
        rest_l = jnp.where(lane == i1, -jnp.inf, logits)
        m2 = jnp.max(rest_l, axis=-1, keepdims=True)
        i2 = jnp.min(jnp.where(rest_l == m2, lane, LANES), axis=-1, keepdims=True)
        e2 = jnp.exp(m2 - m1)
        den = 1.0 + e2
        gate_ref[...] = jnp.where(lane == i1, 1.0 / den, 0.0) + jnp.where(lane == i2, e2 / den, 0.0)


def outproj_norm(x, mix_p, mix_s, mo_p, mo_s, w, g, router=None):
    n, d = x.shape
    tm = _row_tile(n, 512)
    assert mix_p.shape[0] % tm == 0 and mix_s.shape[0] % tm == 0
    pt = mix_p.shape[0] // tm
    moe = router is not None
    row = lambda c: pl.BlockSpec((tm, c), lambda i: (i, 0))
    row_p = lambda c: pl.BlockSpec((tm, c), lambda i: (jnp.minimum(i, pt - 1), 0))
    row_s = lambda c: pl.BlockSpec((tm, c), lambda i: (jnp.maximum(i - pt, 0), 0))
    full = lambda a: pl.BlockSpec(a.shape, lambda i: (0, 0))
    ins = [x, mix_p, mix_s, mo_p, mo_s, w, g.reshape(1, d)]
    in_specs = [row(d), row_p(mix_p.shape[1]), row_s(mix_s.shape[1]), row_p(mo_p.shape[1]), row_s(mo_s.shape[1]),
                full(w), pl.BlockSpec((1, d), lambda i: (0, 0))]
    out_specs = [row(d), row(d)]
    out_shape = [jax.ShapeDtypeStruct((n, d), F32), jax.ShapeDtypeStruct((n, d), BF16)]
    if moe:
        rp = jnp.pad(router, ((0, 0), (0, LANES - router.shape[1])))
        ins.append(rp)
        in_specs.append(full(rp))
        out_specs.append(row(LANES))
        out_shape.append(jax.ShapeDtypeStruct((n, LANES), F32))
    return pl.pallas_call(
        functools.partial(_outproj_kernel, moe=moe, prompt_tiles=pt),
        grid=(n // tm,),
        in_specs=in_specs, out_specs=out_specs, out_shape=out_shape,
        compiler_params=_params(),
        name="outproj_norm",
    )(*ins)


def _ffn_kernel(h_ref, x_ref, wg_ref, wu_ref, wd_ref, o_ref):
    @pl.when(pl.program_id(1) == 0)
    def _():
        o_ref[...] = x_ref[...]

    h = h_ref[...]
    a = _dot(h, wg_ref[...])
    u = _dot(h, wu_ref[...])
    act = (a * jax.nn.sigmoid(a) * u).astype(BF16)
    o_ref[...] += _dot(act, wd_ref[...])


def ffn_dense(h, x, wg, wu, wd, nf=2):
    n, d = x.shape
    tm = _row_tile(n, 512)
    tf = wg.shape[1] // nf
    return pl.pallas_call(
        _ffn_kernel,
        grid=(n // tm, nf),
        in_specs=[pl.BlockSpec((tm, d), lambda i, j: (i, 0)),
                  pl.BlockSpec((tm, d), lambda i, j: (i, 0)),
                  pl.BlockSpec((d, tf), lambda i, j: (0, j)),
                  pl.BlockSpec((d, tf), lambda i, j: (0, j)),
                  pl.BlockSpec((tf, d), lambda i, j: (j, 0))],
        out_specs=pl.BlockSpec((tm, d), lambda i, j: (i, 0)),
        out_shape=jax.ShapeDtypeStruct((n, d), F32),
        compiler_params=_params(),
        name="ffn_dense",
    )(h, x, wg, wu, wd)


MOE_RB = 128
MOE_RB_SHIFT = 7


def _moe_kernel(cnt_ref, h_ref, x_ref, gate_ref, gt_ref, wg_ref, wu_ref, wd_ref, o_ref, xs_ref, y_ref):
    i, e, j = pl.program_id(0), pl.program_id(1), pl.program_id(2)
    tm = h_ref.shape[0]
    nblk = (cnt_ref[i, e] + (MOE_RB - 1)) >> MOE_RB_SHIFT

    @pl.when((e == 0) & (j == 0))
    def _():
        o_ref[...] = x_ref[...]

    def block_rows(k):
        return pl.ds(pl.multiple_of(k * MOE_RB, MOE_RB), MOE_RB)

    @pl.when(j == 0)
    def _():
        gt = gt_ref[...]
        grow = jnp.sum(jnp.where(lax.broadcasted_iota(jnp.int32, gt.shape, 0) == e, gt, 0.0), axis=0, keepdims=True)
        mrow = grow != 0.0
        before = lax.broadcasted_iota(jnp.int32, (tm, tm), 0) < lax.broadcasted_iota(jnp.int32, (tm, tm), 1)
        slot = _dot(jnp.broadcast_to(mrow.astype(BF16), (8, tm)), before.astype(BF16))[0:1]

        def pack(k, _):
            rid = (k * MOE_RB + lax.broadcasted_iota(jnp.int32, (MOE_RB, 1), 0)).astype(F32)
            onehot = (mrow & (slot == rid)).astype(BF16)
            xs_ref[block_rows(k), :] = _dot(onehot, h_ref[...]).astype(BF16)
            y_ref[block_rows(k), :] = jnp.zeros((MOE_RB, y_ref.shape[1]), F32)
            return 0

        lax.fori_loop(0, nblk, pack, 0)

    def expert(k, _):
        xs = xs_ref[block_rows(k), :]
        a = _dot(xs, wg_ref[...])
        u = _dot(xs, wu_ref[...])
        act = (a * jax.nn.sigmoid(a) * u).astype(BF16)
        y_ref[block_rows(k), :] += _dot(act, wd_ref[...])
        return 0

    lax.fori_loop(0, nblk, expert, 0)

    @pl.when(j == pl.num_programs(2) - 1)
    def _():
        gate = gate_ref[...]
        ge = jnp.sum(jnp.where(lax.broadcasted_iota(jnp.int32, gate.shape, 1) == e, gate, 0.0), axis=1, keepdims=True)
        mcol = ge != 0.0
        after = lax.broadcasted_iota(jnp.int32, (tm, tm), 1) < lax.broadcasted_iota(jnp.int32, (tm, tm), 0)
        slot = _dot(after.astype(BF16), jnp.broadcast_to(mcol.astype(BF16), (tm, LANES)))[:, 0:1]

        def unpack(k, _):
            cid = (k * MOE_RB + lax.broadcasted_iota(jnp.int32, (1, MOE_RB), 1)).astype(F32)
            onehot_t = (mcol & (slot == cid)).astype(BF16)
            y_hi, y_lo = _split_hi_lo(y_ref[block_rows(k), :])
            o_ref[...] += ge * _dot(jnp.concatenate([onehot_t, onehot_t], axis=1),
                                    jnp.concatenate([y_hi, y_lo], axis=0))
            return 0

        lax.fori_loop(0, nblk, unpack, 0)


def moe_ffn(h, x, gate, wg, wu, wd):
    n, d = x.shape
    tm = _row_tile(n, 1024)
    nt = n // tm
    n_exp = wg.shape[0]
    tf = next(t for t in (896, 512, 256, 128) if wg.shape[2] % t == 0)
    nf = wg.shape[2] // tf
    routed = gate[:, :n_exp] != 0.0
    counts = jnp.sum(routed.reshape(nt, tm, n_exp), axis=1, dtype=jnp.int32)
    gate_t = jnp.transpose(gate[:, :n_exp])
    return pl.pallas_call(
        _moe_kernel,
        grid_spec=pltpu.PrefetchScalarGridSpec(
            num_scalar_prefetch=1,
            grid=(nt, n_exp, nf),
            in_specs=[pl.BlockSpec((tm, d), lambda i, e, j, c: (i, 0)),
                      pl.BlockSpec((tm, d), lambda i, e, j, c: (i, 0)),
                      pl.BlockSpec((tm, LANES), lambda i, e, j, c: (i, 0)),
                      pl.BlockSpec((n_exp, tm), lambda i, e, j, c: (0, i)),
                      pl.BlockSpec((None, d, tf), lambda i, e, j, c: (e, 0, j)),
                      pl.BlockSpec((None, d, tf), lambda i, e, j, c: (e, 0, j)),
                      pl.BlockSpec((None, tf, d), lambda i, e, j, c: (e, j, 0))],
            out_specs=pl.BlockSpec((tm, d), lambda i, e, j, c: (i, 0)),
            scratch_shapes=[pltpu.VMEM((tm, d), BF16), pltpu.VMEM((tm, d), F32)],
        ),
        out_shape=jax.ShapeDtypeStruct((n, d), F32),
        compiler_params=_params(),
        name="moe_ffn",
    )(counts, h, x, gate, gate_t, wg, wu, wd)


def _final_norm_kernel(x_ref, g_ref, o_ref):
    o_ref[...] = _rms(x_ref[...], g_ref[...])


def final_norm(x, g):
    n, d = x.shape
    tm = _row_tile(n, 1024)
    return pl.pallas_call(
        _final_norm_kernel,
        grid=(n // tm,),
        in_specs=[pl.BlockSpec((tm, d), lambda i: (i, 0)), pl.BlockSpec((1, d), lambda i: (0, 0))],
        out_specs=pl.BlockSpec((tm, d), lambda i: (i, 0)),
        out_shape=jax.ShapeDtypeStruct((n, d), F32),
        compiler_params=_params(),
        name="final_norm",
    )(x, g.reshape(1, d))


SLC_TILE = 512
WIN_TILE = 128


def _kv_prep_kernel(s_ref, w_ref, tab_ref, ska_ref, svt_ref, wka_ref, wvt_ref):
    tab = tab_ref[...]
    onehot, pieces = tab[:, :LANES], tab[:, LANES:]
    s = s_ref[...]
    w = w_ref[...]
    for g in range(N_KV_GROUPS):
        kc = slice(g * HEAD_DIM, (g + 1) * HEAD_DIM)
        vc = slice((N_KV_GROUPS + g) * HEAD_DIM, (N_KV_GROUPS + g + 1) * HEAD_DIM)
        ska_ref[g] = jnp.concatenate([onehot, s[:, kc].astype(BF16), pieces], axis=1)
        svt_ref[g] = s[:, vc].T.astype(BF16)
        wka_ref[g] = jnp.concatenate([w[:, kc].astype(BF16), pieces], axis=1)
        wv_t = w[:, vc].T.astype(BF16)
        for j in range(SLC_TILE // WIN_TILE):
            wvt_ref[g, j] = wv_t[:, j * WIN_TILE:(j + 1) * WIN_TILE]


def kv_prep(slc_rows, win_rows, tab, nb, t_len):
    tm = SLC_TILE
    nt = t_len // tm
    wpt = tm // WIN_TILE
    g3 = N_KV_GROUPS
    return pl.pallas_call(
        _kv_prep_kernel,
        grid=(nb, nt),
        in_specs=[pl.BlockSpec((tm, KV_ROW), lambda b, t: (b * nt + t, 0)),
                  pl.BlockSpec((tm, KV_ROW), lambda b, t: (b * nt + t, 0)),
                  pl.BlockSpec((tm, tab.shape[1]), lambda b, t: (t, 0))],
        out_specs=[pl.BlockSpec((None, g3, tm, 2 * LANES), lambda b, t: (b, 0, t, 0)),
                   pl.BlockSpec((None, g3, None, HEAD_DIM, tm), lambda b, t: (b, 0, t, 0, 0)),
                   pl.BlockSpec((None, g3, tm, LANES), lambda b, t: (b, 0, t, 0)),
                   pl.BlockSpec((None, g3, wpt, HEAD_DIM, WIN_TILE), lambda b, t: (b, 0, t, 0, 0))],
        out_shape=[jax.ShapeDtypeStruct((nb, g3, t_len, 2 * LANES), BF16),
                   jax.ShapeDtypeStruct((nb, g3, nt, HEAD_DIM, tm), BF16),
                   jax.ShapeDtypeStruct((nb, g3, t_len, LANES), BF16),
                   jax.ShapeDtypeStruct((nb, g3, t_len // WIN_TILE, HEAD_DIM, WIN_TILE), BF16)],
        compiler_params=_params(),
        name="kv_prep",
    )(slc_rows, win_rows, tab)


def _cmp_first_layer(get_cols, kv, pe_ref, w1_ref):
    acc_a = None
    acc_b = None
    for p in range(CMP_STRIDE):
        xs = jnp.concatenate(
            [get_cols(p * KV_ROW + (kv * N_KV_GROUPS + g) * HEAD_DIM) for g in range(N_KV_GROUPS)], axis=0)
        q = p + CMP_STRIDE
        da = _dot((xs + pe_ref[kv, p:p + 1, :]).astype(BF16), w1_ref[kv, p * HEAD_DIM:(p + 1) * HEAD_DIM, :])
        db = _dot((xs + pe_ref[kv, q:q + 1, :]).astype(BF16), w1_ref[kv, q * HEAD_DIM:(q + 1) * HEAD_DIM, :])
        acc_a = da if acc_a is None else acc_a + da
        acc_b = db if acc_b is None else acc_b + db
    return acc_a, acc_b


def _cmp_second_layer(kv, nch, a_ref, b_ref, w2_ref, emit):
    for g in range(N_KV_GROUPS):
        hid = a_ref[g, 0:nch, :] + b_ref[g, pl.ds(1, nch), :]
        hid = hid * jax.nn.sigmoid(hid)
        emit(kv, g, _dot(hid.astype(BF16), w2_ref[kv]))


def _compress_prompt_kernel(x_ref, pe_ref, w1_ref, w2_ref, cke_ref, cka_ref, cvt_ref, a_ref, b_ref, *, nch, tc):
    def emit(kv, g, out):
        if kv == 0:
            cka_ref[g] = jnp.concatenate([out.astype(BF16), cke_ref[...]], axis=1)
        else:
            cvt_ref[g] = out.T.astype(BF16)

    for kv in range(2):
        for ct in range(nch // tc):
            rows = slice(ct * tc, (ct + 1) * tc)
            acc_a, acc_b = _cmp_first_layer(lambda c0: x_ref[rows, c0:c0 + HEAD_DIM], kv, pe_ref, w1_ref)
            for g in range(N_KV_GROUPS):
                a_ref[g, rows, :] = acc_a[g * tc:(g + 1) * tc]
                b_ref[g, rows, :] = acc_b[g * tc:(g + 1) * tc]
        b_ref[:, nch:nch + 8, :] = jnp.zeros((N_KV_GROUPS, 8, b_ref.shape[2]), F32)
        _cmp_second_layer(kv, nch, a_ref, b_ref, w2_ref, emit)


def compress_prompt(cmp_rows, pe, w1, w2, cke, nb, t_len):
    nch = t_len // CMP_STRIDE
    tc = min(nch, 128)
    hid = w1.shape[2]
    xc = cmp_rows.reshape(nb, nch, CHUNK_ROW)
    full = lambda a: pl.BlockSpec(a.shape, lambda b: (0,) * a.ndim)
    return pl.pallas_call(
        functools.partial(_compress_prompt_kernel, nch=nch, tc=tc),
        grid=(nb,),
        in_specs=[pl.BlockSpec((None, nch, CHUNK_ROW), lambda b: (b, 0, 0)), full(pe), full(w1), full(w2), full(cke)],
        out_specs=[pl.BlockSpec((None, N_KV_GROUPS, nch, LANES), lambda b: (b, 0, 0, 0)),
                   pl.BlockSpec((None, N_KV_GROUPS, HEAD_DIM, nch), lambda b: (b, 0, 0, 0))],
        out_shape=[jax.ShapeDtypeStruct((nb, N_KV_GROUPS, nch, LANES), BF16),
                   jax.ShapeDtypeStruct((nb, N_KV_GROUPS, HEAD_DIM, nch), BF16)],
        scratch_shapes=[pltpu.VMEM((N_KV_GROUPS, nch + 8, hid), F32)] * 2,
        compiler_params=_params(),
        name="compress_prompt",
    )(xc, pe, w1, w2, cke)


def _compress_sample_kernel(pt_ref, *refs, n_pages):
    page_refs = refs[:n_pages]
    new_ref, pe_ref, w1_ref, w2_ref, ck_ref, cv_ref, a_ref, b_ref = refs[n_pages:]
    nch = n_pages * (PAGE_SIZE // CMP_STRIDE)
    for kv in range(2):
        acc_a, acc_b = _cmp_first_layer(
            lambda c0: jnp.concatenate([pr[:, c0:c0 + HEAD_DIM] for pr in page_refs], axis=0), kv, pe_ref, w1_ref)
        _, new_b = _cmp_first_layer(
            lambda c0: jnp.broadcast_to(new_ref[:, c0:c0 + HEAD_DIM], (8, HEAD_DIM)), kv, pe_ref, w1_ref)
        for g in range(N_KV_GROUPS):
            a_ref[g, 0:nch, :] = acc_a[g * nch:(g + 1) * nch]
            b_ref[g, 0:nch, :] = acc_b[g * nch:(g + 1) * nch]
            b_ref[g, nch:nch + 8, :] = new_b[g * 8:(g + 1) * 8]

        def emit(kv, g, out):
            (ck_ref, cv_ref)[kv][g] = out.astype(BF16)

        _cmp_second_layer(kv, nch, a_ref, b_ref, w2_ref, emit)


def compress_sample(pool, page_table, cmp_new, pe, w1, w2):
    nseq, n_pages = page_table.shape
    dec = cmp_new.shape[1]
    cpp = PAGE_SIZE // CMP_STRIDE
    nch = n_pages * cpp
    hid = w1.shape[2]
    pool_c = pool.reshape(pool.shape[0], cpp, CHUNK_ROW)
    new_c = jnp.pad(cmp_new.reshape(nseq, 1, dec * KV_ROW), ((0, 0), (0, 0), (0, CHUNK_ROW - dec * KV_ROW)))
    page_spec = lambda p: pl.BlockSpec((None, cpp, CHUNK_ROW), lambda b, pt: (pt[b, p], 0, 0))
    full = lambda a: pl.BlockSpec(a.shape, lambda b, pt: (0,) * a.ndim)
    o_spec = pl.BlockSpec((None, N_KV_GROUPS, nch, HEAD_DIM), lambda b, pt: (b, 0, 0, 0))
    o_shape = jax.ShapeDtypeStruct((nseq, N_KV_GROUPS, nch, HEAD_DIM), BF16)
    return pl.pallas_call(
        functools.partial(_compress_sample_kernel, n_pages=n_pages),
        grid_spec=pltpu.PrefetchScalarGridSpec(
            num_scalar_prefetch=1,
            grid=(nseq,),
            in_specs=[page_spec(p) for p in range(n_pages)]
            + [pl.BlockSpec((None, 1, CHUNK_ROW), lambda b, pt: (b, 0, 0)), full(pe), full(w1), full(w2)],
            out_specs=[o_spec, o_spec],
            scratch_shapes=[pltpu.VMEM((N_KV_GROUPS, nch + 8, hid), F32)] * 2,
        ),
        out_shape=[o_shape, o_shape],
        compiler_params=_params(),
        name="compress_sample",
    )(page_table, *([pool_c] * n_pages), new_c, pe, w1, w2)


def _masked_softmax(s, mask, axis=-1):
    m = jnp.max(s, axis=axis, keepdims=True)
    e = jnp.where(mask, jnp.exp(s - m), 0.0)
    d = jnp.sum(e, axis=axis, keepdims=True)
    return e / jnp.where(d > 0, d, 1.0)


def _select_blocks(imp, cur, axis=-1):
    axis = axis % imp.ndim
    j = lax.broadcasted_iota(jnp.int32, imp.shape, axis)
    valid = j <= cur
    forced = valid & ((j == 0) | (j == cur) | (j == cur - 1))
    v = jnp.where(forced, jnp.inf, jnp.where(valid, imp, -jnp.inf))
    sel = jnp.zeros(imp.shape, F32)
    for _ in range(N_SEL):
        m = jnp.max(v, axis=axis, keepdims=True)
        jm = jnp.min(jnp.where(v == m, j, LANES), axis=axis, keepdims=True)
        pick = j == jm
        v = jnp.where(pick, -jnp.inf, v)
        sel = jnp.where(pick, 1.0, sel)
    return (sel > 0.0) & valid


def _stack_heads(q):
    return jnp.concatenate([q[:, r * HEAD_DIM:(r + 1) * HEAD_DIM] for r in range(HEADS_PER_GROUP)], axis=0)


def _importance(p_c, tq, mimp_ref):
    pcs = p_c[0:tq]
    for r in range(1, HEADS_PER_GROUP):
        pcs = pcs + p_c[r * tq:(r + 1) * tq]
    hi, lo = _split_hi_lo(pcs)
    return _dot(hi, mimp_ref[...]) + _dot(lo, mimp_ref[...])


def _gated_merge(gsig, g, o_c, o_s, o_w, tq, store):
    for r in range(HEADS_PER_GROUP):
        c = 3 * (g * HEADS_PER_GROUP + r)
        rows = slice(r * tq, (r + 1) * tq)
        store(r, gsig[:, c:c + 1] * o_c[rows] + gsig[:, c + 1:c + 2] * o_s[rows] + gsig[:, c + 2:c + 3] * o_w[rows])


def _nsa_prompt_kernel(q_ref, glt_ref, qe_ref, cka_ref, cvt_ref, ska_ref, svt_ref, wka_ref, wvt_ref,
                       mimpt_ref, o_ref, *, t_len):
    i = pl.program_id(2)
    tq = Q_BLOCK
    rows = HEADS_PER_GROUP * tq
    ncp = t_len // CMP_STRIDE
    s0 = i * tq
    q4 = (_stack_heads(q_ref[...]) * (HEAD_DIM ** -0.5)).astype(BF16)
    qb = jnp.concatenate([q4, qe_ref[...]], axis=1)
    qpos = s0 + (lax.broadcasted_iota(jnp.int32, (1, rows), 1) & (tq - 1))

    cpos = lax.broadcasted_iota(jnp.int32, (ncp, 1), 0) * CMP_STRIDE + (CMP_BLOCK - 1)
    mask_c = cpos <= qpos
    p_c = _masked_softmax(jnp.where(mask_c, _dot_nt(cka_ref[...], qb), NEG), mask_c, axis=0)
    o_c = _dot(cvt_ref[...], p_c.astype(BF16))

    pcs = p_c[:, 0:tq]
    for r in range(1, HEADS_PER_GROUP):
        pcs = pcs + p_c[:, r * tq:(r + 1) * tq]
    hi, lo = _split_hi_lo(pcs)
    imp = _dot(mimpt_ref[...], hi) + _dot(mimpt_ref[...], lo)
    sel = _select_blocks(imp, qpos[:, 0:tq] >> SLC_SHIFT, axis=0)
    selbias = jnp.where(sel, 0.0, NEG).T.astype(BF16)
    qa = jnp.concatenate([jnp.concatenate([selbias] * HEADS_PER_GROUP, axis=0), qb], axis=1)

    def scores(kt):
        return _dot_nt(ska_ref[pl.ds(pl.multiple_of(kt * SLC_TILE, SLC_TILE), SLC_TILE), :], qa)

    def flash_update(stats, s, vt):
        m, l, acc = stats
        m_new = jnp.maximum(m, jnp.max(s, axis=0, keepdims=True))
        a = jnp.exp(m - m_new)
        p = jnp.exp(s - m_new)
        return m_new, a * l + jnp.sum(p, axis=0, keepdims=True), a * acc + _dot(vt, p.astype(BF16))

    def bulk(kt, carry):
        s_cur, stats = carry
        s_next = scores(kt + 1)
        return s_next, flash_update(stats, s_cur, svt_ref[kt])

    stats = (jnp.full((1, rows), M_INIT, F32), jnp.zeros((1, rows), F32), jnp.zeros((HEAD_DIM, rows), F32))
    kt_last = s0 // SLC_TILE
    s_last, stats = lax.fori_loop(0, kt_last, bulk, (scores(0), stats))
    kpos = kt_last * SLC_TILE + lax.broadcasted_iota(jnp.int32, (SLC_TILE, 1), 0)
    _, l_s, acc_s = flash_update(stats, jnp.where(kpos <= qpos, s_last, NEG), svt_ref[kt_last])
    o_s = acc_s / l_s

    nwt = (WINDOW + tq) // WIN_TILE
    wb = jnp.maximum(i - WINDOW // WIN_TILE, 0)
    kw = pl.ds(pl.multiple_of(wb * WIN_TILE, WIN_TILE), nwt * WIN_TILE)
    dist = qpos - (wb * WIN_TILE + lax.broadcasted_iota(jnp.int32, (nwt * WIN_TILE, 1), 0))
    mask_w = (dist >= 0) & (dist < WINDOW)
    p_w = _masked_softmax(jnp.where(mask_w, _dot_nt(wka_ref[kw, :], qb), NEG), mask_w, axis=0).astype(BF16)
    o_w = _dot(wvt_ref[wb], p_w[0:WIN_TILE])
    for w in range(1, nwt):
        o_w = o_w + _dot(wvt_ref[wb + w], p_w[w * WIN_TILE:(w + 1) * WIN_TILE])

    gsig = jax.nn.sigmoid(glt_ref[...])
    lane_gate = lambda c: jnp.concatenate([gsig[3 * r + c:3 * r + c + 1, :] for r in range(HEADS_PER_GROUP)], axis=1)
    out_t = lane_gate(0) * o_c + lane_gate(1) * o_s + lane_gate(2) * o_w
    for r in range(HEADS_PER_GROUP):
        o_ref[:, r * HEAD_DIM:(r + 1) * HEAD_DIM] = out_t[:, r * tq:(r + 1) * tq].T.astype(o_ref.dtype)


def nsa_prompt(q, gl_t, cka, cvt, ska, svt, wka, wvt, consts, nb, t_len):
    nqb = t_len // Q_BLOCK
    rows = HEADS_PER_GROUP * Q_BLOCK
    gw = 3 * HEADS_PER_GROUP
    qe, mimp_t = consts
    per_bg = lambda a: pl.BlockSpec((None, None) + a.shape[2:], lambda b, g, i: (b, g) + (0,) * (a.ndim - 2))
    return pl.pallas_call(
        functools.partial(_nsa_prompt_kernel, t_len=t_len),
        grid=(nb, N_KV_GROUPS, nqb),
        in_specs=[pl.BlockSpec((Q_BLOCK, HEADS_PER_GROUP * HEAD_DIM), lambda b, g, i: (b * nqb + i, g)),
                  pl.BlockSpec((None, gw, Q_BLOCK), lambda b, g, i: (g, 0, b * nqb + i)),
                  pl.BlockSpec((None, rows, HEAD_DIM), lambda b, g, i: (g, 0, 0)),
                  per_bg(cka), per_bg(cvt), per_bg(ska), per_bg(svt), per_bg(wka), per_bg(wvt),
                  pl.BlockSpec(mimp_t.shape, lambda b, g, i: (0, 0))],
        out_specs=pl.BlockSpec((Q_BLOCK, HEADS_PER_GROUP * HEAD_DIM), lambda b, g, i: (b * nqb + i, g)),
        out_shape=jax.ShapeDtypeStruct((nb * t_len, NSA_W), BF16),
        compiler_params=_params(),
        name="nsa_prompt",
    )(q, gl_t, qe, cka, cvt, ska, svt, wka, wvt, mimp_t)


def _nsa_sample_kernel(pt_ref, *refs, n_pages, dec, win_buf):
    page_refs = refs[:n_pages]
    (q_ref, gl_ref, slp_ref, ck_ref, cv_ref, snew_ref, wc_ref, wnew_ref, es_ref, mimp_ref, o_ref) = refs[n_pages:]
    tq = dec
    rows = HEADS_PER_GROUP * tq
    past = n_pages * PAGE_SIZE
    ncs = ck_ref.shape[1]
    t_row = lax.broadcasted_iota(jnp.int32, (rows, 1), 0) & (tq - 1)
    qpos = past + t_row
    gsig = jax.nn.sigmoid(gl_ref[...])
    q_all = q_ref[...]
    snew = snew_ref[...]
    wc = wc_ref[...]
    wnew = wnew_ref[...]
    tnew = lax.broadcasted_iota(jnp.int32, (1, tq), 1)
    mask_new = tnew <= t_row
    dist_new = (t_row - tnew).astype(F32)

    q4s, o_cs, imps = [], [], []
    for g in range(N_KV_GROUPS):
        q4 = (_stack_heads(q_all[:, g * HEADS_PER_GROUP * HEAD_DIM:(g + 1) * HEADS_PER_GROUP * HEAD_DIM])
              * (HEAD_DIM ** -0.5)).astype(BF16)
        cpos = lax.broadcasted_iota(jnp.int32, (1, ncs), 1) * CMP_STRIDE + (CMP_BLOCK - 1)
        mask_c = cpos <= qpos
        s_c = _dot_nt(q4, ck_ref[g]) - slp_ref[g] * (qpos - cpos).astype(F32)
        p_c = _masked_softmax(jnp.where(mask_c, s_c, NEG), mask_c)
        q4s.append(q4)
        o_cs.append(_dot(p_c.astype(BF16), cv_ref[g]))
        imps.append(_importance(p_c, tq, mimp_ref))
    cur = (past + (lax.broadcasted_iota(jnp.int32, (1, N_KV_GROUPS * tq), 1) & (tq - 1))) >> SLC_SHIFT
    sel_t = _select_blocks(jnp.concatenate(imps, axis=0).T, cur, axis=0)
    selbias_all = jnp.where(sel_t, 0.0, NEG).T

    for g in range(N_KV_GROUPS):
        kc = slice(g * HEAD_DIM, (g + 1) * HEAD_DIM)
        vc = slice((N_KV_GROUPS + g) * HEAD_DIM, (N_KV_GROUPS + g + 1) * HEAD_DIM)
        q4, o_c = q4s[g], o_cs[g]
        slope = slp_ref[g]
        sb4 = jnp.concatenate([selbias_all[g * tq:(g + 1) * tq].astype(BF16)] * HEADS_PER_GROUP, axis=0)

        k_past = jnp.concatenate([pr[:, kc] for pr in page_refs], axis=0).astype(BF16)
        v_past = jnp.concatenate([pr[:, vc] for pr in page_refs], axis=0).astype(BF16)
        kpos = lax.broadcasted_iota(jnp.int32, (1, past), 1)
        s_p = _dot_nt(q4, k_past) + _dot_nt(sb4, es_ref[...]) - slope * (qpos - kpos).astype(F32)
        s_n = _dot_nt(q4, snew[:, kc].astype(BF16)) - slope * dist_new
        s_n = jnp.where(mask_new, s_n, NEG)
        m = jnp.maximum(jnp.max(s_p, axis=-1, keepdims=True), jnp.max(s_n, axis=-1, keepdims=True))
        e_p = jnp.exp(s_p - m)
        e_n = jnp.exp(s_n - m)
        l = jnp.sum(e_p, axis=-1, keepdims=True) + jnp.sum(e_n, axis=-1, keepdims=True)
        o_s = (_dot(e_p.astype(BF16), v_past) + _dot(e_n.astype(BF16), snew[:, vc].astype(BF16))) / l

        wdist = qpos - (past - win_buf + lax.broadcasted_iota(jnp.int32, (1, win_buf), 1))
        mask_w = wdist < WINDOW
        s_w = _dot_nt(q4, wc[:, kc].astype(BF16)) - slope * wdist.astype(F32)
        s_w = jnp.where(mask_w, s_w, NEG)
        s_wn = _dot_nt(q4, wnew[:, kc].astype(BF16)) - slope * dist_new
        s_wn = jnp.where(mask_new, s_wn, NEG)
        m = jnp.maximum(jnp.max(s_w, axis=-1, keepdims=True), jnp.max(s_wn, axis=-1, keepdims=True))
        e_w = jnp.exp(s_w - m)
        e_wn = jnp.exp(s_wn - m)
        l = jnp.sum(e_w, axis=-1, keepdims=True) + jnp.sum(e_wn, axis=-1, keepdims=True)
        o_w = (_dot(e_w.astype(BF16), wc[:, vc].astype(BF16)) + _dot(e_wn.astype(BF16), wnew[:, vc].astype(BF16))) / l

        def store(r, val, g=g):
            c = (g * HEADS_PER_GROUP + r) * HEAD_DIM
            o_ref[:, c:c + HEAD_DIM] = val.astype(o_ref.dtype)

        _gated_merge(gsig, g, o_c, o_s, o_w, tq, store)


def nsa_sample(q, gl, ck, cv, pool, page_table, slc_rows, win_cache, win_rows, consts, row_off, dec):
    nseq, n_pages = page_table.shape
    win_buf = win_cache.shape[1]
    slp, es, mimp = consts
    full = lambda a: pl.BlockSpec(a.shape, lambda b, pt: (0,) * a.ndim)
    row = lambda c: pl.BlockSpec((dec, c), lambda b, pt: (row_off + b, 0))
    seq4 = lambda a: pl.BlockSpec((None,) + a.shape[1:], lambda b, pt: (b, 0, 0, 0))
    page_spec = lambda p: pl.BlockSpec((None, PAGE_SIZE, KV_ROW), lambda b, pt: (pt[b, p], 0, 0))
    return pl.pallas_call(
        functools.partial(_nsa_sample_kernel, n_pages=n_pages, dec=dec, win_buf=win_buf),
        grid_spec=pltpu.PrefetchScalarGridSpec(
            num_scalar_prefetch=1,
            grid=(nseq,),
            in_specs=[page_spec(p) for p in range(n_pages)]
            + [row(NSA_W), row(GATE_W), full(slp), seq4(ck), seq4(cv), row(KV_ROW),
               pl.BlockSpec((None, win_buf, KV_ROW), lambda b, pt: (b, 0, 0)), row(KV_ROW), full(es), full(mimp)],
            out_specs=pl.BlockSpec((dec, NSA_W), lambda b, pt: (b, 0)),
        ),
        out_shape=jax.ShapeDtypeStruct((nseq * dec, NSA_W), F32),
        compiler_params=_params(),
        name="nsa_sample",
    )(page_table, *([pool] * n_pages), q, gl, slp, ck, cv, slc_rows, win_cache, win_rows, es, mimp)


def _alibi_slopes():
    return 2.0 ** (-ALIBI_MAX_BIAS * jnp.arange(1, N_HEADS + 1, dtype=F32) / N_HEADS)


def _slope_rows(tq):
    return jnp.repeat(_alibi_slopes().reshape(N_KV_GROUPS, HEADS_PER_GROUP), tq, axis=1)[..., None]


def _importance_matrix(n_cmp):
    n = jnp.arange(n_cmp)[:, None]
    j = jnp.arange(LANES)[None, :]
    per_sel = SLC_BLOCK // CMP_STRIDE
    inside = (SLC_BLOCK - CMP_BLOCK) // CMP_STRIDE + 1
    return ((n // per_sel == j) & (n % per_sel < inside)).astype(BF16)


def _block_onehot(n_keys):
    k = jnp.arange(n_keys)[:, None]
    return (k // SLC_BLOCK == jnp.arange(LANES)[None, :]).astype(BF16)


def _position_pieces(pos):
    pos = pos[:, None]
    hi = ((pos // SLC_BLOCK) * SLC_BLOCK).astype(BF16)
    lo = (pos % SLC_BLOCK).astype(BF16)
    pieces = jnp.concatenate([hi, hi, hi, lo, lo, lo], axis=-1)
    return jnp.pad(pieces, ((0, 0), (0, HEAD_DIM - pieces.shape[-1])))


def _prompt_tables(t_len):
    slp = _slope_rows(Q_BLOCK)
    s1 = slp.astype(BF16)
    s2 = (slp - s1.astype(F32)).astype(BF16)
    s3 = (slp - s1.astype(F32) - s2.astype(F32)).astype(BF16)
    qe = jnp.concatenate([s1, s2, s3, s1, s2, s3], axis=-1)
    qe = jnp.pad(qe, ((0, 0), (0, 0), (0, HEAD_DIM - qe.shape[-1])))
    key_tab = jnp.concatenate([_block_onehot(t_len), _position_pieces(jnp.arange(t_len))], axis=-1)
    ncp = t_len // CMP_STRIDE
    cmp_tab = _position_pieces(jnp.arange(ncp) * CMP_STRIDE + (CMP_BLOCK - 1))
    return (qe, jnp.transpose(_importance_matrix(ncp))), key_tab, cmp_tab


def kernel(x_prompt, x_sample, state_conv, cache_mem_kv, cache_cmp_kv, cache_slc_kv, cache_win_kv, page_table,
           mem_prompt, norm1, norm2, norm_final, w_in_a, conv_w, conv_b, conv_ln_g, conv_ln_b, w_in_b, kv_norm,
           w_kv_shared, cmp_pos, cmp_w1, cmp_w2, w_mem_kv, w_out, ffn_w_gate, ffn_w_up, ffn_w_down, moe_router,
           moe_w_gate, moe_w_up, moe_w_down):
    bp, t_len, d = x_prompt.shape
    db, dec, _ = x_sample.shape
    depth = norm1.shape[0]
    n_a = w_in_a.shape[0]
    n_mem = mem_prompt.shape[1]
    n_p = bp * t_len
    n_s = db * dec
    n_pages = page_table.shape[1]
    past = n_pages * PAGE_SIZE
    win_buf = cache_win_kv.shape[1]
    bf = lambda a: a.astype(BF16)

    x = jnp.concatenate([x_prompt.reshape(n_p, d), x_sample.reshape(n_s, d)], axis=0)

    mem_kv_p = mem_kv_proj(mem_prompt.reshape(bp * n_mem, d), w_mem_kv)
    mem_kv_p = mem_kv_p.reshape(depth, bp, n_mem, 2 * MEM_W)
    mem_kv_s = cache_mem_kv.reshape(depth, db, n_mem, 2 * MEM_W)

    conv_p, conv_s = [], []
    zero_state = jnp.zeros((bp, CONV_STATE, C_CONV), F32)
    cmp_rows = slc_rows = win_rows = None
    nsa_ctx = None

    for l in range(depth):
        if l < n_a:
            uc, qm = norm_proj(x, norm1[l], bf(w_in_a[l]), (2 * C_CONV, MEM_W))
            mix_p, st_p = conv_mixer(uc, zero_state, conv_w[l], conv_b[l], conv_ln_g[l], conv_ln_b[l],
                                     bp, t_len, Q_BLOCK)
            mix_s, st_s = conv_mixer(uc, state_conv[l], conv_w[l], conv_b[l], conv_ln_g[l], conv_ln_b[l],
                                     db, dec, dec, row0=n_p)
            conv_p.append(st_p)
            conv_s.append(st_s)
        else:
            if nsa_ctx is None:
                cmp_rows, slc_rows, win_rows = norm_proj(x, kv_norm, bf(w_kv_shared), (KV_ROW,) * 3)
                w1 = bf(cmp_w1)
                w2 = bf(cmp_w2)
                prompt_consts, key_tab, cmp_tab = _prompt_tables(t_len)
                cka, cvt = compress_prompt(cmp_rows[:n_p], cmp_pos, w1, w2, cmp_tab, bp, t_len)
                ck_s, cv_s = compress_sample(cache_cmp_kv.reshape(-1, PAGE_SIZE, KV_ROW), page_table,
                                             cmp_rows[n_p:].reshape(db, dec, KV_ROW), cmp_pos, w1, w2)
                ska, svt, wka, wvt = kv_prep(slc_rows, win_rows, key_tab, bp, t_len)
                nsa_ctx = dict(
                    prompt=prompt_consts,
                    sample=(_slope_rows(dec), _block_onehot(past), _importance_matrix(ck_s.shape[2])),
                    pool=cache_slc_kv.reshape(-1, PAGE_SIZE, KV_ROW),
                    win_cache=cache_win_kv.reshape(db, win_buf, KV_ROW))
            w_in = w_in_b[l - n_a]
            w_in = jnp.concatenate([w_in[:, :NSA_W], w_in[:, NSA_W + GATE_W:], w_in[:, NSA_W:NSA_W + GATE_W]], axis=1)
            q, qm, gl = norm_proj(x, norm1[l], bf(w_in), (NSA_W, MEM_W, GATE_W))
            gw = 3 * HEADS_PER_GROUP
            gl_t = jnp.transpose(gl[:n_p].reshape(n_p, N_KV_GROUPS, gw), (1, 2, 0))
            mix_p = nsa_prompt(q, gl_t, cka, cvt, ska, svt, wka, wvt, nsa_ctx["prompt"], bp, t_len)
            mix_s = nsa_sample(q, gl, ck_s, cv_s, nsa_ctx["pool"], page_table, slc_rows, nsa_ctx["win_cache"],
                               win_rows, nsa_ctx["sample"], n_p // dec, dec)
        mo_p = mem_attend(qm, mem_kv_p[l], bp, t_len, 512)
        mo_s = mem_attend(qm, mem_kv_s[l], db, dec, dec, spb=MEM_SEQS_PER_STEP, row0=n_p)
        if l % 2 == 0:
            xn, h2 = outproj_norm(x, mix_p, mix_s, mo_p, mo_s, bf(w_out[l]), norm2[l])
            x = ffn_dense(h2, xn, bf(ffn_w_gate[l // 2]), bf(ffn_w_up[l // 2]), bf(ffn_w_down[l // 2]))
        else:
            xn, h2, gate = outproj_norm(x, mix_p, mix_s, mo_p, mo_s, bf(w_out[l]), norm2[l],
                                        router=moe_router[l // 2])
            x = moe_ffn(h2, xn, gate, bf(moe_w_gate[l // 2]), bf(moe_w_up[l // 2]), bf(moe_w_down[l // 2]))

    y = final_norm(x, norm_final)
    kv5 = lambda a, b, t: a.reshape(b, t, 2, N_KV_GROUPS, HEAD_DIM)
    keep_p = min(WINDOW, t_len)
    win_p = kv5(win_rows[:n_p], bp, t_len)[:, t_len - keep_p:]
    win_all = jnp.concatenate([cache_win_kv, kv5(win_rows[n_p:], db, dec)], axis=1)
    keep_s = min(WINDOW, past + dec)
    return (y[:n_p].reshape(bp, t_len, d), y[n_p:].reshape(db, dec, d),
            jnp.stack(conv_p), jnp.stack(conv_s),
            mem_kv_p.reshape(depth, bp, n_mem, 2, MEM_HEADS, MEM_HEAD_DIM),
            kv5(cmp_rows[:n_p], bp, t_len), kv5(cmp_rows[n_p:], db, dec),
            kv5(slc_rows[:n_p], bp, t_len), kv5(slc_rows[n_p:], db, dec),
            win_p, win_all[:, win_all.shape[1] - keep_s:])
```

```python
import functools

import jax
import jax.numpy as jnp
from jax import lax
from jax.experimental import pallas as pl
from jax.experimental.pallas import tpu as pltpu

F32 = jnp.float32
BF16 = jnp.bfloat16

HEAD_DIM = 64
N_KV_GROUPS = 3
HEADS_PER_GROUP = 4
N_HEADS = N_KV_GROUPS * HEADS_PER_GROUP
NSA_W = N_HEADS * HEAD_DIM
GATE_W = 3 * N_HEADS
CMP_STRIDE = 16
CMP_BLOCK = 32
SLC_BLOCK = 64
SLC_SHIFT = 6
N_SEL = 16
WINDOW = 512
Q_BLOCK = 128
ALIBI_MAX_BIAS = 8.0
MEM_HEADS = 4
MEM_HEAD_DIM = 64
MEM_W = MEM_HEADS * MEM_HEAD_DIM
MEM_SEQS_PER_STEP = 8
C_CONV = 768
CONV_K = 31
CONV_STATE = CONV_K - 1
N_EXPERTS = 8
EPS = 1e-6
PAGE_SIZE = 128
KV_ROW = 2 * N_KV_GROUPS * HEAD_DIM
CHUNK_ROW = CMP_STRIDE * KV_ROW
LANES = 128
SUBLANES = 8
NEG = -1e30
M_INIT = -1e29
VMEM_LIMIT = 56 * 1024 * 1024


def _dot(a, b):
    return jnp.dot(a, b, preferred_element_type=F32)


def _dot_nt(a, b):
    return lax.dot_general(a, b, (((1,), (1,)), ((), ())), preferred_element_type=F32)


def _split_hi_lo(x):
    hi = x.astype(BF16)
    lo = (x - hi.astype(F32)).astype(BF16)
    return hi, lo


def _rms(x, g):
    return x * lax.rsqrt(jnp.mean(x * x, axis=-1, keepdims=True) + EPS) * g


def _params(**kw):
    return pltpu.CompilerParams(vmem_limit_bytes=VMEM_LIMIT, **kw)


def _log2(n):
    assert n > 0 and n & (n - 1) == 0, n
    return n.bit_length() - 1


def _row_tile(n, pref):
    tm = pref
    while n % tm:
        tm //= 2
    assert tm >= 8, (n, pref)
    return tm


def _norm_proj_kernel(x_ref, g_ref, w_ref, *o_refs, splits):
    h = _rms(x_ref[...], g_ref[...])
    y = _dot(h.astype(BF16), w_ref[...])
    off = 0
    for o_ref, n in zip(o_refs, splits):
        o_ref[...] = y[:, off:off + n]
        off += n


def norm_proj(x, g, w, splits):
    n, d = x.shape
    tm = _row_tile(n, 512)
    return pl.pallas_call(
        functools.partial(_norm_proj_kernel, splits=splits),
        grid=(n // tm,),
        in_specs=[pl.BlockSpec((tm, d), lambda i: (i, 0)),
                  pl.BlockSpec((1, d), lambda i: (0, 0)),
                  pl.BlockSpec(w.shape, lambda i: (0, 0))],
        out_specs=[pl.BlockSpec((tm, s), lambda i: (i, 0)) for s in splits],
        out_shape=[jax.ShapeDtypeStruct((n, s), F32) for s in splits],
        compiler_params=_params(),
        name="norm_proj",
    )(x, g.reshape(1, d), w)


def _mem_kv_kernel(x_ref, w_ref, o_ref):
    o_ref[...] = _dot(x_ref[...].astype(BF16), w_ref[...].astype(BF16))


def mem_kv_proj(x, w):
    depth, d, e = w.shape
    n = x.shape[0]
    return pl.pallas_call(
        _mem_kv_kernel,
        grid=(depth,),
        in_specs=[pl.BlockSpec((n, d), lambda l: (0, 0)),
                  pl.BlockSpec((None, d, e), lambda l: (l, 0, 0))],
        out_specs=pl.BlockSpec((None, n, e), lambda l: (l, 0, 0)),
        out_shape=jax.ShapeDtypeStruct((depth, n, e), F32),
        compiler_params=_params(),
        name="mem_kv_proj",
    )(x, w)


CONV_PAD = 32


def _conv_kernel(u_ref, st_ref, w_ref, b_ref, lg_ref, lb_ref, mix_ref, ns_ref, vp_ref, *, tt, rc):
    t = pl.program_id(1)
    lo = CONV_PAD - CONV_STATE

    @pl.when(t == 0)
    def _():
        vp_ref[lo:CONV_PAD, :] = st_ref[...]

    u = u_ref[...]
    vp_ref[CONV_PAD:CONV_PAD + tt, :] = u[:, :C_CONV] * jax.nn.sigmoid(u[:, C_CONV:])
    vp_ref[CONV_PAD + tt:CONV_PAD + tt + SUBLANES, :] = jnp.zeros((SUBLANES, C_CONV), F32)
    for c in range(tt // rc):
        acc = None
        for b in range(SUBLANES):
            part = None
            for k in range(CONV_K):
                a, phase = divmod(lo + k, SUBLANES)
                if phase == b:
                    term = w_ref[k:k + 1, :] * vp_ref[pl.ds(c * rc + SUBLANES * a, rc + SUBLANES), :]
                    part = term if part is None else part + term
            acc = part[b:b + rc] if acc is None else acc + part[b:b + rc]
        h = acc + b_ref[...]
        mu = jnp.mean(h, axis=-1, keepdims=True)
        hc = h - mu
        var = jnp.mean(hc * hc, axis=-1, keepdims=True)
        y = hc * lax.rsqrt(var + EPS) * lg_ref[...] + lb_ref[...]
        mix_ref[c * rc:(c + 1) * rc, :] = y * jax.nn.sigmoid(y)
    new_state = vp_ref[pl.ds(lo + tt, CONV_STATE), :]
    ns_ref[...] = new_state
    vp_ref[lo:CONV_PAD, :] = new_state


def conv_mixer(u, state, w, b, lg, lb, nseq, t_len, tt, row0=0):
    rc = min(tt, 32)
    nt = t_len // tt
    blk0 = row0 // tt
    vec = lambda a: a.reshape(1, C_CONV)
    return pl.pallas_call(
        functools.partial(_conv_kernel, tt=tt, rc=rc),
        grid=(nseq, nt),
        in_specs=[pl.BlockSpec((tt, 2 * C_CONV), lambda s, t: (blk0 + s * nt + t, 0)),
                  pl.BlockSpec((None, CONV_STATE, C_CONV), lambda s, t: (s, 0, 0)),
                  pl.BlockSpec((CONV_K, C_CONV), lambda s, t: (0, 0)),
                  pl.BlockSpec((1, C_CONV), lambda s, t: (0, 0)),
                  pl.BlockSpec((1, C_CONV), lambda s, t: (0, 0)),
                  pl.BlockSpec((1, C_CONV), lambda s, t: (0, 0))],
        out_specs=[pl.BlockSpec((tt, C_CONV), lambda s, t: (s * nt + t, 0)),
                   pl.BlockSpec((None, CONV_STATE, C_CONV), lambda s, t: (s, 0, 0))],
        out_shape=[jax.ShapeDtypeStruct((nseq * t_len, C_CONV), F32),
                   jax.ShapeDtypeStruct((nseq, CONV_STATE, C_CONV), F32)],
        scratch_shapes=[pltpu.VMEM((CONV_PAD + tt + SUBLANES, C_CONV), F32)],
        compiler_params=_params(),
        name="conv_mixer",
    )(u, state, w, vec(b), vec(lg), vec(lb))


def _mem_attn_kernel(q_ref, kv_ref, o_ref, *, spb, tq):
    n_mem = kv_ref.shape[1]
    q = q_ref[...]
    kv = kv_ref[...].reshape(spb * n_mem, 2 * MEM_W).astype(BF16)
    own = None
    if spb > 1:
        q_seq = lax.broadcasted_iota(jnp.int32, (spb * tq, 1), 0) >> _log2(tq)
        k_seq = lax.broadcasted_iota(jnp.int32, (1, spb * n_mem), 1) >> _log2(n_mem)
        own = q_seq == k_seq
    for h in range(MEM_HEADS):
        c = h * MEM_HEAD_DIM
        qh = q[:, c:c + MEM_HEAD_DIM].astype(BF16)
        s = _dot_nt(qh, kv[:, c:c + MEM_HEAD_DIM]) * (MEM_HEAD_DIM ** -0.5)
        if own is not None:
            s = jnp.where(own, s, NEG)
        e = jnp.exp(s - jnp.max(s, axis=-1, keepdims=True))
        o = _dot(e.astype(BF16), kv[:, MEM_W + c:MEM_W + c + MEM_HEAD_DIM])
        o_ref[:, c:c + MEM_HEAD_DIM] = o / jnp.sum(e, axis=-1, keepdims=True)


def mem_attend(qm, kv, nseq, t_len, tq, spb=1, row0=0):
    nt = t_len // tq
    assert spb == 1 or nt == 1
    n_mem = kv.shape[1]
    rows = spb * tq
    blk0 = row0 // rows
    return pl.pallas_call(
        functools.partial(_mem_attn_kernel, spb=spb, tq=tq),
        grid=(nseq // spb, nt),
        in_specs=[pl.BlockSpec((rows, MEM_W), lambda s, t: (blk0 + s * nt + t, 0)),
                  pl.BlockSpec((spb, n_mem, 2 * MEM_W), lambda s, t: (s, 0, 0))],
        out_specs=pl.BlockSpec((rows, MEM_W), lambda s, t: (s * nt + t, 0)),
        out_shape=jax.ShapeDtypeStruct((nseq * t_len, MEM_W), F32),
        compiler_params=_params(),
        name="mem_attend",
    )(qm, kv)


def _outproj_kernel(x_ref, mixp_ref, mixs_ref, mop_ref, mos_ref, w_ref, g_ref, *rest, moe, prompt_tiles):
    if moe:
        r_ref, xn_ref, h_ref, gate_ref = rest
    else:
        xn_ref, h_ref = rest
    is_prompt = pl.program_id(0) < prompt_tiles
    mix = jnp.where(is_prompt, mixp_ref[...].astype(BF16), mixs_ref[...].astype(BF16))
    mo = jnp.where(is_prompt, mop_ref[...].astype(BF16), mos_ref[...].astype(BF16))
    d_mix = mix.shape[1]
    y = _dot(mix, w_ref[0:d_mix, :]) + _dot(mo, w_ref[d_mix:, :])
    xn = x_ref[...] + y
    xn_ref[...] = xn
    h = _rms(xn, g_ref[...])
    h_ref[...] = h.astype(BF16)
    if moe:
        h_hi, h_lo = _split_hi_lo(h)
        r_hi, r_lo = _split_hi_lo(r_ref[...])
        logits = _dot(h_hi, r_hi) + _dot(h_lo, r_hi) + _dot(h_hi, r_lo)
        lane = lax.broadcasted_iota(jnp.int32, logits.shape, 1)
        logits = jnp.where(lane < N_EXPERTS, logits, -jnp.inf)
        m1 = jnp.max(logits, axis=-1, keepdims=True)
        i1 = jnp.min(jnp.where(logits == m1, lane, LANES), axis=-1, keepdims=True)
        rest_l = jnp.where(lane == i1, -jnp.inf, logits)
        m2 = jnp.max(rest_l, axis=-1, keepdims=True)
        i2 = jnp.min(jnp.where(rest_l == m2, lane, LANES), axis=-1, keepdims=True)
        e2 = jnp.exp(m2 - m1)
        den = 1.0 + e2
        gate_ref[...] = jnp.where(lane == i1, 1.0 / den, 0.0) + jnp.where(lane == i2, e2 / den, 0.0)


def outproj_norm(x, mix_p, mix_s, mo_p, mo_s, w, g, router=None):
    n, d = x.shape
    tm = _row_tile(n, 512)
    assert mix_p.shape[0] % tm == 0 and mix_s.shape[0] % tm == 0
    pt = mix_p.shape[0] // tm
    moe = router is not None
    row = lambda c: pl.BlockSpec((tm, c), lambda i: (i, 0))
    row_p = lambda c: pl.BlockSpec((tm, c), lambda i: (jnp.minimum(i, pt - 1), 0))
    row_s = lambda c: pl.BlockSpec((tm, c), lambda i: (jnp.maximum(i - pt, 0), 0))
    full = lambda a: pl.BlockSpec(a.shape, lambda i: (0, 0))
    ins = [x, mix_p, mix_s, mo_p, mo_s, w, g.reshape(1, d)]
    in_specs = [row(d), row_p(mix_p.shape[1]), row_s(mix_s.shape[1]), row_p(mo_p.shape[1]), row_s(mo_s.shape[1]),
                full(w), pl.BlockSpec((1, d), lambda i: (0, 0))]
    out_specs = [row(d), row(d)]
    out_shape = [jax.ShapeDtypeStruct((n, d), F32), jax.ShapeDtypeStruct((n, d), BF16)]
    if moe:
        rp = jnp.pad(router, ((0, 0), (0, LANES - router.shape[1])))
        ins.append(rp)
        in_specs.append(full(rp))
        out_specs.append(row(LANES))
        out_shape.append(jax.ShapeDtypeStruct((n, LANES), F32))
    return pl.pallas_call(
        functools.partial(_outproj_kernel, moe=moe, prompt_tiles=pt),
        grid=(n // tm,),
        in_specs=in_specs, out_specs=out_specs, out_shape=out_shape,
        compiler_params=_params(),
        name="outproj_norm",
    )(*ins)


def _ffn_kernel(h_ref, x_ref, wg_ref, wu_ref, wd_ref, o_ref):
    @pl.when(pl.program_id(1) == 0)
    def _():
        o_ref[...] = x_ref[...]

    h = h_ref[...]
    a = _dot(h, wg_ref[...])
    u = _dot(h, wu_ref[...])
    act = (a * jax.nn.sigmoid(a) * u).astype(BF16)
    o_ref[...] += _dot(act, wd_ref[...])


def ffn_dense(h, x, wg, wu, wd, nf=2):
    n, d = x.shape
    tm = _row_tile(n, 512)
    tf = wg.shape[1] // nf
    return pl.pallas_call(
        _ffn_kernel,
        grid=(n // tm, nf),
        in_specs=[pl.BlockSpec((tm, d), lambda i, j: (i, 0)),
                  pl.BlockSpec((tm, d), lambda i, j: (i, 0)),
                  pl.BlockSpec((d, tf), lambda i, j: (0, j)),
                  pl.BlockSpec((d, tf), lambda i, j: (0, j)),
                  pl.BlockSpec((tf, d), lambda i, j: (j, 0))],
        out_specs=pl.BlockSpec((tm, d), lambda i, j: (i, 0)),
        out_shape=jax.ShapeDtypeStruct((n, d), F32),
        compiler_params=_params(),
        name="ffn_dense",
    )(h, x, wg, wu, wd)


MOE_RB = 128
MOE_RB_SHIFT = 7


def _moe_kernel(cnt_ref, h_ref, x_ref, gate_ref, gt_ref, wg_ref, wu_ref, wd_ref, o_ref,
                xs_ref, y_ref, srow_ref, scol_ref):
    i, e, j = pl.program_id(0), pl.program_id(1), pl.program_id(2)
    tm = h_ref.shape[0]
    nblk = (cnt_ref[i, e] + (MOE_RB - 1)) >> MOE_RB_SHIFT

    @pl.when((e == 0) & (j == 0))
    def _():
        o_ref[...] = x_ref[...]
        r = lax.broadcasted_iota(jnp.int32, (tm, tm), 0)
        c = lax.broadcasted_iota(jnp.int32, (tm, tm), 1)
        srow_ref[...] = _dot((gt_ref[...] != 0.0).astype(BF16), (r < c).astype(BF16))
        scol_ref[...] = _dot((c < r).astype(BF16), (gate_ref[...] != 0.0).astype(BF16))

    def block_rows(k):
        return pl.ds(pl.multiple_of(k * MOE_RB, MOE_RB), MOE_RB)

    def pick_row(a):
        return jnp.sum(jnp.where(lax.broadcasted_iota(jnp.int32, a.shape, 0) == e, a, 0.0), axis=0, keepdims=True)

    def pick_col(a):
        return jnp.sum(jnp.where(lax.broadcasted_iota(jnp.int32, a.shape, 1) == e, a, 0.0), axis=1, keepdims=True)

    @pl.when(j == 0)
    def _():
        mrow = pick_row(gt_ref[...]) != 0.0
        slot = pick_row(srow_ref[...])

        def pack(k, _):
            rid = (k * MOE_RB + lax.broadcasted_iota(jnp.int32, (MOE_RB, 1), 0)).astype(F32)
            onehot = (mrow & (slot == rid)).astype(BF16)
            xs_ref[block_rows(k), :] = _dot(onehot, h_ref[...]).astype(BF16)
            y_ref[block_rows(k), :] = jnp.zeros((MOE_RB, y_ref.shape[1]), F32)
            return 0

        lax.fori_loop(0, nblk, pack, 0)

    def expert(k, _):
        xs = xs_ref[block_rows(k), :]
        a = _dot(xs, wg_ref[...])
        u = _dot(xs, wu_ref[...])
        act = (a * jax.nn.sigmoid(a) * u).astype(BF16)
        y_ref[block_rows(k), :] += _dot(act, wd_ref[...])
        return 0

    lax.fori_loop(0, nblk, expert, 0)

    @pl.when(j == pl.num_programs(2) - 1)
    def _():
        ge = pick_col(gate_ref[...])
        mcol = ge != 0.0
        slot = pick_col(scol_ref[...])

        def unpack(k, _):
            cid = (k * MOE_RB + lax.broadcasted_iota(jnp.int32, (1, MOE_RB), 1)).astype(F32)
            onehot_t = (mcol & (slot == cid)).astype(BF16)
            y_hi, y_lo = _split_hi_lo(y_ref[block_rows(k), :])
            o_ref[...] += ge * _dot(jnp.concatenate([onehot_t, onehot_t], axis=1),
                                    jnp.concatenate([y_hi, y_lo], axis=0))
            return 0

        lax.fori_loop(0, nblk, unpack, 0)


def moe_ffn(h, x, gate, wg, wu, wd):
    n, d = x.shape
    tm = _row_tile(n, 1024)
    nt = n // tm
    n_exp = wg.shape[0]
    tf = next(t for t in (896, 512, 256, 128) if wg.shape[2] % t == 0)
    nf = wg.shape[2] // tf
    routed = gate[:, :n_exp] != 0.0
    counts = jnp.sum(routed.reshape(nt, tm, n_exp), axis=1, dtype=jnp.int32)
    gate_t = jnp.transpose(gate[:, :n_exp])
    return pl.pallas_call(
        _moe_kernel,
        grid_spec=pltpu.PrefetchScalarGridSpec(
            num_scalar_prefetch=1,
            grid=(nt, n_exp, nf),
            in_specs=[pl.BlockSpec((tm, d), lambda i, e, j, c: (i, 0)),
                      pl.BlockSpec((tm, d), lambda i, e, j, c: (i, 0)),
                      pl.BlockSpec((tm, LANES), lambda i, e, j, c: (i, 0)),
                      pl.BlockSpec((n_exp, tm), lambda i, e, j, c: (0, i)),
                      pl.BlockSpec((None, d, tf), lambda i, e, j, c: (e, 0, j)),
                      pl.BlockSpec((None, d, tf), lambda i, e, j, c: (e, 0, j)),
                      pl.BlockSpec((None, tf, d), lambda i, e, j, c: (e, j, 0))],
            out_specs=pl.BlockSpec((tm, d), lambda i, e, j, c: (i, 0)),
            scratch_shapes=[pltpu.VMEM((tm, d), BF16), pltpu.VMEM((tm, d), F32),
                            pltpu.VMEM((n_exp, tm), F32), pltpu.VMEM((tm, LANES), F32)],
        ),
        out_shape=jax.ShapeDtypeStruct((n, d), F32),
        compiler_params=_params(),
        name="moe_ffn",
    )(counts, h, x, gate, gate_t, wg, wu, wd)


def _final_norm_kernel(x_ref, g_ref, o_ref):
    o_ref[...] = _rms(x_ref[...], g_ref[...])


def final_norm(x, g, row0, n):
    d = x.shape[1]
    tm = _row_tile(n, 1024)
    assert row0 % tm == 0
    blk0 = row0 // tm
    return pl.pallas_call(
        _final_norm_kernel,
        grid=(n // tm,),
        in_specs=[pl.BlockSpec((tm, d), lambda i: (blk0 + i, 0)), pl.BlockSpec((1, d), lambda i: (0, 0))],
        out_specs=pl.BlockSpec((tm, d), lambda i: (i, 0)),
        out_shape=jax.ShapeDtypeStruct((n, d), F32),
        compiler_params=_params(),
        name="final_norm",
    )(x, g.reshape(1, d))


SLC_TILE = 512
WIN_TILE = 128


def _kv_prep_kernel(s_ref, w_ref, tab_ref, ska_ref, svt_ref, wka_ref, wvt_ref):
    tab = tab_ref[...]
    onehot, pieces = tab[:, :LANES], tab[:, LANES:]
    s = s_ref[...]
    w = w_ref[...]
    for g in range(N_KV_GROUPS):
        kc = slice(g * HEAD_DIM, (g + 1) * HEAD_DIM)
        vc = slice((N_KV_GROUPS + g) * HEAD_DIM, (N_KV_GROUPS + g + 1) * HEAD_DIM)
        ska_ref[g] = jnp.concatenate([onehot, s[:, kc].astype(BF16), pieces], axis=1)
        svt_ref[g] = s[:, vc].T.astype(BF16)
        wka_ref[g] = jnp.concatenate([w[:, kc].astype(BF16), pieces], axis=1)
        wv_t = w[:, vc].T.astype(BF16)
        for j in range(SLC_TILE // WIN_TILE):
            wvt_ref[g, j] = wv_t[:, j * WIN_TILE:(j + 1) * WIN_TILE]


def kv_prep(slc_rows, win_rows, tab, nb, t_len):
    tm = SLC_TILE
    nt = t_len // tm
    wpt = tm // WIN_TILE
    g3 = N_KV_GROUPS
    return pl.pallas_call(
        _kv_prep_kernel,
        grid=(nb, nt),
        in_specs=[pl.BlockSpec((tm, KV_ROW), lambda b, t: (b * nt + t, 0)),
                  pl.BlockSpec((tm, KV_ROW), lambda b, t: (b * nt + t, 0)),
                  pl.BlockSpec((tm, tab.shape[1]), lambda b, t: (t, 0))],
        out_specs=[pl.BlockSpec((None, g3, tm, 2 * LANES), lambda b, t: (b, 0, t, 0)),
                   pl.BlockSpec((None, g3, None, HEAD_DIM, tm), lambda b, t: (b, 0, t, 0, 0)),
                   pl.BlockSpec((None, g3, tm, LANES), lambda b, t: (b, 0, t, 0)),
                   pl.BlockSpec((None, g3, wpt, HEAD_DIM, WIN_TILE), lambda b, t: (b, 0, t, 0, 0))],
        out_shape=[jax.ShapeDtypeStruct((nb, g3, t_len, 2 * LANES), BF16),
                   jax.ShapeDtypeStruct((nb, g3, nt, HEAD_DIM, tm), BF16),
                   jax.ShapeDtypeStruct((nb, g3, t_len, LANES), BF16),
                   jax.ShapeDtypeStruct((nb, g3, t_len // WIN_TILE, HEAD_DIM, WIN_TILE), BF16)],
        compiler_params=_params(),
        name="kv_prep",
    )(slc_rows, win_rows, tab)


def _cmp_first_layer(get_cols, kv, pe_ref, w1_ref):
    acc_a = None
    acc_b = None
    for p in range(CMP_STRIDE):
        xs = jnp.concatenate(
            [get_cols(p * KV_ROW + (kv * N_KV_GROUPS + g) * HEAD_DIM) for g in range(N_KV_GROUPS)], axis=0)
        q = p + CMP_STRIDE
        da = _dot((xs + pe_ref[kv, p:p + 1, :]).astype(BF16), w1_ref[kv, p * HEAD_DIM:(p + 1) * HEAD_DIM, :])
        db = _dot((xs + pe_ref[kv, q:q + 1, :]).astype(BF16), w1_ref[kv, q * HEAD_DIM:(q + 1) * HEAD_DIM, :])
        acc_a = da if acc_a is None else acc_a + da
        acc_b = db if acc_b is None else acc_b + db
    return acc_a, acc_b


def _cmp_second_layer(kv, nch, a_ref, b_ref, w2_ref, emit):
    for g in range(N_KV_GROUPS):
        hid = a_ref[g, 0:nch, :] + b_ref[g, pl.ds(1, nch), :]
        hid = hid * jax.nn.sigmoid(hid)
        emit(kv, g, _dot(hid.astype(BF16), w2_ref[kv]))


def _compress_prompt_kernel(x_ref, pe_ref, w1_ref, w2_ref, cke_ref, cka_ref, cvt_ref, a_ref, b_ref, *, nch, tc):
    def emit(kv, g, out):
        if kv == 0:
            cka_ref[g] = jnp.concatenate([out.astype(BF16), cke_ref[...]], axis=1)
        else:
            cvt_ref[g] = out.T.astype(BF16)

    for kv in range(2):
        for ct in range(nch // tc):
            rows = slice(ct * tc, (ct + 1) * tc)
            acc_a, acc_b = _cmp_first_layer(lambda c0: x_ref[rows, c0:c0 + HEAD_DIM], kv, pe_ref, w1_ref)
            for g in range(N_KV_GROUPS):
                a_ref[g, rows, :] = acc_a[g * tc:(g + 1) * tc]
                b_ref[g, rows, :] = acc_b[g * tc:(g + 1) * tc]
        b_ref[:, nch:nch + 8, :] = jnp.zeros((N_KV_GROUPS, 8, b_ref.shape[2]), F32)
        _cmp_second_layer(kv, nch, a_ref, b_ref, w2_ref, emit)


def compress_prompt(cmp_rows, pe, w1, w2, cke, nb, t_len):
    nch = t_len // CMP_STRIDE
    tc = min(nch, 128)
    hid = w1.shape[2]
    xc = cmp_rows.reshape(nb, nch, CHUNK_ROW)
    full = lambda a: pl.BlockSpec(a.shape, lambda b: (0,) * a.ndim)
    return pl.pallas_call(
        functools.partial(_compress_prompt_kernel, nch=nch, tc=tc),
        grid=(nb,),
        in_specs=[pl.BlockSpec((None, nch, CHUNK_ROW), lambda b: (b, 0, 0)), full(pe), full(w1), full(w2), full(cke)],
        out_specs=[pl.BlockSpec((None, N_KV_GROUPS, nch, LANES), lambda b: (b, 0, 0, 0)),
                   pl.BlockSpec((None, N_KV_GROUPS, HEAD_DIM, nch), lambda b: (b, 0, 0, 0))],
        out_shape=[jax.ShapeDtypeStruct((nb, N_KV_GROUPS, nch, LANES), BF16),
                   jax.ShapeDtypeStruct((nb, N_KV_GROUPS, HEAD_DIM, nch), BF16)],
        scratch_shapes=[pltpu.VMEM((N_KV_GROUPS, nch + 8, hid), F32)] * 2,
        compiler_params=_params(),
        name="compress_prompt",
    )(xc, pe, w1, w2, cke)


def _compress_sample_kernel(pt_ref, *refs, n_pages):
    page_refs = refs[:n_pages]
    new_ref, pe_ref, w1_ref, w2_ref, ck_ref, cv_ref, a_ref, b_ref = refs[n_pages:]
    nch = n_pages * (PAGE_SIZE // CMP_STRIDE)
    for kv in range(2):
        acc_a, acc_b = _cmp_first_layer(
            lambda c0: jnp.concatenate([pr[:, c0:c0 + HEAD_DIM] for pr in page_refs], axis=0), kv, pe_ref, w1_ref)
        _, new_b = _cmp_first_layer(
            lambda c0: jnp.broadcast_to(new_ref[:, c0:c0 + HEAD_DIM], (8, HEAD_DIM)), kv, pe_ref, w1_ref)
        for g in range(N_KV_GROUPS):
            a_ref[g, 0:nch, :] = acc_a[g * nch:(g + 1) * nch]
            b_ref[g, 0:nch, :] = acc_b[g * nch:(g + 1) * nch]
            b_ref[g, nch:nch + 8, :] = new_b[g * 8:(g + 1) * 8]

        def emit(kv, g, out):
            (ck_ref, cv_ref)[kv][g] = out.astype(BF16)

        _cmp_second_layer(kv, nch, a_ref, b_ref, w2_ref, emit)


def compress_sample(pool, page_table, cmp_new, pe, w1, w2):
    nseq, n_pages = page_table.shape
    dec = cmp_new.shape[1]
    cpp = PAGE_SIZE // CMP_STRIDE
    nch = n_pages * cpp
    hid = w1.shape[2]
    pool_c = pool.reshape(pool.shape[0], cpp, CHUNK_ROW)
    new_c = jnp.pad(cmp_new.reshape(nseq, 1, dec * KV_ROW), ((0, 0), (0, 0), (0, CHUNK_ROW - dec * KV_ROW)))
    page_spec = lambda p: pl.BlockSpec((None, cpp, CHUNK_ROW), lambda b, pt: (pt[b, p], 0, 0))
    full = lambda a: pl.BlockSpec(a.shape, lambda b, pt: (0,) * a.ndim)
    o_spec = pl.BlockSpec((None, N_KV_GROUPS, nch, HEAD_DIM), lambda b, pt: (b, 0, 0, 0))
    o_shape = jax.ShapeDtypeStruct((nseq, N_KV_GROUPS, nch, HEAD_DIM), BF16)
    return pl.pallas_call(
        functools.partial(_compress_sample_kernel, n_pages=n_pages),
        grid_spec=pltpu.PrefetchScalarGridSpec(
            num_scalar_prefetch=1,
            grid=(nseq,),
            in_specs=[page_spec(p) for p in range(n_pages)]
            + [pl.BlockSpec((None, 1, CHUNK_ROW), lambda b, pt: (b, 0, 0)), full(pe), full(w1), full(w2)],
            out_specs=[o_spec, o_spec],
            scratch_shapes=[pltpu.VMEM((N_KV_GROUPS, nch + 8, hid), F32)] * 2,
        ),
        out_shape=[o_shape, o_shape],
        compiler_params=_params(),
        name="compress_sample",
    )(page_table, *([pool_c] * n_pages), new_c, pe, w1, w2)


def _softmax_parts(s, axis=-1):
    m = jnp.maximum(jnp.max(s, axis=axis, keepdims=True), M_INIT)
    e = jnp.exp(s - m)
    return e, jnp.sum(e, axis=axis, keepdims=True)


def _masked_softmax(s, axis=-1):
    e, d = _softmax_parts(s, axis)
    return e * (1.0 / jnp.where(d > 0, d, 1.0))


def _select_blocks(imp, cur, axis=-1):
    axis = axis % imp.ndim
    j = lax.broadcasted_iota(jnp.int32, imp.shape, axis)
    valid = j <= cur
    forced = valid & ((j == 0) | (j == cur) | (j == cur - 1))
    v = jnp.where(forced, jnp.inf, jnp.where(valid, imp, -jnp.inf))
    sel = jnp.zeros(imp.shape, F32)
    for _ in range(N_SEL):
        m = jnp.max(v, axis=axis, keepdims=True)
        jm = jnp.min(jnp.where(v == m, j, LANES), axis=axis, keepdims=True)
        pick = j == jm
        v = jnp.where(pick, -jnp.inf, v)
        sel = jnp.where(pick, 1.0, sel)
    return (sel > 0.0) & valid


def _stack_heads(q):
    return jnp.concatenate([q[:, r * HEAD_DIM:(r + 1) * HEAD_DIM] for r in range(HEADS_PER_GROUP)], axis=0)


def _importance(p_c, tq, mimp_ref):
    pcs = p_c[0:tq]
    for r in range(1, HEADS_PER_GROUP):
        pcs = pcs + p_c[r * tq:(r + 1) * tq]
    hi, lo = _split_hi_lo(pcs)
    return _dot(hi, mimp_ref[...]) + _dot(lo, mimp_ref[...])


def _gated_merge(gsig, g, o_c, o_s, o_w, tq, store):
    for r in range(HEADS_PER_GROUP):
        c = 3 * (g * HEADS_PER_GROUP + r)
        rows = slice(r * tq, (r + 1) * tq)
        store(r, gsig[:, c:c + 1] * o_c[rows] + gsig[:, c + 1:c + 2] * o_s[rows] + gsig[:, c + 2:c + 3] * o_w[rows])


def _nsa_prompt_kernel(q_ref, glt_ref, qe_ref, cka_ref, cvt_ref, ska_ref, svt_ref, wka_ref, wvt_ref,
                       mimpt_ref, o_ref, *, t_len):
    i = pl.program_id(2)
    tq = Q_BLOCK
    rows = HEADS_PER_GROUP * tq
    ncp = t_len // CMP_STRIDE
    s0 = i * tq
    q4 = (_stack_heads(q_ref[...]) * (HEAD_DIM ** -0.5)).astype(BF16)
    qb = jnp.concatenate([q4, qe_ref[...]], axis=1)
    qpos = s0 + (lax.broadcasted_iota(jnp.int32, (1, rows), 1) & (tq - 1))

    cpos = lax.broadcasted_iota(jnp.int32, (ncp, 1), 0) * CMP_STRIDE + (CMP_BLOCK - 1)
    mask_c = cpos <= qpos
    p_c = _masked_softmax(jnp.where(mask_c, _dot_nt(cka_ref[...], qb), NEG), axis=0)
    o_c = _dot(cvt_ref[...], p_c.astype(BF16))

    nwt = (WINDOW + tq) // WIN_TILE
    wb = jnp.maximum(i - WINDOW // WIN_TILE, 0)
    kw = pl.ds(pl.multiple_of(wb * WIN_TILE, WIN_TILE), nwt * WIN_TILE)
    dist = qpos - (wb * WIN_TILE + lax.broadcasted_iota(jnp.int32, (nwt * WIN_TILE, 1), 0))
    mask_w = (dist >= 0) & (dist < WINDOW)
    e_w, d_w = _softmax_parts(jnp.where(mask_w, _dot_nt(wka_ref[kw, :], qb), NEG), axis=0)
    e_w = e_w.astype(BF16)
    o_w = _dot(wvt_ref[wb], e_w[0:WIN_TILE])
    for w in range(1, nwt):
        o_w = o_w + _dot(wvt_ref[wb + w], e_w[w * WIN_TILE:(w + 1) * WIN_TILE])
    o_w = o_w / d_w

    pcs = p_c[:, 0:tq]
    for r in range(1, HEADS_PER_GROUP):
        pcs = pcs + p_c[:, r * tq:(r + 1) * tq]
    hi, lo = _split_hi_lo(pcs)
    imp = _dot(mimpt_ref[...], hi) + _dot(mimpt_ref[...], lo)
    sel = _select_blocks(imp, qpos[:, 0:tq] >> SLC_SHIFT, axis=0)
    selbias = jnp.where(sel, 0.0, NEG).T.astype(BF16)
    qa = jnp.concatenate([jnp.concatenate([selbias] * HEADS_PER_GROUP, axis=0), qb], axis=1)

    def scores(kt):
        return _dot_nt(ska_ref[pl.ds(pl.multiple_of(kt * SLC_TILE, SLC_TILE), SLC_TILE), :], qa)

    def flash_update(stats, s, vt):
        m, l, acc = stats
        m_new = jnp.maximum(m, jnp.max(s, axis=0, keepdims=True))
        a = jnp.exp(m - m_new)
        p = jnp.exp(s - m_new)
        return m_new, a * l + jnp.sum(p, axis=0, keepdims=True), a * acc + _dot(vt, p.astype(BF16))

    def bulk(kt, carry):
        s_cur, stats = carry
        s_next = scores(kt + 1)
        return s_next, flash_update(stats, s_cur, svt_ref[kt])

    stats = (jnp.full((1, rows), M_INIT, F32), jnp.zeros((1, rows), F32), jnp.zeros((HEAD_DIM, rows), F32))
    kt_last = s0 // SLC_TILE
    s_last, stats = lax.fori_loop(0, kt_last, bulk, (scores(0), stats))
    kpos = kt_last * SLC_TILE + lax.broadcasted_iota(jnp.int32, (SLC_TILE, 1), 0)
    _, l_s, acc_s = flash_update(stats, jnp.where(kpos <= qpos, s_last, NEG), svt_ref[kt_last])
    o_s = acc_s / l_s

    gsig = jax.nn.sigmoid(glt_ref[...])
    lane_gate = lambda c: jnp.concatenate([gsig[3 * r + c:3 * r + c + 1, :] for r in range(HEADS_PER_GROUP)], axis=1)
    out_t = lane_gate(0) * o_c + lane_gate(1) * o_s + lane_gate(2) * o_w
    for r in range(HEADS_PER_GROUP):
        o_ref[:, r * HEAD_DIM:(r + 1) * HEAD_DIM] = out_t[:, r * tq:(r + 1) * tq].T.astype(o_ref.dtype)


def nsa_prompt(q, gl_t, cka, cvt, ska, svt, wka, wvt, consts, nb, t_len):
    nqb = t_len // Q_BLOCK
    rows = HEADS_PER_GROUP * Q_BLOCK
    gw = 3 * HEADS_PER_GROUP
    qe, mimp_t = consts
    per_bg = lambda a: pl.BlockSpec((None, None) + a.shape[2:], lambda b, g, i: (b, g) + (0,) * (a.ndim - 2))
    return pl.pallas_call(
        functools.partial(_nsa_prompt_kernel, t_len=t_len),
        grid=(nb, N_KV_GROUPS, nqb),
        in_specs=[pl.BlockSpec((Q_BLOCK, HEADS_PER_GROUP * HEAD_DIM), lambda b, g, i: (b * nqb + i, g)),
                  pl.BlockSpec((None, gw, Q_BLOCK), lambda b, g, i: (g, 0, b * nqb + i)),
                  pl.BlockSpec((None, rows, HEAD_DIM), lambda b, g, i: (g, 0, 0)),
                  per_bg(cka), per_bg(cvt), per_bg(ska), per_bg(svt), per_bg(wka), per_bg(wvt),
                  pl.BlockSpec(mimp_t.shape, lambda b, g, i: (0, 0))],
        out_specs=pl.BlockSpec((Q_BLOCK, HEADS_PER_GROUP * HEAD_DIM), lambda b, g, i: (b * nqb + i, g)),
        out_shape=jax.ShapeDtypeStruct((nb * t_len, NSA_W), BF16),
        compiler_params=_params(),
        name="nsa_prompt",
    )(q, gl_t, qe, cka, cvt, ska, svt, wka, wvt, mimp_t)


def _nsa_sample_kernel(pt_ref, *refs, n_pages, dec, win_buf):
    page_refs = refs[:n_pages]
    (q_ref, gl_ref, slp_ref, ck_ref, cv_ref, snew_ref, wc_ref, wnew_ref, es_ref, mimp_ref, o_ref) = refs[n_pages:]
    tq = dec
    rows = HEADS_PER_GROUP * tq
    past = n_pages * PAGE_SIZE
    ncs = ck_ref.shape[1]
    t_row = lax.broadcasted_iota(jnp.int32, (rows, 1), 0) & (tq - 1)
    qpos = past + t_row
    gsig = jax.nn.sigmoid(gl_ref[...])
    q_all = q_ref[...]
    snew = snew_ref[...]
    wc = wc_ref[...]
    wnew = wnew_ref[...]
    tnew = lax.broadcasted_iota(jnp.int32, (1, tq), 1)
    mask_new = tnew <= t_row
    dist_new = (t_row - tnew).astype(F32)

    q4s, o_cs, imps = [], [], []
    for g in range(N_KV_GROUPS):
        q4 = (_stack_heads(q_all[:, g * HEADS_PER_GROUP * HEAD_DIM:(g + 1) * HEADS_PER_GROUP * HEAD_DIM])
              * (HEAD_DIM ** -0.5)).astype(BF16)
        cpos = lax.broadcasted_iota(jnp.int32, (1, ncs), 1) * CMP_STRIDE + (CMP_BLOCK - 1)
        mask_c = cpos <= qpos
        s_c = _dot_nt(q4, ck_ref[g]) - slp_ref[g] * (qpos - cpos).astype(F32)
        p_c = _masked_softmax(jnp.where(mask_c, s_c, NEG))
        q4s.append(q4)
        o_cs.append(_dot(p_c.astype(BF16), cv_ref[g]))
        imps.append(_importance(p_c, tq, mimp_ref))
    cur = (past + (lax.broadcasted_iota(jnp.int32, (1, N_KV_GROUPS * tq), 1) & (tq - 1))) >> SLC_SHIFT
    sel_t = _select_blocks(jnp.concatenate(imps, axis=0).T, cur, axis=0)
    selbias_all = jnp.where(sel_t, 0.0, NEG).T

    for g in range(N_KV_GROUPS):
        kc = slice(g * HEAD_DIM, (g + 1) * HEAD_DIM)
        vc = slice((N_KV_GROUPS + g) * HEAD_DIM, (N_KV_GROUPS + g + 1) * HEAD_DIM)
        q4, o_c = q4s[g], o_cs[g]
        slope = slp_ref[g]
        sb4 = jnp.concatenate([selbias_all[g * tq:(g + 1) * tq].astype(BF16)] * HEADS_PER_GROUP, axis=0)

        k_past = jnp.concatenate([pr[:, kc] for pr in page_refs], axis=0).astype(BF16)
        v_past = jnp.concatenate([pr[:, vc] for pr in page_refs], axis=0).astype(BF16)
        kpos = lax.broadcasted_iota(jnp.int32, (1, past), 1)
        s_p = _dot_nt(q4, k_past) + _dot_nt(sb4, es_ref[...]) - slope * (qpos - kpos).astype(F32)
        s_n = _dot_nt(q4, snew[:, kc].astype(BF16)) - slope * dist_new
        s_n = jnp.where(mask_new, s_n, NEG)
        m = jnp.maximum(jnp.max(s_p, axis=-1, keepdims=True), jnp.max(s_n, axis=-1, keepdims=True))
        e_p = jnp.exp(s_p - m)
        e_n = jnp.exp(s_n - m)
        l = jnp.sum(e_p, axis=-1, keepdims=True) + jnp.sum(e_n, axis=-1, keepdims=True)
        o_s = (_dot(e_p.astype(BF16), v_past) + _dot(e_n.astype(BF16), snew[:, vc].astype(BF16))) / l

        wdist = qpos - (past - win_buf + lax.broadcasted_iota(jnp.int32, (1, win_buf), 1))
        mask_w = wdist < WINDOW
        s_w = _dot_nt(q4, wc[:, kc].astype(BF16)) - slope * wdist.astype(F32)
        s_w = jnp.where(mask_w, s_w, NEG)
        s_wn = _dot_nt(q4, wnew[:, kc].astype(BF16)) - slope * dist_new
        s_wn = jnp.where(mask_new, s_wn, NEG)
        m = jnp.maximum(jnp.max(s_w, axis=-1, keepdims=True), jnp.max(s_wn, axis=-1, keepdims=True))
        e_w = jnp.exp(s_w - m)
        e_wn = jnp.exp(s_wn - m)
        l = jnp.sum(e_w, axis=-1, keepdims=True) + jnp.sum(e_wn, axis=-1, keepdims=True)
        o_w = (_dot(e_w.astype(BF16), wc[:, vc].astype(BF16)) + _dot(e_wn.astype(BF16), wnew[:, vc].astype(BF16))) / l

        def store(r, val, g=g):
            c = (g * HEADS_PER_GROUP + r) * HEAD_DIM
            o_ref[:, c:c + HEAD_DIM] = val.astype(o_ref.dtype)

        _gated_merge(gsig, g, o_c, o_s, o_w, tq, store)


def nsa_sample(q, gl, ck, cv, pool, page_table, slc_rows, win_cache, win_rows, consts, row_off, dec):
    nseq, n_pages = page_table.shape
    win_buf = win_cache.shape[1]
    slp, es, mimp = consts
    full = lambda a: pl.BlockSpec(a.shape, lambda b, pt: (0,) * a.ndim)
    row = lambda c: pl.BlockSpec((dec, c), lambda b, pt: (row_off + b, 0))
    seq4 = lambda a: pl.BlockSpec((None,) + a.shape[1:], lambda b, pt: (b, 0, 0, 0))
    page_spec = lambda p: pl.BlockSpec((None, PAGE_SIZE, KV_ROW), lambda b, pt: (pt[b, p], 0, 0))
    return pl.pallas_call(
        functools.partial(_nsa_sample_kernel, n_pages=n_pages, dec=dec, win_buf=win_buf),
        grid_spec=pltpu.PrefetchScalarGridSpec(
            num_scalar_prefetch=1,
            grid=(nseq,),
            in_specs=[page_spec(p) for p in range(n_pages)]
            + [row(NSA_W), row(GATE_W), full(slp), seq4(ck), seq4(cv), row(KV_ROW),
               pl.BlockSpec((None, win_buf, KV_ROW), lambda b, pt: (b, 0, 0)), row(KV_ROW), full(es), full(mimp)],
            out_specs=pl.BlockSpec((dec, NSA_W), lambda b, pt: (b, 0)),
        ),
        out_shape=jax.ShapeDtypeStruct((nseq * dec, NSA_W), F32),
        compiler_params=_params(),
        name="nsa_sample",
    )(page_table, *([pool] * n_pages), q, gl, slp, ck, cv, slc_rows, win_cache, win_rows, es, mimp)


def _alibi_slopes():
    return 2.0 ** (-ALIBI_MAX_BIAS * jnp.arange(1, N_HEADS + 1, dtype=F32) / N_HEADS)


def _slope_rows(tq):
    return jnp.repeat(_alibi_slopes().reshape(N_KV_GROUPS, HEADS_PER_GROUP), tq, axis=1)[..., None]


def _importance_matrix(n_cmp):
    n = jnp.arange(n_cmp)[:, None]
    j = jnp.arange(LANES)[None, :]
    per_sel = SLC_BLOCK // CMP_STRIDE
    inside = (SLC_BLOCK - CMP_BLOCK) // CMP_STRIDE + 1
    return ((n // per_sel == j) & (n % per_sel < inside)).astype(BF16)


def _block_onehot(n_keys):
    k = jnp.arange(n_keys)[:, None]
    return (k // SLC_BLOCK == jnp.arange(LANES)[None, :]).astype(BF16)


def _position_pieces(pos):
    pos = pos[:, None]
    hi = ((pos // SLC_BLOCK) * SLC_BLOCK).astype(BF16)
    lo = (pos % SLC_BLOCK).astype(BF16)
    pieces = jnp.concatenate([hi, hi, hi, lo, lo, lo], axis=-1)
    return jnp.pad(pieces, ((0, 0), (0, HEAD_DIM - pieces.shape[-1])))


def _prompt_tables(t_len):
    slp = _slope_rows(Q_BLOCK)
    s1 = slp.astype(BF16)
    s2 = (slp - s1.astype(F32)).astype(BF16)
    s3 = (slp - s1.astype(F32) - s2.astype(F32)).astype(BF16)
    qe = jnp.concatenate([s1, s2, s3, s1, s2, s3], axis=-1)
    qe = jnp.pad(qe, ((0, 0), (0, 0), (0, HEAD_DIM - qe.shape[-1])))
    key_tab = jnp.concatenate([_block_onehot(t_len), _position_pieces(jnp.arange(t_len))], axis=-1)
    ncp = t_len // CMP_STRIDE
    cmp_tab = _position_pieces(jnp.arange(ncp) * CMP_STRIDE + (CMP_BLOCK - 1))
    return (qe, jnp.transpose(_importance_matrix(ncp))), key_tab, cmp_tab


def kernel(x_prompt, x_sample, state_conv, cache_mem_kv, cache_cmp_kv, cache_slc_kv, cache_win_kv, page_table,
           mem_prompt, norm1, norm2, norm_final, w_in_a, conv_w, conv_b, conv_ln_g, conv_ln_b, w_in_b, kv_norm,
           w_kv_shared, cmp_pos, cmp_w1, cmp_w2, w_mem_kv, w_out, ffn_w_gate, ffn_w_up, ffn_w_down, moe_router,
           moe_w_gate, moe_w_up, moe_w_down):
    bp, t_len, d = x_prompt.shape
    db, dec, _ = x_sample.shape
    depth = norm1.shape[0]
    n_a = w_in_a.shape[0]
    n_mem = mem_prompt.shape[1]
    n_p = bp * t_len
    n_s = db * dec
    n_pages = page_table.shape[1]
    past = n_pages * PAGE_SIZE
    win_buf = cache_win_kv.shape[1]
    bf = lambda a: a.astype(BF16)

    x = jnp.concatenate([x_prompt.reshape(n_p, d), x_sample.reshape(n_s, d)], axis=0)

    mem_kv_p = mem_kv_proj(mem_prompt.reshape(bp * n_mem, d), w_mem_kv)
    mem_kv_p = mem_kv_p.reshape(depth, bp, n_mem, 2 * MEM_W)
    mem_kv_s = cache_mem_kv.reshape(depth, db, n_mem, 2 * MEM_W)

    conv_p, conv_s = [], []
    zero_state = jnp.zeros((bp, CONV_STATE, C_CONV), F32)
    cmp_rows = slc_rows = win_rows = None
    nsa_ctx = None

    for l in range(depth):
        if l < n_a:
            uc, qm = norm_proj(x, norm1[l], bf(w_in_a[l]), (2 * C_CONV, MEM_W))
            mix_p, st_p = conv_mixer(uc, zero_state, conv_w[l], conv_b[l], conv_ln_g[l], conv_ln_b[l],
                                     bp, t_len, Q_BLOCK)
            mix_s, st_s = conv_mixer(uc, state_conv[l], conv_w[l], conv_b[l], conv_ln_g[l], conv_ln_b[l],
                                     db, dec, dec, row0=n_p)
            conv_p.append(st_p)
            conv_s.append(st_s)
        else:
            if nsa_ctx is None:
                cmp_rows, slc_rows, win_rows = norm_proj(x, kv_norm, bf(w_kv_shared), (KV_ROW,) * 3)
                w1 = bf(cmp_w1)
                w2 = bf(cmp_w2)
                prompt_consts, key_tab, cmp_tab = _prompt_tables(t_len)
                cka, cvt = compress_prompt(cmp_rows[:n_p], cmp_pos, w1, w2, cmp_tab, bp, t_len)
                ck_s, cv_s = compress_sample(cache_cmp_kv.reshape(-1, PAGE_SIZE, KV_ROW), page_table,
                                             cmp_rows[n_p:].reshape(db, dec, KV_ROW), cmp_pos, w1, w2)
                ska, svt, wka, wvt = kv_prep(slc_rows, win_rows, key_tab, bp, t_len)
                nsa_ctx = dict(
                    prompt=prompt_consts,
                    sample=(_slope_rows(dec), _block_onehot(past), _importance_matrix(ck_s.shape[2])),
                    pool=cache_slc_kv.reshape(-1, PAGE_SIZE, KV_ROW),
                    win_cache=cache_win_kv.reshape(db, win_buf, KV_ROW))
            w_in = w_in_b[l - n_a]
            w_in = jnp.concatenate([w_in[:, :NSA_W], w_in[:, NSA_W + GATE_W:], w_in[:, NSA_W:NSA_W + GATE_W]], axis=1)
            q, qm, gl = norm_proj(x, norm1[l], bf(w_in), (NSA_W, MEM_W, GATE_W))
            gw = 3 * HEADS_PER_GROUP
            gl_t = jnp.transpose(gl[:n_p].reshape(n_p, N_KV_GROUPS, gw), (1, 2, 0))
            mix_p = nsa_prompt(q, gl_t, cka, cvt, ska, svt, wka, wvt, nsa_ctx["prompt"], bp, t_len)
            mix_s = nsa_sample(q, gl, ck_s, cv_s, nsa_ctx["pool"], page_table, slc_rows, nsa_ctx["win_cache"],
                               win_rows, nsa_ctx["sample"], n_p // dec, dec)
        mo_p = mem_attend(qm, mem_kv_p[l], bp, t_len, 512)
        mo_s = mem_attend(qm, mem_kv_s[l], db, dec, dec, spb=MEM_SEQS_PER_STEP, row0=n_p)
        if l % 2 == 0:
            xn, h2 = outproj_norm(x, mix_p, mix_s, mo_p, mo_s, bf(w_out[l]), norm2[l])
            x = ffn_dense(h2, xn, bf(ffn_w_gate[l // 2]), bf(ffn_w_up[l // 2]), bf(ffn_w_down[l // 2]))
        else:
            xn, h2, gate = outproj_norm(x, mix_p, mix_s, mo_p, mo_s, bf(w_out[l]), norm2[l],
                                        router=moe_router[l // 2])
            x = moe_ffn(h2, xn, gate, bf(moe_w_gate[l // 2]), bf(moe_w_up[l // 2]), bf(moe_w_down[l // 2]))

    y_p = final_norm(x, norm_final, 0, n_p)
    y_s = final_norm(x, norm_final, n_p, n_s)
    kv5 = lambda a, b, t: a.reshape(b, t, 2, N_KV_GROUPS, HEAD_DIM)
    keep_p = min(WINDOW, t_len)
    win_p = kv5(win_rows[:n_p], bp, t_len)[:, t_len - keep_p:]
    win_all = jnp.concatenate([cache_win_kv, kv5(win_rows[n_p:], db, dec)], axis=1)
    keep_s = min(WINDOW, past + dec)
    return (y_p.reshape(bp, t_len, d), y_s.reshape(db, dec, d),
            jnp.stack(conv_p), jnp.stack(conv_s),
            mem_kv_p.reshape(depth, bp, n_mem, 2, MEM_HEADS, MEM_HEAD_DIM),
            kv5(cmp_rows[:n_p], bp, t_len), kv5(cmp_rows[n_p:], db, dec),
            kv5(slc_rows[:n_p], bp, t_len), kv5(slc_rows[n_p:], db, dec),
            win_p, win_all[:, win_all.shape[1] - keep_s:])
```

```python
import functools

import jax
import jax.numpy as jnp
from jax import lax
from jax.experimental import pallas as pl
from jax.experimental.pallas import tpu as pltpu

F32 = jnp.float32
BF16 = jnp.bfloat16

HEAD_DIM = 64
N_KV_GROUPS = 3
HEADS_PER_GROUP = 4
N_HEADS = N_KV_GROUPS * HEADS_PER_GROUP
NSA_W = N_HEADS * HEAD_DIM
GATE_W = 3 * N_HEADS
CMP_STRIDE = 16
CMP_BLOCK = 32
SLC_BLOCK = 64
SLC_SHIFT = 6
N_SEL = 16
WINDOW = 512
Q_BLOCK = 128
ALIBI_MAX_BIAS = 8.0
MEM_HEADS = 4
MEM_HEAD_DIM = 64
MEM_W = MEM_HEADS * MEM_HEAD_DIM
MEM_SEQS_PER_STEP = 8
C_CONV = 768
CONV_K = 31
CONV_STATE = CONV_K - 1
N_EXPERTS = 8
EPS = 1e-6
PAGE_SIZE = 128
KV_ROW = 2 * N_KV_GROUPS * HEAD_DIM
CHUNK_ROW = CMP_STRIDE * KV_ROW
LANES = 128
SUBLANES = 8
NEG = -1e30
M_INIT = -1e29
VMEM_LIMIT = 56 * 1024 * 1024


def _dot(a, b):
    return jnp.dot(a, b, preferred_element_type=F32)


def _dot_nt(a, b):
    return lax.dot_general(a, b, (((1,), (1,)), ((), ())), preferred_element_type=F32)


def _split_hi_lo(x):
    hi = x.astype(BF16)
    lo = (x - hi.astype(F32)).astype(BF16)
    return hi, lo


def _rms(x, g):
    return x * lax.rsqrt(jnp.mean(x * x, axis=-1, keepdims=True) + EPS) * g


def _params(**kw):
    return pltpu.CompilerParams(vmem_limit_bytes=VMEM_LIMIT, **kw)


def _log2(n):
    assert n > 0 and n & (n - 1) == 0, n
    return n.bit_length() - 1


def _row_tile(n, pref):
    tm = pref
    while n % tm:
        tm //= 2
    assert tm >= 8, (n, pref)
    return tm


def _norm_proj_kernel(x_ref, g_ref, w_ref, *o_refs, splits):
    h = _rms(x_ref[...], g_ref[...])
    y = _dot(h.astype(BF16), w_ref[...])
    off = 0
    for o_ref, n in zip(o_refs, splits):
        o_ref[...] = y[:, off:off + n]
        off += n


def norm_proj(x, g, w, splits):
    n, d = x.shape
    tm = _row_tile(n, 512)
    return pl.pallas_call(
        functools.partial(_norm_proj_kernel, splits=splits),
        grid=(n // tm,),
        in_specs=[pl.BlockSpec((tm, d), lambda i: (i, 0)),
                  pl.BlockSpec((1, d), lambda i: (0, 0)),
                  pl.BlockSpec(w.shape, lambda i: (0, 0))],
        out_specs=[pl.BlockSpec((tm, s), lambda i: (i, 0)) for s in splits],
        out_shape=[jax.ShapeDtypeStruct((n, s), F32) for s in splits],
        compiler_params=_params(),
        name="norm_proj",
    )(x, g.reshape(1, d), w)


def _mem_kv_kernel(x_ref, w_ref, o_ref):
    o_ref[...] = _dot(x_ref[...].astype(BF16), w_ref[...].astype(BF16))


def mem_kv_proj(x, w):
    depth, d, e = w.shape
    n = x.shape[0]
    return pl.pallas_call(
        _mem_kv_kernel,
        grid=(depth,),
        in_specs=[pl.BlockSpec((n, d), lambda l: (0, 0)),
                  pl.BlockSpec((None, d, e), lambda l: (l, 0, 0))],
        out_specs=pl.BlockSpec((None, n, e), lambda l: (l, 0, 0)),
        out_shape=jax.ShapeDtypeStruct((depth, n, e), F32),
        compiler_params=_params(),
        name="mem_kv_proj",
    )(x, w)


CONV_PAD = 32


def _conv_kernel(u_ref, st_ref, w_ref, b_ref, lg_ref, lb_ref, mix_ref, ns_ref, vp_ref, *, tt, rc):
    t = pl.program_id(1)
    lo = CONV_PAD - CONV_STATE

    @pl.when(t == 0)
    def _():
        vp_ref[lo:CONV_PAD, :] = st_ref[...]

    u = u_ref[...]
    vp_ref[CONV_PAD:CONV_PAD + tt, :] = u[:, :C_CONV] * jax.nn.sigmoid(u[:, C_CONV:])
    vp_ref[CONV_PAD + tt:CONV_PAD + tt + SUBLANES, :] = jnp.zeros((SUBLANES, C_CONV), F32)
    for c in range(tt // rc):
        acc = None
        for b in range(SUBLANES):
            part = None
            for k in range(CONV_K):
                a, phase = divmod(lo + k, SUBLANES)
                if phase == b:
                    term = w_ref[k:k + 1, :] * vp_ref[pl.ds(c * rc + SUBLANES * a, rc + SUBLANES), :]
                    part = term if part is None else part + term
            acc = part[b:b + rc] if acc is None else acc + part[b:b + rc]
        h = acc + b_ref[...]
        mu = jnp.mean(h, axis=-1, keepdims=True)
        hc = h - mu
        var = jnp.mean(hc * hc, axis=-1, keepdims=True)
        y = hc * lax.rsqrt(var + EPS) * lg_ref[...] + lb_ref[...]
        mix_ref[c * rc:(c + 1) * rc, :] = y * jax.nn.sigmoid(y)
    new_state = vp_ref[pl.ds(lo + tt, CONV_STATE), :]
    ns_ref[...] = new_state
    vp_ref[lo:CONV_PAD, :] = new_state


def conv_mixer(u, state, w, b, lg, lb, nseq, t_len, tt, row0=0):
    rc = min(tt, 32)
    nt = t_len // tt
    blk0 = row0 // tt
    vec = lambda a: a.reshape(1, C_CONV)
    return pl.pallas_call(
        functools.partial(_conv_kernel, tt=tt, rc=rc),
        grid=(nseq, nt),
        in_specs=[pl.BlockSpec((tt, 2 * C_CONV), lambda s, t: (blk0 + s * nt + t, 0)),
                  pl.BlockSpec((None, CONV_STATE, C_CONV), lambda s, t: (s, 0, 0)),
                  pl.BlockSpec((CONV_K, C_CONV), lambda s, t: (0, 0)),
                  pl.BlockSpec((1, C_CONV), lambda s, t: (0, 0)),
                  pl.BlockSpec((1, C_CONV), lambda s, t: (0, 0)),
                  pl.BlockSpec((1, C_CONV), lambda s, t: (0, 0))],
        out_specs=[pl.BlockSpec((tt, C_CONV), lambda s, t: (s * nt + t, 0)),
                   pl.BlockSpec((None, CONV_STATE, C_CONV), lambda s, t: (s, 0, 0))],
        out_shape=[jax.ShapeDtypeStruct((nseq * t_len, C_CONV), F32),
                   jax.ShapeDtypeStruct((nseq, CONV_STATE, C_CONV), F32)],
        scratch_shapes=[pltpu.VMEM((CONV_PAD + tt + SUBLANES, C_CONV), F32)],
        compiler_params=_params(),
        name="conv_mixer",
    )(u, state, w, vec(b), vec(lg), vec(lb))


def _mem_attn_kernel(q_ref, kv_ref, o_ref, *, spb, tq):
    n_mem = kv_ref.shape[1]
    q = q_ref[...]
    kv = kv_ref[...].reshape(spb * n_mem, 2 * MEM_W).astype(BF16)
    own = None
    if spb > 1:
        q_seq = lax.broadcasted_iota(jnp.int32, (spb * tq, 1), 0) >> _log2(tq)
        k_seq = lax.broadcasted_iota(jnp.int32, (1, spb * n_mem), 1) >> _log2(n_mem)
        own = q_seq == k_seq
    for h in range(MEM_HEADS):
        c = h * MEM_HEAD_DIM
        qh = q[:, c:c + MEM_HEAD_DIM].astype(BF16)
        s = _dot_nt(qh, kv[:, c:c + MEM_HEAD_DIM]) * (MEM_HEAD_DIM ** -0.5)
        if own is not None:
            s = jnp.where(own, s, NEG)
        e = jnp.exp(s - jnp.max(s, axis=-1, keepdims=True))
        o = _dot(e.astype(BF16), kv[:, MEM_W + c:MEM_W + c + MEM_HEAD_DIM])
        o_ref[:, c:c + MEM_HEAD_DIM] = o / jnp.sum(e, axis=-1, keepdims=True)


def mem_attend(qm, kv, layer, nseq, t_len, tq, spb=1, row0=0):
    nt = t_len // tq
    assert spb == 1 or nt == 1
    n_mem = kv.shape[2]
    rows = spb * tq
    blk0 = row0 // rows
    return pl.pallas_call(
        functools.partial(_mem_attn_kernel, spb=spb, tq=tq),
        grid=(nseq // spb, nt),
        in_specs=[pl.BlockSpec((rows, MEM_W), lambda s, t: (blk0 + s * nt + t, 0)),
                  pl.BlockSpec((None, spb, n_mem, 2 * MEM_W), lambda s, t: (layer, s, 0, 0))],
        out_specs=pl.BlockSpec((rows, MEM_W), lambda s, t: (s * nt + t, 0)),
        out_shape=jax.ShapeDtypeStruct((nseq * t_len, MEM_W), F32),
        compiler_params=_params(),
        name="mem_attend",
    )(qm, kv)


def _outproj_kernel(x_ref, mixp_ref, mixs_ref, mop_ref, mos_ref, w_ref, g_ref, *rest, moe, prompt_tiles):
    if moe:
        r_ref, xn_ref, h_ref, gate_ref = rest
    else:
        xn_ref, h_ref = rest
    is_prompt = pl.program_id(0) < prompt_tiles
    mix = jnp.where(is_prompt, mixp_ref[...].astype(BF16), mixs_ref[...].astype(BF16))
    mo = jnp.where(is_prompt, mop_ref[...].astype(BF16), mos_ref[...].astype(BF16))
    d_mix = mix.shape[1]
    y = _dot(mix, w_ref[0:d_mix, :]) + _dot(mo, w_ref[d_mix:, :])
    xn = x_ref[...] + y
    xn_ref[...] = xn
    h = _rms(xn, g_ref[...])
    h_ref[...] = h.astype(BF16)
    if moe:
        h_hi, h_lo = _split_hi_lo(h)
        r_hi, r_lo = _split_hi_lo(r_ref[...])
        logits = _dot(h_hi, r_hi) + _dot(h_lo, r_hi) + _dot(h_hi, r_lo)
        lane = lax.broadcasted_iota(jnp.int32, logits.shape, 1)
        logits = jnp.where(lane < N_EXPERTS, logits, -jnp.inf)
        m1 = jnp.max(logits, axis=-1, keepdims=True)
        i1 = jnp.min(jnp.where(logits == m1, lane, LANES), axis=-1, keepdims=True)
        rest_l = jnp.where(lane == i1, -jnp.inf, logits)
        m2 = jnp.max(rest_l, axis=-1, keepdims=True)
        i2 = jnp.min(jnp.where(rest_l == m2, lane, LANES), axis=-1, keepdims=True)
        e2 = jnp.exp(m2 - m1)
        den = 1.0 + e2
        gate_ref[...] = jnp.where(lane == i1, 1.0 / den, 0.0) + jnp.where(lane == i2, e2 / den, 0.0)


def outproj_norm(x, mix_p, mix_s, mo_p, mo_s, w, g, router=None):
    n, d = x.shape
    tm = _row_tile(n, 512)
    assert mix_p.shape[0] % tm == 0 and mix_s.shape[0] % tm == 0
    pt = mix_p.shape[0] // tm
    moe = router is not None
    row = lambda c: pl.BlockSpec((tm, c), lambda i: (i, 0))
    row_p = lambda c: pl.BlockSpec((tm, c), lambda i: (jnp.minimum(i, pt - 1), 0))
    row_s = lambda c: pl.BlockSpec((tm, c), lambda i: (jnp.maximum(i - pt, 0), 0))
    full = lambda a: pl.BlockSpec(a.shape, lambda i: (0, 0))
    ins = [x, mix_p, mix_s, mo_p, mo_s, w, g.reshape(1, d)]
    in_specs = [row(d), row_p(mix_p.shape[1]), row_s(mix_s.shape[1]), row_p(mo_p.shape[1]), row_s(mo_s.shape[1]),
                full(w), pl.BlockSpec((1, d), lambda i: (0, 0))]
    out_specs = [row(d), row(d)]
    out_shape = [jax.ShapeDtypeStruct((n, d), F32), jax.ShapeDtypeStruct((n, d), BF16)]
    if moe:
        rp = jnp.pad(router, ((0, 0), (0, LANES - router.shape[1])))
        ins.append(rp)
        in_specs.append(full(rp))
        out_specs.append(row(LANES))
        out_shape.append(jax.ShapeDtypeStruct((n, LANES), F32))
    return pl.pallas_call(
        functools.partial(_outproj_kernel, moe=moe, prompt_tiles=pt),
        grid=(n // tm,),
        in_specs=in_specs, out_specs=out_specs, out_shape=out_shape,
        compiler_params=_params(),
        name="outproj_norm",
    )(*ins)


def _ffn_kernel(h_ref, x_ref, wg_ref, wu_ref, wd_ref, o_ref):
    @pl.when(pl.program_id(1) == 0)
    def _():
        o_ref[...] = x_ref[...]

    h = h_ref[...]
    a = _dot(h, wg_ref[...])
    u = _dot(h, wu_ref[...])
    act = (a * jax.nn.sigmoid(a) * u).astype(BF16)
    o_ref[...] += _dot(act, wd_ref[...])


def ffn_dense(h, x, wg, wu, wd, nf=2):
    n, d = x.shape
    tm = _row_tile(n, 512)
    tf = wg.shape[1] // nf
    return pl.pallas_call(
        _ffn_kernel,
        grid=(n // tm, nf),
        in_specs=[pl.BlockSpec((tm, d), lambda i, j: (i, 0)),
                  pl.BlockSpec((tm, d), lambda i, j: (i, 0)),
                  pl.BlockSpec((d, tf), lambda i, j: (0, j)),
                  pl.BlockSpec((d, tf), lambda i, j: (0, j)),
                  pl.BlockSpec((tf, d), lambda i, j: (j, 0))],
        out_specs=pl.BlockSpec((tm, d), lambda i, j: (i, 0)),
        out_shape=jax.ShapeDtypeStruct((n, d), F32),
        compiler_params=_params(),
        name="ffn_dense",
    )(h, x, wg, wu, wd)


MOE_RB = 128
MOE_RB_SHIFT = 7


def _moe_kernel(cnt_ref, h_ref, x_ref, gate_ref, gt_ref, wg_ref, wu_ref, wd_ref, o_ref,
                xs_ref, y_ref, srow_ref, scol_ref):
    i, e, j = pl.program_id(0), pl.program_id(1), pl.program_id(2)
    tm = h_ref.shape[0]
    nblk = (cnt_ref[i, e] + (MOE_RB - 1)) >> MOE_RB_SHIFT

    @pl.when((e == 0) & (j == 0))
    def _():
        o_ref[...] = x_ref[...]
        r = lax.broadcasted_iota(jnp.int32, (tm, tm), 0)
        c = lax.broadcasted_iota(jnp.int32, (tm, tm), 1)
        srow_ref[...] = _dot((gt_ref[...] != 0.0).astype(BF16), (r < c).astype(BF16))
        scol_ref[...] = _dot((c < r).astype(BF16), (gate_ref[...] != 0.0).astype(BF16))

    def block_rows(k):
        return pl.ds(pl.multiple_of(k * MOE_RB, MOE_RB), MOE_RB)

    def pick_row(a):
        return jnp.sum(jnp.where(lax.broadcasted_iota(jnp.int32, a.shape, 0) == e, a, 0.0), axis=0, keepdims=True)

    def pick_col(a):
        return jnp.sum(jnp.where(lax.broadcasted_iota(jnp.int32, a.shape, 1) == e, a, 0.0), axis=1, keepdims=True)

    @pl.when(j == 0)
    def _():
        mrow = pick_row(gt_ref[...]) != 0.0
        slot = pick_row(srow_ref[...])

        def pack(k, _):
            rid = (k * MOE_RB + lax.broadcasted_iota(jnp.int32, (MOE_RB, 1), 0)).astype(F32)
            onehot = (mrow & (slot == rid)).astype(BF16)
            xs_ref[block_rows(k), :] = _dot(onehot, h_ref[...]).astype(BF16)
            y_ref[block_rows(k), :] = jnp.zeros((MOE_RB, y_ref.shape[1]), F32)
            return 0

        lax.fori_loop(0, nblk, pack, 0)

    def expert(k, _):
        xs = xs_ref[block_rows(k), :]
        a = _dot(xs, wg_ref[...])
        u = _dot(xs, wu_ref[...])
        act = (a * jax.nn.sigmoid(a) * u).astype(BF16)
        y_ref[block_rows(k), :] += _dot(act, wd_ref[...])
        return 0

    lax.fori_loop(0, nblk, expert, 0)

    @pl.when(j == pl.num_programs(2) - 1)
    def _():
        ge = pick_col(gate_ref[...])
        mcol = ge != 0.0
        slot = pick_col(scol_ref[...])

        def unpack(k, _):
            cid = (k * MOE_RB + lax.broadcasted_iota(jnp.int32, (1, MOE_RB), 1)).astype(F32)
            onehot_t = (mcol & (slot == cid)).astype(BF16)
            y_hi, y_lo = _split_hi_lo(y_ref[block_rows(k), :])
            o_ref[...] += ge * _dot(jnp.concatenate([onehot_t, onehot_t], axis=1),
                                    jnp.concatenate([y_hi, y_lo], axis=0))
            return 0

        lax.fori_loop(0, nblk, unpack, 0)


def moe_ffn(h, x, gate, wg, wu, wd, layer):
    n, d = x.shape
    tm = _row_tile(n, 1024)
    nt = n // tm
    n_exp, d_exp = wg.shape[1], wg.shape[3]
    tf = next(t for t in (896, 512, 256, 128) if d_exp % t == 0)
    nf = d_exp // tf
    routed = gate[:, :n_exp] != 0.0
    counts = jnp.sum(routed.reshape(nt, tm, n_exp), axis=1, dtype=jnp.int32)
    gate_t = jnp.transpose(gate[:, :n_exp])
    return pl.pallas_call(
        _moe_kernel,
        grid_spec=pltpu.PrefetchScalarGridSpec(
            num_scalar_prefetch=1,
            grid=(nt, n_exp, nf),
            in_specs=[pl.BlockSpec((tm, d), lambda i, e, j, c: (i, 0)),
                      pl.BlockSpec((tm, d), lambda i, e, j, c: (i, 0)),
                      pl.BlockSpec((tm, LANES), lambda i, e, j, c: (i, 0)),
                      pl.BlockSpec((n_exp, tm), lambda i, e, j, c: (0, i)),
                      pl.BlockSpec((None, None, d, tf), lambda i, e, j, c: (layer, e, 0, j)),
                      pl.BlockSpec((None, None, d, tf), lambda i, e, j, c: (layer, e, 0, j)),
                      pl.BlockSpec((None, None, tf, d), lambda i, e, j, c: (layer, e, j, 0))],
            out_specs=pl.BlockSpec((tm, d), lambda i, e, j, c: (i, 0)),
            scratch_shapes=[pltpu.VMEM((tm, d), BF16), pltpu.VMEM((tm, d), F32),
                            pltpu.VMEM((n_exp, tm), F32), pltpu.VMEM((tm, LANES), F32)],
        ),
        out_shape=jax.ShapeDtypeStruct((n, d), F32),
        compiler_params=_params(),
        name="moe_ffn",
    )(counts, h, x, gate, gate_t, wg, wu, wd)


def _final_norm_kernel(x_ref, g_ref, o_ref):
    o_ref[...] = _rms(x_ref[...], g_ref[...])


def final_norm(x, g, row0, n):
    d = x.shape[1]
    tm = _row_tile(n, 1024)
    assert row0 % tm == 0
    blk0 = row0 // tm
    return pl.pallas_call(
        _final_norm_kernel,
        grid=(n // tm,),
        in_specs=[pl.BlockSpec((tm, d), lambda i: (blk0 + i, 0)), pl.BlockSpec((1, d), lambda i: (0, 0))],
        out_specs=pl.BlockSpec((tm, d), lambda i: (i, 0)),
        out_shape=jax.ShapeDtypeStruct((n, d), F32),
        compiler_params=_params(),
        name="final_norm",
    )(x, g.reshape(1, d))


SLC_TILE = 512
WIN_TILE = 128


def _kv_prep_kernel(s_ref, w_ref, tab_ref, ska_ref, svt_ref, wka_ref, wvt_ref):
    tab = tab_ref[...]
    onehot, pieces = tab[:, :LANES], tab[:, LANES:]
    s = s_ref[...]
    w = w_ref[...]
    for g in range(N_KV_GROUPS):
        kc = slice(g * HEAD_DIM, (g + 1) * HEAD_DIM)
        vc = slice((N_KV_GROUPS + g) * HEAD_DIM, (N_KV_GROUPS + g + 1) * HEAD_DIM)
        ska_ref[g] = jnp.concatenate([onehot, s[:, kc].astype(BF16), pieces], axis=1)
        svt_ref[g] = s[:, vc].T.astype(BF16)
        wka_ref[g] = jnp.concatenate([w[:, kc].astype(BF16), pieces], axis=1)
        wv_t = w[:, vc].T.astype(BF16)
        for j in range(SLC_TILE // WIN_TILE):
            wvt_ref[g, j] = wv_t[:, j * WIN_TILE:(j + 1) * WIN_TILE]


def kv_prep(slc_rows, win_rows, tab, nb, t_len):
    tm = SLC_TILE
    nt = t_len // tm
    wpt = tm // WIN_TILE
    g3 = N_KV_GROUPS
    return pl.pallas_call(
        _kv_prep_kernel,
        grid=(nb, nt),
        in_specs=[pl.BlockSpec((tm, KV_ROW), lambda b, t: (b * nt + t, 0)),
                  pl.BlockSpec((tm, KV_ROW), lambda b, t: (b * nt + t, 0)),
                  pl.BlockSpec((tm, tab.shape[1]), lambda b, t: (t, 0))],
        out_specs=[pl.BlockSpec((None, g3, tm, 2 * LANES), lambda b, t: (b, 0, t, 0)),
                   pl.BlockSpec((None, g3, None, HEAD_DIM, tm), lambda b, t: (b, 0, t, 0, 0)),
                   pl.BlockSpec((None, g3, tm, LANES), lambda b, t: (b, 0, t, 0)),
                   pl.BlockSpec((None, g3, wpt, HEAD_DIM, WIN_TILE), lambda b, t: (b, 0, t, 0, 0))],
        out_shape=[jax.ShapeDtypeStruct((nb, g3, t_len, 2 * LANES), BF16),
                   jax.ShapeDtypeStruct((nb, g3, nt, HEAD_DIM, tm), BF16),
                   jax.ShapeDtypeStruct((nb, g3, t_len, LANES), BF16),
                   jax.ShapeDtypeStruct((nb, g3, t_len // WIN_TILE, HEAD_DIM, WIN_TILE), BF16)],
        compiler_params=_params(),
        name="kv_prep",
    )(slc_rows, win_rows, tab)


CMP_PGROUP = 4


def _cmp_first_layer(get_cols, kv, pe_ref, w1_ref):
    acc_a = None
    acc_b = None
    kw = CMP_PGROUP * HEAD_DIM
    for pg in range(CMP_STRIDE // CMP_PGROUP):
        xs = jnp.concatenate(
            [jnp.concatenate([get_cols(p * KV_ROW + (kv * N_KV_GROUPS + g) * HEAD_DIM)
                              for p in range(pg * CMP_PGROUP, (pg + 1) * CMP_PGROUP)], axis=1)
             for g in range(N_KV_GROUPS)], axis=0)
        qg = pg + CMP_STRIDE // CMP_PGROUP
        da = _dot((xs + pe_ref[kv, pg:pg + 1, :]).astype(BF16), w1_ref[kv, pg * kw:(pg + 1) * kw, :])
        db = _dot((xs + pe_ref[kv, qg:qg + 1, :]).astype(BF16), w1_ref[kv, qg * kw:(qg + 1) * kw, :])
        acc_a = da if acc_a is None else acc_a + da
        acc_b = db if acc_b is None else acc_b + db
    return acc_a, acc_b


def _cmp_second_layer(kv, nch, a_ref, b_ref, w2_ref, emit):
    for g in range(N_KV_GROUPS):
        hid = a_ref[g, 0:nch, :] + b_ref[g, pl.ds(1, nch), :]
        hid = hid * jax.nn.sigmoid(hid)
        emit(kv, g, _dot(hid.astype(BF16), w2_ref[kv]))


def _compress_prompt_kernel(x_ref, pe_ref, w1_ref, w2_ref, cke_ref, cka_ref, cvt_ref, a_ref, b_ref, *, nch, tc):
    def emit(kv, g, out):
        if kv == 0:
            cka_ref[g] = jnp.concatenate([out.astype(BF16), cke_ref[...]], axis=1)
        else:
            cvt_ref[g] = out.T.astype(BF16)

    for kv in range(2):
        for ct in range(nch // tc):
            rows = slice(ct * tc, (ct + 1) * tc)
            acc_a, acc_b = _cmp_first_layer(lambda c0: x_ref[rows, c0:c0 + HEAD_DIM], kv, pe_ref, w1_ref)
            for g in range(N_KV_GROUPS):
                a_ref[g, rows, :] = acc_a[g * tc:(g + 1) * tc]
                b_ref[g, rows, :] = acc_b[g * tc:(g + 1) * tc]
        b_ref[:, nch:nch + 8, :] = jnp.zeros((N_KV_GROUPS, 8, b_ref.shape[2]), F32)
        _cmp_second_layer(kv, nch, a_ref, b_ref, w2_ref, emit)


def compress_prompt(cmp_rows, pe, w1, w2, cke, nb, t_len):
    nch = t_len // CMP_STRIDE
    tc = min(nch, 128)
    hid = w1.shape[2]
    xc = cmp_rows.reshape(nb, nch, CHUNK_ROW)
    full = lambda a: pl.BlockSpec(a.shape, lambda b: (0,) * a.ndim)
    return pl.pallas_call(
        functools.partial(_compress_prompt_kernel, nch=nch, tc=tc),
        grid=(nb,),
        in_specs=[pl.BlockSpec((None, nch, CHUNK_ROW), lambda b: (b, 0, 0)), full(pe), full(w1), full(w2), full(cke)],
        out_specs=[pl.BlockSpec((None, N_KV_GROUPS, nch, LANES), lambda b: (b, 0, 0, 0)),
                   pl.BlockSpec((None, N_KV_GROUPS, HEAD_DIM, nch), lambda b: (b, 0, 0, 0))],
        out_shape=[jax.ShapeDtypeStruct((nb, N_KV_GROUPS, nch, LANES), BF16),
                   jax.ShapeDtypeStruct((nb, N_KV_GROUPS, HEAD_DIM, nch), BF16)],
        scratch_shapes=[pltpu.VMEM((N_KV_GROUPS, nch + 8, hid), F32)] * 2,
        compiler_params=_params(),
        name="compress_prompt",
    )(xc, pe, w1, w2, cke)


def _compress_sample_kernel(pt_ref, *refs, n_pages):
    page_refs = refs[:n_pages]
    new_ref, pe_ref, w1_ref, w2_ref, ck_ref, cv_ref, a_ref, b_ref = refs[n_pages:]
    nch = n_pages * (PAGE_SIZE // CMP_STRIDE)
    for kv in range(2):
        acc_a, acc_b = _cmp_first_layer(
            lambda c0: jnp.concatenate([pr[:, c0:c0 + HEAD_DIM] for pr in page_refs], axis=0), kv, pe_ref, w1_ref)
        _, new_b = _cmp_first_layer(
            lambda c0: jnp.broadcast_to(new_ref[:, c0:c0 + HEAD_DIM], (8, HEAD_DIM)), kv, pe_ref, w1_ref)
        for g in range(N_KV_GROUPS):
            a_ref[g, 0:nch, :] = acc_a[g * nch:(g + 1) * nch]
            b_ref[g, 0:nch, :] = acc_b[g * nch:(g + 1) * nch]
            b_ref[g, nch:nch + 8, :] = new_b[g * 8:(g + 1) * 8]

        def emit(kv, g, out):
            (ck_ref, cv_ref)[kv][g] = out.astype(BF16)

        _cmp_second_layer(kv, nch, a_ref, b_ref, w2_ref, emit)


def compress_sample(pool, page_table, cmp_new, pe, w1, w2):
    nseq, n_pages = page_table.shape
    dec = cmp_new.shape[1]
    cpp = PAGE_SIZE // CMP_STRIDE
    nch = n_pages * cpp
    hid = w1.shape[2]
    pool_c = pool.reshape(pool.shape[0], cpp, CHUNK_ROW)
    new_c = jnp.pad(cmp_new.reshape(nseq, 1, dec * KV_ROW), ((0, 0), (0, 0), (0, CHUNK_ROW - dec * KV_ROW)))
    page_spec = lambda p: pl.BlockSpec((None, cpp, CHUNK_ROW), lambda b, pt: (pt[b, p], 0, 0))
    full = lambda a: pl.BlockSpec(a.shape, lambda b, pt: (0,) * a.ndim)
    o_spec = pl.BlockSpec((None, N_KV_GROUPS, nch, HEAD_DIM), lambda b, pt: (b, 0, 0, 0))
    o_shape = jax.ShapeDtypeStruct((nseq, N_KV_GROUPS, nch, HEAD_DIM), BF16)
    return pl.pallas_call(
        functools.partial(_compress_sample_kernel, n_pages=n_pages),
        grid_spec=pltpu.PrefetchScalarGridSpec(
            num_scalar_prefetch=1,
            grid=(nseq,),
            in_specs=[page_spec(p) for p in range(n_pages)]
            + [pl.BlockSpec((None, 1, CHUNK_ROW), lambda b, pt: (b, 0, 0)), full(pe), full(w1), full(w2)],
            out_specs=[o_spec, o_spec],
            scratch_shapes=[pltpu.VMEM((N_KV_GROUPS, nch + 8, hid), F32)] * 2,
        ),
        out_shape=[o_shape, o_shape],
        compiler_params=_params(),
        name="compress_sample",
    )(page_table, *([pool_c] * n_pages), new_c, pe, w1, w2)


def _softmax_parts(s, axis=-1):
    m = jnp.maximum(jnp.max(s, axis=axis, keepdims=True), M_INIT)
    e = jnp.exp(s - m)
    return e, jnp.sum(e, axis=axis, keepdims=True)


def _masked_softmax(s, axis=-1):
    e, d = _softmax_parts(s, axis)
    return e * (1.0 / jnp.where(d > 0, d, 1.0))


def _select_blocks(imp, cur, axis=-1):
    axis = axis % imp.ndim
    j = lax.broadcasted_iota(jnp.int32, imp.shape, axis)
    valid = j <= cur
    forced = valid & ((j == 0) | (j == cur) | (j == cur - 1))
    v = jnp.where(forced, jnp.inf, jnp.where(valid, imp, -jnp.inf))
    sel = jnp.zeros(imp.shape, F32)
    for _ in range(N_SEL):
        m = jnp.max(v, axis=axis, keepdims=True)
        jm = jnp.min(jnp.where(v == m, j, LANES), axis=axis, keepdims=True)
        pick = j == jm
        v = jnp.where(pick, -jnp.inf, v)
        sel = jnp.where(pick, 1.0, sel)
    return (sel > 0.0) & valid


def _stack_heads(q):
    return jnp.concatenate([q[:, r * HEAD_DIM:(r + 1) * HEAD_DIM] for r in range(HEADS_PER_GROUP)], axis=0)


def _importance(p_c, tq, mimp_ref):
    pcs = p_c[0:tq]
    for r in range(1, HEADS_PER_GROUP):
        pcs = pcs + p_c[r * tq:(r + 1) * tq]
    hi, lo = _split_hi_lo(pcs)
    return _dot(hi, mimp_ref[...]) + _dot(lo, mimp_ref[...])


def _gated_merge(gsig, g, o_c, o_s, o_w, tq, store):
    for r in range(HEADS_PER_GROUP):
        c = 3 * (g * HEADS_PER_GROUP + r)
        rows = slice(r * tq, (r + 1) * tq)
        store(r, gsig[:, c:c + 1] * o_c[rows] + gsig[:, c + 1:c + 2] * o_s[rows] + gsig[:, c + 2:c + 3] * o_w[rows])


def _nsa_prompt_kernel(q_ref, glt_ref, qe_ref, cka_ref, cvt_ref, ska_ref, svt_ref, wka_ref, wvt_ref,
                       mimpt_ref, o_ref, *, t_len):
    i = pl.program_id(2)
    tq = Q_BLOCK
    rows = HEADS_PER_GROUP * tq
    ncp = t_len // CMP_STRIDE
    s0 = i * tq
    q4 = (_stack_heads(q_ref[...]) * (HEAD_DIM ** -0.5)).astype(BF16)
    qb = jnp.concatenate([q4, qe_ref[...]], axis=1)
    qpos = s0 + (lax.broadcasted_iota(jnp.int32, (1, rows), 1) & (tq - 1))

    cpos = lax.broadcasted_iota(jnp.int32, (ncp, 1), 0) * CMP_STRIDE + (CMP_BLOCK - 1)
    mask_c = cpos <= qpos
    p_c = _masked_softmax(jnp.where(mask_c, _dot_nt(cka_ref[...], qb), NEG), axis=0)
    o_c = _dot(cvt_ref[...], p_c.astype(BF16))

    nwt = (WINDOW + tq) // WIN_TILE
    wb = jnp.maximum(i - WINDOW // WIN_TILE, 0)
    kw = pl.ds(pl.multiple_of(wb * WIN_TILE, WIN_TILE), nwt * WIN_TILE)
    dist = qpos - (wb * WIN_TILE + lax.broadcasted_iota(jnp.int32, (nwt * WIN_TILE, 1), 0))
    mask_w = (dist >= 0) & (dist < WINDOW)
    e_w, d_w = _softmax_parts(jnp.where(mask_w, _dot_nt(wka_ref[kw, :], qb), NEG), axis=0)
    e_w = e_w.astype(BF16)
    o_w = _dot(wvt_ref[wb], e_w[0:WIN_TILE])
    for w in range(1, nwt):
        o_w = o_w + _dot(wvt_ref[wb + w], e_w[w * WIN_TILE:(w + 1) * WIN_TILE])
    o_w = o_w / d_w

    pcs = p_c[:, 0:tq]
    for r in range(1, HEADS_PER_GROUP):
        pcs = pcs + p_c[:, r * tq:(r + 1) * tq]
    hi, lo = _split_hi_lo(pcs)
    imp = _dot(mimpt_ref[...], hi) + _dot(mimpt_ref[...], lo)
    sel = _select_blocks(imp, qpos[:, 0:tq] >> SLC_SHIFT, axis=0)
    selbias = jnp.where(sel, 0.0, NEG).T.astype(BF16)
    qa = jnp.concatenate([jnp.concatenate([selbias] * HEADS_PER_GROUP, axis=0), qb], axis=1)

    def scores(kt):
        return _dot_nt(ska_ref[pl.ds(pl.multiple_of(kt * SLC_TILE, SLC_TILE), SLC_TILE), :], qa)

    def flash_update(stats, s, vt):
        m, l, acc = stats
        m_new = jnp.maximum(m, jnp.max(s, axis=0, keepdims=True))
        a = jnp.exp(m - m_new)
        p = jnp.exp(s - m_new)
        return m_new, a * l + jnp.sum(p, axis=0, keepdims=True), a * acc + _dot(vt, p.astype(BF16))

    def bulk(kt, carry):
        s_cur, stats = carry
        s_next = scores(kt + 1)
        return s_next, flash_update(stats, s_cur, svt_ref[kt])

    stats = (jnp.full((1, rows), M_INIT, F32), jnp.zeros((1, rows), F32), jnp.zeros((HEAD_DIM, rows), F32))
    kt_last = s0 // SLC_TILE
    s_last, stats = lax.fori_loop(0, kt_last, bulk, (scores(0), stats))
    kpos = kt_last * SLC_TILE + lax.broadcasted_iota(jnp.int32, (SLC_TILE, 1), 0)
    _, l_s, acc_s = flash_update(stats, jnp.where(kpos <= qpos, s_last, NEG), svt_ref[kt_last])
    o_s = acc_s / l_s

    gsig = jax.nn.sigmoid(glt_ref[...])
    lane_gate = lambda c: jnp.concatenate([gsig[3 * r + c:3 * r + c + 1, :] for r in range(HEADS_PER_GROUP)], axis=1)
    out_t = lane_gate(0) * o_c + lane_gate(1) * o_s + lane_gate(2) * o_w
    for r in range(HEADS_PER_GROUP):
        o_ref[:, r * HEAD_DIM:(r + 1) * HEAD_DIM] = out_t[:, r * tq:(r + 1) * tq].T.astype(o_ref.dtype)


def nsa_prompt(q, gl_t, cka, cvt, ska, svt, wka, wvt, consts, nb, t_len):
    nqb = t_len // Q_BLOCK
    rows = HEADS_PER_GROUP * Q_BLOCK
    gw = 3 * HEADS_PER_GROUP
    qe, mimp_t = consts
    per_bg = lambda a: pl.BlockSpec((None, None) + a.shape[2:], lambda b, g, i: (b, g) + (0,) * (a.ndim - 2))
    return pl.pallas_call(
        functools.partial(_nsa_prompt_kernel, t_len=t_len),
        grid=(nb, N_KV_GROUPS, nqb),
        in_specs=[pl.BlockSpec((Q_BLOCK, HEADS_PER_GROUP * HEAD_DIM), lambda b, g, i: (b * nqb + i, g)),
                  pl.BlockSpec((None, gw, Q_BLOCK), lambda b, g, i: (g, 0, b * nqb + i)),
                  pl.BlockSpec((None, rows, HEAD_DIM), lambda b, g, i: (g, 0, 0)),
                  per_bg(cka), per_bg(cvt), per_bg(ska), per_bg(svt), per_bg(wka), per_bg(wvt),
                  pl.BlockSpec(mimp_t.shape, lambda b, g, i: (0, 0))],
        out_specs=pl.BlockSpec((Q_BLOCK, HEADS_PER_GROUP * HEAD_DIM), lambda b, g, i: (b * nqb + i, g)),
        out_shape=jax.ShapeDtypeStruct((nb * t_len, NSA_W), BF16),
        compiler_params=_params(),
        name="nsa_prompt",
    )(q, gl_t, qe, cka, cvt, ska, svt, wka, wvt, mimp_t)


def _nsa_sample_kernel(pt_ref, *refs, n_pages, dec, win_buf):
    page_refs = refs[:n_pages]
    (q_ref, gl_ref, slp_ref, ck_ref, cv_ref, snew_ref, wc_ref, wnew_ref, es_ref, mimp_ref, o_ref) = refs[n_pages:]
    tq = dec
    rows = HEADS_PER_GROUP * tq
    past = n_pages * PAGE_SIZE
    ncs = ck_ref.shape[1]
    t_row = lax.broadcasted_iota(jnp.int32, (rows, 1), 0) & (tq - 1)
    qpos = past + t_row
    gsig = jax.nn.sigmoid(gl_ref[...])
    q_all = q_ref[...]
    snew = snew_ref[...]
    wc = wc_ref[...]
    wnew = wnew_ref[...]
    tnew = lax.broadcasted_iota(jnp.int32, (1, tq), 1)
    mask_new = tnew <= t_row
    dist_new = (t_row - tnew).astype(F32)

    q4s, o_cs, imps = [], [], []
    for g in range(N_KV_GROUPS):
        q4 = (_stack_heads(q_all[:, g * HEADS_PER_GROUP * HEAD_DIM:(g + 1) * HEADS_PER_GROUP * HEAD_DIM])
              * (HEAD_DIM ** -0.5)).astype(BF16)
        cpos = lax.broadcasted_iota(jnp.int32, (1, ncs), 1) * CMP_STRIDE + (CMP_BLOCK - 1)
        mask_c = cpos <= qpos
        s_c = _dot_nt(q4, ck_ref[g]) - slp_ref[g] * (qpos - cpos).astype(F32)
        p_c = _masked_softmax(jnp.where(mask_c, s_c, NEG))
        q4s.append(q4)
        o_cs.append(_dot(p_c.astype(BF16), cv_ref[g]))
        imps.append(_importance(p_c, tq, mimp_ref))
    cur = (past + (lax.broadcasted_iota(jnp.int32, (1, N_KV_GROUPS * tq), 1) & (tq - 1))) >> SLC_SHIFT
    sel_t = _select_blocks(jnp.concatenate(imps, axis=0).T, cur, axis=0)
    selbias_all = jnp.where(sel_t, 0.0, NEG).T

    for g in range(N_KV_GROUPS):
        kc = slice(g * HEAD_DIM, (g + 1) * HEAD_DIM)
        vc = slice((N_KV_GROUPS + g) * HEAD_DIM, (N_KV_GROUPS + g + 1) * HEAD_DIM)
        q4, o_c = q4s[g], o_cs[g]
        slope = slp_ref[g]
        sb4 = jnp.concatenate([selbias_all[g * tq:(g + 1) * tq].astype(BF16)] * HEADS_PER_GROUP, axis=0)

        k_past = jnp.concatenate([pr[:, kc] for pr in page_refs], axis=0).astype(BF16)
        v_past = jnp.concatenate([pr[:, vc] for pr in page_refs], axis=0).astype(BF16)
        kpos = lax.broadcasted_iota(jnp.int32, (1, past), 1)
        s_p = _dot_nt(q4, k_past) + _dot_nt(sb4, es_ref[...]) - slope * (qpos - kpos).astype(F32)
        s_n = _dot_nt(q4, snew[:, kc].astype(BF16)) - slope * dist_new
        s_n = jnp.where(mask_new, s_n, NEG)
        m = jnp.maximum(jnp.max(s_p, axis=-1, keepdims=True), jnp.max(s_n, axis=-1, keepdims=True))
        e_p = jnp.exp(s_p - m)
        e_n = jnp.exp(s_n - m)
        l = jnp.sum(e_p, axis=-1, keepdims=True) + jnp.sum(e_n, axis=-1, keepdims=True)
        o_s = (_dot(e_p.astype(BF16), v_past) + _dot(e_n.astype(BF16), snew[:, vc].astype(BF16))) / l

        wdist = qpos - (past - win_buf + lax.broadcasted_iota(jnp.int32, (1, win_buf), 1))
        mask_w = wdist < WINDOW
        s_w = _dot_nt(q4, wc[:, kc].astype(BF16)) - slope * wdist.astype(F32)
        s_w = jnp.where(mask_w, s_w, NEG)
        s_wn = _dot_nt(q4, wnew[:, kc].astype(BF16)) - slope * dist_new
        s_wn = jnp.where(mask_new, s_wn, NEG)
        m = jnp.maximum(jnp.max(s_w, axis=-1, keepdims=True), jnp.max(s_wn, axis=-1, keepdims=True))
        e_w = jnp.exp(s_w - m)
        e_wn = jnp.exp(s_wn - m)
        l = jnp.sum(e_w, axis=-1, keepdims=True) + jnp.sum(e_wn, axis=-1, keepdims=True)
        o_w = (_dot(e_w.astype(BF16), wc[:, vc].astype(BF16)) + _dot(e_wn.astype(BF16), wnew[:, vc].astype(BF16))) / l

        def store(r, val, g=g):
            c = (g * HEADS_PER_GROUP + r) * HEAD_DIM
            o_ref[:, c:c + HEAD_DIM] = val.astype(o_ref.dtype)

        _gated_merge(gsig, g, o_c, o_s, o_w, tq, store)


def nsa_sample(q, gl, ck, cv, pool, page_table, slc_rows, win_cache, win_rows, consts, row_off, dec):
    nseq, n_pages = page_table.shape
    win_buf = win_cache.shape[1]
    slp, es, mimp = consts
    full = lambda a: pl.BlockSpec(a.shape, lambda b, pt: (0,) * a.ndim)
    row = lambda c: pl.BlockSpec((dec, c), lambda b, pt: (row_off + b, 0))
    seq4 = lambda a: pl.BlockSpec((None,) + a.shape[1:], lambda b, pt: (b, 0, 0, 0))
    page_spec = lambda p: pl.BlockSpec((None, PAGE_SIZE, KV_ROW), lambda b, pt: (pt[b, p], 0, 0))
    return pl.pallas_call(
        functools.partial(_nsa_sample_kernel, n_pages=n_pages, dec=dec, win_buf=win_buf),
        grid_spec=pltpu.PrefetchScalarGridSpec(
            num_scalar_prefetch=1,
            grid=(nseq,),
            in_specs=[page_spec(p) for p in range(n_pages)]
            + [row(NSA_W), row(GATE_W), full(slp), seq4(ck), seq4(cv), row(KV_ROW),
               pl.BlockSpec((None, win_buf, KV_ROW), lambda b, pt: (b, 0, 0)), row(KV_ROW), full(es), full(mimp)],
            out_specs=pl.BlockSpec((dec, NSA_W), lambda b, pt: (b, 0)),
        ),
        out_shape=jax.ShapeDtypeStruct((nseq * dec, NSA_W), F32),
        compiler_params=_params(),
        name="nsa_sample",
    )(page_table, *([pool] * n_pages), q, gl, slp, ck, cv, slc_rows, win_cache, win_rows, es, mimp)


def _alibi_slopes():
    return 2.0 ** (-ALIBI_MAX_BIAS * jnp.arange(1, N_HEADS + 1, dtype=F32) / N_HEADS)


def _slope_rows(tq):
    return jnp.repeat(_alibi_slopes().reshape(N_KV_GROUPS, HEADS_PER_GROUP), tq, axis=1)[..., None]


def _importance_matrix(n_cmp):
    n = jnp.arange(n_cmp)[:, None]
    j = jnp.arange(LANES)[None, :]
    per_sel = SLC_BLOCK // CMP_STRIDE
    inside = (SLC_BLOCK - CMP_BLOCK) // CMP_STRIDE + 1
    return ((n // per_sel == j) & (n % per_sel < inside)).astype(BF16)


def _block_onehot(n_keys):
    k = jnp.arange(n_keys)[:, None]
    return (k // SLC_BLOCK == jnp.arange(LANES)[None, :]).astype(BF16)


def _position_pieces(pos):
    pos = pos[:, None]
    hi = ((pos // SLC_BLOCK) * SLC_BLOCK).astype(BF16)
    lo = (pos % SLC_BLOCK).astype(BF16)
    pieces = jnp.concatenate([hi, hi, hi, lo, lo, lo], axis=-1)
    return jnp.pad(pieces, ((0, 0), (0, HEAD_DIM - pieces.shape[-1])))


def _prompt_tables(t_len):
    slp = _slope_rows(Q_BLOCK)
    s1 = slp.astype(BF16)
    s2 = (slp - s1.astype(F32)).astype(BF16)
    s3 = (slp - s1.astype(F32) - s2.astype(F32)).astype(BF16)
    qe = jnp.concatenate([s1, s2, s3, s1, s2, s3], axis=-1)
    qe = jnp.pad(qe, ((0, 0), (0, 0), (0, HEAD_DIM - qe.shape[-1])))
    key_tab = jnp.concatenate([_block_onehot(t_len), _position_pieces(jnp.arange(t_len))], axis=-1)
    ncp = t_len // CMP_STRIDE
    cmp_tab = _position_pieces(jnp.arange(ncp) * CMP_STRIDE + (CMP_BLOCK - 1))
    return (qe, jnp.transpose(_importance_matrix(ncp))), key_tab, cmp_tab


def kernel(x_prompt, x_sample, state_conv, cache_mem_kv, cache_cmp_kv, cache_slc_kv, cache_win_kv, page_table,
           mem_prompt, norm1, norm2, norm_final, w_in_a, conv_w, conv_b, conv_ln_g, conv_ln_b, w_in_b, kv_norm,
           w_kv_shared, cmp_pos, cmp_w1, cmp_w2, w_mem_kv, w_out, ffn_w_gate, ffn_w_up, ffn_w_down, moe_router,
           moe_w_gate, moe_w_up, moe_w_down):
    bp, t_len, d = x_prompt.shape
    db, dec, _ = x_sample.shape
    depth = norm1.shape[0]
    n_a = w_in_a.shape[0]
    n_mem = mem_prompt.shape[1]
    n_p = bp * t_len
    n_s = db * dec
    n_pages = page_table.shape[1]
    past = n_pages * PAGE_SIZE
    win_buf = cache_win_kv.shape[1]
    bf = lambda a: a.astype(BF16)

    x = jnp.concatenate([x_prompt.reshape(n_p, d), x_sample.reshape(n_s, d)], axis=0)

    mem_kv_p = mem_kv_proj(mem_prompt.reshape(bp * n_mem, d), w_mem_kv)
    mem_kv_p = mem_kv_p.reshape(depth, bp, n_mem, 2 * MEM_W)
    mem_kv_s = cache_mem_kv.reshape(depth, db, n_mem, 2 * MEM_W)
    moe_wg, moe_wu, moe_wd = bf(moe_w_gate), bf(moe_w_up), bf(moe_w_down)

    conv_p, conv_s = [], []
    zero_state = jnp.zeros((bp, CONV_STATE, C_CONV), F32)
    cmp_rows = slc_rows = win_rows = None
    nsa_ctx = None

    for l in range(depth):
        if l < n_a:
            uc, qm = norm_proj(x, norm1[l], bf(w_in_a[l]), (2 * C_CONV, MEM_W))
            mix_p, st_p = conv_mixer(uc, zero_state, conv_w[l], conv_b[l], conv_ln_g[l], conv_ln_b[l],
                                     bp, t_len, Q_BLOCK)
            mix_s, st_s = conv_mixer(uc, state_conv[l], conv_w[l], conv_b[l], conv_ln_g[l], conv_ln_b[l],
                                     db, dec, dec, row0=n_p)
            conv_p.append(st_p)
            conv_s.append(st_s)
        else:
            if nsa_ctx is None:
                cmp_rows, slc_rows, win_rows = norm_proj(x, kv_norm, bf(w_kv_shared), (KV_ROW,) * 3)
                w1 = bf(cmp_w1)
                w2 = bf(cmp_w2)
                prompt_consts, key_tab, cmp_tab = _prompt_tables(t_len)
                pe = cmp_pos.reshape(2, CMP_BLOCK // CMP_PGROUP, CMP_PGROUP * HEAD_DIM)
                cka, cvt = compress_prompt(cmp_rows[:n_p], pe, w1, w2, cmp_tab, bp, t_len)
                ck_s, cv_s = compress_sample(cache_cmp_kv.reshape(-1, PAGE_SIZE, KV_ROW), page_table,
                                             cmp_rows[n_p:].reshape(db, dec, KV_ROW), pe, w1, w2)
                ska, svt, wka, wvt = kv_prep(slc_rows, win_rows, key_tab, bp, t_len)
                nsa_ctx = dict(
                    prompt=prompt_consts,
                    sample=(_slope_rows(dec), _block_onehot(past), _importance_matrix(ck_s.shape[2])),
                    pool=cache_slc_kv.reshape(-1, PAGE_SIZE, KV_ROW),
                    win_cache=cache_win_kv.reshape(db, win_buf, KV_ROW))
            w_in = w_in_b[l - n_a]
            w_in = jnp.concatenate([w_in[:, :NSA_W], w_in[:, NSA_W + GATE_W:], w_in[:, NSA_W:NSA_W + GATE_W]], axis=1)
            q, qm, gl = norm_proj(x, norm1[l], bf(w_in), (NSA_W, MEM_W, GATE_W))
            gw = 3 * HEADS_PER_GROUP
            gl_t = jnp.transpose(gl[:n_p].reshape(n_p, N_KV_GROUPS, gw), (1, 2, 0))
            mix_p = nsa_prompt(q, gl_t, cka, cvt, ska, svt, wka, wvt, nsa_ctx["prompt"], bp, t_len)
            mix_s = nsa_sample(q, gl, ck_s, cv_s, nsa_ctx["pool"], page_table, slc_rows, nsa_ctx["win_cache"],
                               win_rows, nsa_ctx["sample"], n_p // dec, dec)
        mo_p = mem_attend(qm, mem_kv_p, l, bp, t_len, 512)
        mo_s = mem_attend(qm, mem_kv_s, l, db, dec, dec, spb=MEM_SEQS_PER_STEP, row0=n_p)
        if l % 2 == 0:
            xn, h2 = outproj_norm(x, mix_p, mix_s, mo_p, mo_s, bf(w_out[l]), norm2[l])
            x = ffn_dense(h2, xn, bf(ffn_w_gate[l // 2]), bf(ffn_w_up[l // 2]), bf(ffn_w_down[l // 2]))
        else:
            xn, h2, gate = outproj_norm(x, mix_p, mix_s, mo_p, mo_s, bf(w_out[l]), norm2[l],
                                        router=moe_router[l // 2])
            x = moe_ffn(h2, xn, gate, moe_wg, moe_wu, moe_wd, l // 2)

    y_p = final_norm(x, norm_final, 0, n_p)
    y_s = final_norm(x, norm_final, n_p, n_s)
    kv5 = lambda a, b, t: a.reshape(b, t, 2, N_KV_GROUPS, HEAD_DIM)
    keep_p = min(WINDOW, t_len)
    win_p = kv5(win_rows[:n_p], bp, t_len)[:, t_len - keep_p:]
    win_all = jnp.concatenate([cache_win_kv, kv5(win_rows[n_p:], db, dec)], axis=1)
    keep_s = min(WINDOW, past + dec)
    return (y_p.reshape(bp, t_len, d), y_s.reshape(db, dec, d),
            jnp.stack(conv_p), jnp.stack(conv_s),
            mem_kv_p.reshape(depth, bp, n_mem, 2, MEM_HEADS, MEM_HEAD_DIM),
            kv5(cmp_rows[:n_p], bp, t_len), kv5(cmp_rows[n_p:], db, dec),
            kv5(slc_rows[:n_p], bp, t_len), kv5(slc_rows[n_p:], db, dec),
            win_p, win_all[:, win_all.shape[1] - keep_s:])
```

```python
import functools

import jax
import jax.numpy as jnp
from jax import lax
from jax.experimental import pallas as pl
from jax.experimental.pallas import tpu as pltpu

F32 = jnp.float32
BF16 = jnp.bfloat16

HEAD_DIM = 64
N_KV_GROUPS = 3
HEADS_PER_GROUP = 4
N_HEADS = N_KV_GROUPS * HEADS_PER_GROUP
NSA_W = N_HEADS * HEAD_DIM
GATE_W = 3 * N_HEADS
CMP_STRIDE = 16
CMP_BLOCK = 32
SLC_BLOCK = 64
SLC_SHIFT = 6
N_SEL = 16
WINDOW = 512
Q_BLOCK = 128
ALIBI_MAX_BIAS = 8.0
MEM_HEADS = 4
MEM_HEAD_DIM = 64
MEM_W = MEM_HEADS * MEM_HEAD_DIM
MEM_SEQS_PER_STEP = 8
C_CONV = 768
CONV_K = 31
CONV_STATE = CONV_K - 1
N_EXPERTS = 8
EPS = 1e-6
PAGE_SIZE = 128
KV_ROW = 2 * N_KV_GROUPS * HEAD_DIM
CHUNK_ROW = CMP_STRIDE * KV_ROW
LANES = 128
SUBLANES = 8
NEG = -1e30
M_INIT = -1e29
VMEM_LIMIT = 56 * 1024 * 1024


def _dot(a, b):
    return jnp.dot(a, b, preferred_element_type=F32)


def _dot_nt(a, b):
    return lax.dot_general(a, b, (((1,), (1,)), ((), ())), preferred_element_type=F32)


def _split_hi_lo(x):
    hi = x.astype(BF16)
    lo = (x - hi.astype(F32)).astype(BF16)
    return hi, lo


def _rms(x, g):
    return x * lax.rsqrt(jnp.mean(x * x, axis=-1, keepdims=True) + EPS) * g


def _params(**kw):
    return pltpu.CompilerParams(vmem_limit_bytes=VMEM_LIMIT, **kw)


def _log2(n):
    assert n > 0 and n & (n - 1) == 0, n
    return n.bit_length() - 1


def _row_tile(n, pref):
    tm = pref
    while n % tm:
        tm //= 2
    assert tm >= 8, (n, pref)
    return tm


def _norm_proj_kernel(x_ref, g_ref, w_ref, *o_refs, splits):
    h = _rms(x_ref[...], g_ref[...])
    y = _dot(h.astype(BF16), w_ref[...])
    off = 0
    for o_ref, n in zip(o_refs, splits):
        o_ref[...] = y[:, off:off + n]
        off += n


def norm_proj(x, g, w, splits):
    n, d = x.shape
    tm = _row_tile(n, 512)
    return pl.pallas_call(
        functools.partial(_norm_proj_kernel, splits=splits),
        grid=(n // tm,),
        in_specs=[pl.BlockSpec((tm, d), lambda i: (i, 0)),
                  pl.BlockSpec((1, d), lambda i: (0, 0)),
                  pl.BlockSpec(w.shape, lambda i: (0, 0))],
        out_specs=[pl.BlockSpec((tm, s), lambda i: (i, 0)) for s in splits],
        out_shape=[jax.ShapeDtypeStruct((n, s), F32) for s in splits],
        compiler_params=_params(),
        name="norm_proj",
    )(x, g.reshape(1, d), w)


def _mem_kv_kernel(x_ref, w_ref, o_ref):
    o_ref[...] = _dot(x_ref[...].astype(BF16), w_ref[...].astype(BF16))


def mem_kv_proj(x, w):
    depth, d, e = w.shape
    n = x.shape[0]
    return pl.pallas_call(
        _mem_kv_kernel,
        grid=(depth,),
        in_specs=[pl.BlockSpec((n, d), lambda l: (0, 0)),
                  pl.BlockSpec((None, d, e), lambda l: (l, 0, 0))],
        out_specs=pl.BlockSpec((None, n, e), lambda l: (l, 0, 0)),
        out_shape=jax.ShapeDtypeStruct((depth, n, e), F32),
        compiler_params=_params(),
        name="mem_kv_proj",
    )(x, w)


CONV_PAD = 32


def _conv_kernel(u_ref, st_ref, w_ref, b_ref, lg_ref, lb_ref, mix_ref, ns_ref, vp_ref, *, tt, rc):
    t = pl.program_id(1)
    lo = CONV_PAD - CONV_STATE

    @pl.when(t == 0)
    def _():
        vp_ref[lo:CONV_PAD, :] = st_ref[...]

    u = u_ref[...]
    vp_ref[CONV_PAD:CONV_PAD + tt, :] = u[:, :C_CONV] * jax.nn.sigmoid(u[:, C_CONV:])
    vp_ref[CONV_PAD + tt:CONV_PAD + tt + SUBLANES, :] = jnp.zeros((SUBLANES, C_CONV), F32)
    for c in range(tt // rc):
        acc = None
        for b in range(SUBLANES):
            part = None
            for k in range(CONV_K):
                a, phase = divmod(lo + k, SUBLANES)
                if phase == b:
                    term = w_ref[k:k + 1, :] * vp_ref[pl.ds(c * rc + SUBLANES * a, rc + SUBLANES), :]
                    part = term if part is None else part + term
            acc = part[b:b + rc] if acc is None else acc + part[b:b + rc]
        h = acc + b_ref[...]
        mu = jnp.mean(h, axis=-1, keepdims=True)
        hc = h - mu
        var = jnp.mean(hc * hc, axis=-1, keepdims=True)
        y = hc * lax.rsqrt(var + EPS) * lg_ref[...] + lb_ref[...]
        mix_ref[c * rc:(c + 1) * rc, :] = y * jax.nn.sigmoid(y)
    new_state = vp_ref[pl.ds(lo + tt, CONV_STATE), :]
    ns_ref[...] = new_state
    vp_ref[lo:CONV_PAD, :] = new_state


def conv_mixer(u, state, w, b, lg, lb, nseq, t_len, tt, row0=0):
    rc = min(tt, 32)
    nt = t_len // tt
    blk0 = row0 // tt
    vec = lambda a: a.reshape(1, C_CONV)
    return pl.pallas_call(
        functools.partial(_conv_kernel, tt=tt, rc=rc),
        grid=(nseq, nt),
        in_specs=[pl.BlockSpec((tt, 2 * C_CONV), lambda s, t: (blk0 + s * nt + t, 0)),
                  pl.BlockSpec((None, CONV_STATE, C_CONV), lambda s, t: (s, 0, 0)),
                  pl.BlockSpec((CONV_K, C_CONV), lambda s, t: (0, 0)),
                  pl.BlockSpec((1, C_CONV), lambda s, t: (0, 0)),
                  pl.BlockSpec((1, C_CONV), lambda s, t: (0, 0)),
                  pl.BlockSpec((1, C_CONV), lambda s, t: (0, 0))],
        out_specs=[pl.BlockSpec((tt, C_CONV), lambda s, t: (s * nt + t, 0)),
                   pl.BlockSpec((None, CONV_STATE, C_CONV), lambda s, t: (s, 0, 0))],
        out_shape=[jax.ShapeDtypeStruct((nseq * t_len, C_CONV), F32),
                   jax.ShapeDtypeStruct((nseq, CONV_STATE, C_CONV), F32)],
        scratch_shapes=[pltpu.VMEM((CONV_PAD + tt + SUBLANES, C_CONV), F32)],
        compiler_params=_params(),
        name="conv_mixer",
    )(u, state, w, vec(b), vec(lg), vec(lb))


def _mem_attn_kernel(q_ref, kv_ref, o_ref, *, spb, tq):
    n_mem = kv_ref.shape[1]
    q = q_ref[...]
    kv = kv_ref[...].reshape(spb * n_mem, 2 * MEM_W).astype(BF16)
    own = None
    if spb > 1:
        q_seq = lax.broadcasted_iota(jnp.int32, (spb * tq, 1), 0) >> _log2(tq)
        k_seq = lax.broadcasted_iota(jnp.int32, (1, spb * n_mem), 1) >> _log2(n_mem)
        own = q_seq == k_seq
    for h in range(MEM_HEADS):
        c = h * MEM_HEAD_DIM
        qh = q[:, c:c + MEM_HEAD_DIM].astype(BF16)
        s = _dot_nt(qh, kv[:, c:c + MEM_HEAD_DIM]) * (MEM_HEAD_DIM ** -0.5)
        if own is not None:
            s = jnp.where(own, s, NEG)
        e = jnp.exp(s - jnp.max(s, axis=-1, keepdims=True))
        o = _dot(e.astype(BF16), kv[:, MEM_W + c:MEM_W + c + MEM_HEAD_DIM])
        o_ref[:, c:c + MEM_HEAD_DIM] = o / jnp.sum(e, axis=-1, keepdims=True)


def mem_attend(qm, kv, layer, nseq, t_len, tq, spb=1, row0=0):
    nt = t_len // tq
    assert spb == 1 or nt == 1
    n_mem = kv.shape[2]
    rows = spb * tq
    blk0 = row0 // rows
    return pl.pallas_call(
        functools.partial(_mem_attn_kernel, spb=spb, tq=tq),
        grid=(nseq // spb, nt),
        in_specs=[pl.BlockSpec((rows, MEM_W), lambda s, t: (blk0 + s * nt + t, 0)),
                  pl.BlockSpec((None, spb, n_mem, 2 * MEM_W), lambda s, t: (layer, s, 0, 0))],
        out_specs=pl.BlockSpec((rows, MEM_W), lambda s, t: (s * nt + t, 0)),
        out_shape=jax.ShapeDtypeStruct((nseq * t_len, MEM_W), F32),
        compiler_params=_params(),
        name="mem_attend",
    )(qm, kv)


def _outproj_kernel(x_ref, mixp_ref, mixs_ref, mop_ref, mos_ref, w_ref, g_ref, *rest, moe, prompt_tiles):
    if moe:
        r_ref, xn_ref, h_ref, gate_ref = rest
    else:
        xn_ref, h_ref = rest
    is_prompt = pl.program_id(0) < prompt_tiles
    mix = jnp.where(is_prompt, mixp_ref[...].astype(BF16), mixs_ref[...].astype(BF16))
    mo = jnp.where(is_prompt, mop_ref[...].astype(BF16), mos_ref[...].astype(BF16))
    d_mix = mix.shape[1]
    y = _dot(mix, w_ref[0:d_mix, :]) + _dot(mo, w_ref[d_mix:, :])
    xn = x_ref[...] + y
    xn_ref[...] = xn
    h = _rms(xn, g_ref[...])
    h_ref[...] = h.astype(BF16)
    if moe:
        h_hi, h_lo = _split_hi_lo(h)
        r_hi, r_lo = _split_hi_lo(r_ref[...])
        logits = _dot(h_hi, r_hi) + _dot(h_lo, r_hi) + _dot(h_hi, r_lo)
        lane = lax.broadcasted_iota(jnp.int32, logits.shape, 1)
        logits = jnp.where(lane < N_EXPERTS, logits, -jnp.inf)
        m1 = jnp.max(logits, axis=-1, keepdims=True)
        i1 = jnp.min(jnp.where(logits == m1, lane, LANES), axis=-1, keepdims=True)
        rest_l = jnp.where(lane == i1, -jnp.inf, logits)
        m2 = jnp.max(rest_l, axis=-1, keepdims=True)
        i2 = jnp.min(jnp.where(rest_l == m2, lane, LANES), axis=-1, keepdims=True)
        e2 = jnp.exp(m2 - m1)
        den = 1.0 + e2
        gate_ref[...] = jnp.where(lane == i1, 1.0 / den, 0.0) + jnp.where(lane == i2, e2 / den, 0.0)


def outproj_norm(x, mix_p, mix_s, mo_p, mo_s, w, g, router=None):
    n, d = x.shape
    tm = _row_tile(n, 512)
    assert mix_p.shape[0] % tm == 0 and mix_s.shape[0] % tm == 0
    pt = mix_p.shape[0] // tm
    moe = router is not None
    row = lambda c: pl.BlockSpec((tm, c), lambda i: (i, 0))
    row_p = lambda c: pl.BlockSpec((tm, c), lambda i: (jnp.minimum(i, pt - 1), 0))
    row_s = lambda c: pl.BlockSpec((tm, c), lambda i: (jnp.maximum(i - pt, 0), 0))
    full = lambda a: pl.BlockSpec(a.shape, lambda i: (0, 0))
    ins = [x, mix_p, mix_s, mo_p, mo_s, w, g.reshape(1, d)]
    in_specs = [row(d), row_p(mix_p.shape[1]), row_s(mix_s.shape[1]), row_p(mo_p.shape[1]), row_s(mo_s.shape[1]),
                full(w), pl.BlockSpec((1, d), lambda i: (0, 0))]
    out_specs = [row(d), row(d)]
    out_shape = [jax.ShapeDtypeStruct((n, d), F32), jax.ShapeDtypeStruct((n, d), BF16)]
    if moe:
        rp = jnp.pad(router, ((0, 0), (0, LANES - router.shape[1])))
        ins.append(rp)
        in_specs.append(full(rp))
        out_specs.append(row(LANES))
        out_shape.append(jax.ShapeDtypeStruct((n, LANES), F32))
    return pl.pallas_call(
        functools.partial(_outproj_kernel, moe=moe, prompt_tiles=pt),
        grid=(n // tm,),
        in_specs=in_specs, out_specs=out_specs, out_shape=out_shape,
        compiler_params=_params(),
        name="outproj_norm",
    )(*ins)


def _ffn_kernel(h_ref, x_ref, wg_ref, wu_ref, wd_ref, o_ref):
    @pl.when(pl.program_id(1) == 0)
    def _():
        o_ref[...] = x_ref[...]

    h = h_ref[...]
    a = _dot(h, wg_ref[...])
    u = _dot(h, wu_ref[...])
    act = (a * jax.nn.sigmoid(a) * u).astype(BF16)
    o_ref[...] += _dot(act, wd_ref[...])


def ffn_dense(h, x, wg, wu, wd, nf=2):
    n, d = x.shape
    tm = _row_tile(n, 512)
    tf = wg.shape[1] // nf
    return pl.pallas_call(
        _ffn_kernel,
        grid=(n // tm, nf),
        in_specs=[pl.BlockSpec((tm, d), lambda i, j: (i, 0)),
                  pl.BlockSpec((tm, d), lambda i, j: (i, 0)),
                  pl.BlockSpec((d, tf), lambda i, j: (0, j)),
                  pl.BlockSpec((d, tf), lambda i, j: (0, j)),
                  pl.BlockSpec((tf, d), lambda i, j: (j, 0))],
        out_specs=pl.BlockSpec((tm, d), lambda i, j: (i, 0)),
        out_shape=jax.ShapeDtypeStruct((n, d), F32),
        compiler_params=_params(),
        name="ffn_dense",
    )(h, x, wg, wu, wd)


MOE_RB = 128
MOE_RB_SHIFT = 7


def _moe_kernel(cnt_ref, h_ref, x_ref, gate_ref, gt_ref, wg_ref, wu_ref, wd_ref, o_ref,
                xs_ref, y_ref, srow_ref, scol_ref):
    i, e, j = pl.program_id(0), pl.program_id(1), pl.program_id(2)
    tm = h_ref.shape[0]
    nblk = (cnt_ref[i, e] + (MOE_RB - 1)) >> MOE_RB_SHIFT

    @pl.when((e == 0) & (j == 0))
    def _():
        o_ref[...] = x_ref[...]
        r = lax.broadcasted_iota(jnp.int32, (tm, tm), 0)
        c = lax.broadcasted_iota(jnp.int32, (tm, tm), 1)
        srow_ref[...] = _dot((gt_ref[...] != 0.0).astype(BF16), (r < c).astype(BF16))
        scol_ref[...] = _dot((c < r).astype(BF16), (gate_ref[...] != 0.0).astype(BF16))

    def block_rows(k):
        return pl.ds(pl.multiple_of(k * MOE_RB, MOE_RB), MOE_RB)

    def pick_row(a):
        return jnp.sum(jnp.where(lax.broadcasted_iota(jnp.int32, a.shape, 0) == e, a, 0.0), axis=0, keepdims=True)

    def pick_col(a):
        return jnp.sum(jnp.where(lax.broadcasted_iota(jnp.int32, a.shape, 1) == e, a, 0.0), axis=1, keepdims=True)

    @pl.when(j == 0)
    def _():
        mrow = pick_row(gt_ref[...]) != 0.0
        slot = pick_row(srow_ref[...])

        def pack(k, _):
            rid = (k * MOE_RB + lax.broadcasted_iota(jnp.int32, (MOE_RB, 1), 0)).astype(F32)
            onehot = (mrow & (slot == rid)).astype(BF16)
            xs_ref[block_rows(k), :] = _dot(onehot, h_ref[...]).astype(BF16)
            y_ref[block_rows(k), :] = jnp.zeros((MOE_RB, y_ref.shape[1]), F32)
            return 0

        lax.fori_loop(0, nblk, pack, 0)

    def expert(rows):
        xs = xs_ref[rows, :]
        a = _dot(xs, wg_ref[...])
        u = _dot(xs, wu_ref[...])
        act = (a * jax.nn.sigmoid(a) * u).astype(BF16)
        y_ref[rows, :] += _dot(act, wd_ref[...])

    def expert_pair(k2, _):
        expert(pl.ds(pl.multiple_of(k2 * (2 * MOE_RB), 2 * MOE_RB), 2 * MOE_RB))
        return 0

    if tm >= 2 * MOE_RB:
        lax.fori_loop(0, nblk >> 1, expert_pair, 0)

        @pl.when((nblk & 1) == 1)
        def _():
            expert(block_rows(nblk - 1))
    else:
        lax.fori_loop(0, nblk, lambda k, _: (expert(block_rows(k)), 0)[1], 0)

    @pl.when(j == pl.num_programs(2) - 1)
    def _():
        ge = pick_col(gate_ref[...])
        mcol = ge != 0.0
        slot = pick_col(scol_ref[...])

        def unpack(k, _):
            cid = (k * MOE_RB + lax.broadcasted_iota(jnp.int32, (1, MOE_RB), 1)).astype(F32)
            onehot_t = (mcol & (slot == cid)).astype(BF16)
            y_hi, y_lo = _split_hi_lo(y_ref[block_rows(k), :])
            o_ref[...] += ge * _dot(jnp.concatenate([onehot_t, onehot_t], axis=1),
                                    jnp.concatenate([y_hi, y_lo], axis=0))
            return 0

        lax.fori_loop(0, nblk, unpack, 0)


def moe_ffn(h, x, gate, wg, wu, wd, layer):
    n, d = x.shape
    tm = _row_tile(n, 1024)
    nt = n // tm
    n_exp, d_exp = wg.shape[1], wg.shape[3]
    tf = next(t for t in (896, 512, 256, 128) if d_exp % t == 0)
    nf = d_exp // tf
    routed = gate[:, :n_exp] != 0.0
    counts = jnp.sum(routed.reshape(nt, tm, n_exp), axis=1, dtype=jnp.int32)
    gate_t = jnp.transpose(gate[:, :n_exp])
    return pl.pallas_call(
        _moe_kernel,
        grid_spec=pltpu.PrefetchScalarGridSpec(
            num_scalar_prefetch=1,
            grid=(nt, n_exp, nf),
            in_specs=[pl.BlockSpec((tm, d), lambda i, e, j, c: (i, 0)),
                      pl.BlockSpec((tm, d), lambda i, e, j, c: (i, 0)),
                      pl.BlockSpec((tm, LANES), lambda i, e, j, c: (i, 0)),
                      pl.BlockSpec((n_exp, tm), lambda i, e, j, c: (0, i)),
                      pl.BlockSpec((None, None, d, tf), lambda i, e, j, c: (layer, e, 0, j)),
                      pl.BlockSpec((None, None, d, tf), lambda i, e, j, c: (layer, e, 0, j)),
                      pl.BlockSpec((None, None, tf, d), lambda i, e, j, c: (layer, e, j, 0))],
            out_specs=pl.BlockSpec((tm, d), lambda i, e, j, c: (i, 0)),
            scratch_shapes=[pltpu.VMEM((tm, d), BF16), pltpu.VMEM((tm, d), F32),
                            pltpu.VMEM((n_exp, tm), F32), pltpu.VMEM((tm, LANES), F32)],
        ),
        out_shape=jax.ShapeDtypeStruct((n, d), F32),
        compiler_params=_params(),
        name="moe_ffn",
    )(counts, h, x, gate, gate_t, wg, wu, wd)


def _final_norm_kernel(x_ref, g_ref, o_ref):
    o_ref[...] = _rms(x_ref[...], g_ref[...])


def final_norm(x, g, row0, n):
    d = x.shape[1]
    tm = _row_tile(n, 1024)
    assert row0 % tm == 0
    blk0 = row0 // tm
    return pl.pallas_call(
        _final_norm_kernel,
        grid=(n // tm,),
        in_specs=[pl.BlockSpec((tm, d), lambda i: (blk0 + i, 0)), pl.BlockSpec((1, d), lambda i: (0, 0))],
        out_specs=pl.BlockSpec((tm, d), lambda i: (i, 0)),
        out_shape=jax.ShapeDtypeStruct((n, d), F32),
        compiler_params=_params(),
        name="final_norm",
    )(x, g.reshape(1, d))


SLC_TILE = 512
WIN_TILE = 128


def _kv_prep_kernel(s_ref, w_ref, tab_ref, ska_ref, svt_ref, wka_ref, wvt_ref):
    tab = tab_ref[...]
    onehot, pieces = tab[:, :LANES], tab[:, LANES:]
    s = s_ref[...]
    w = w_ref[...]
    for g in range(N_KV_GROUPS):
        kc = slice(g * HEAD_DIM, (g + 1) * HEAD_DIM)
        vc = slice((N_KV_GROUPS + g) * HEAD_DIM, (N_KV_GROUPS + g + 1) * HEAD_DIM)
        ska_ref[g] = jnp.concatenate([onehot, s[:, kc].astype(BF16), pieces], axis=1)
        svt_ref[g] = s[:, vc].T.astype(BF16)
        wka_ref[g] = jnp.concatenate([w[:, kc].astype(BF16), pieces], axis=1)
        wv_t = w[:, vc].T.astype(BF16)
        for j in range(SLC_TILE // WIN_TILE):
            wvt_ref[g, j] = wv_t[:, j * WIN_TILE:(j + 1) * WIN_TILE]


def kv_prep(slc_rows, win_rows, tab, nb, t_len):
    tm = SLC_TILE
    nt = t_len // tm
    wpt = tm // WIN_TILE
    g3 = N_KV_GROUPS
    return pl.pallas_call(
        _kv_prep_kernel,
        grid=(nb, nt),
        in_specs=[pl.BlockSpec((tm, KV_ROW), lambda b, t: (b * nt + t, 0)),
                  pl.BlockSpec((tm, KV_ROW), lambda b, t: (b * nt + t, 0)),
                  pl.BlockSpec((tm, tab.shape[1]), lambda b, t: (t, 0))],
        out_specs=[pl.BlockSpec((None, g3, tm, 2 * LANES), lambda b, t: (b, 0, t, 0)),
                   pl.BlockSpec((None, g3, None, HEAD_DIM, tm), lambda b, t: (b, 0, t, 0, 0)),
                   pl.BlockSpec((None, g3, tm, LANES), lambda b, t: (b, 0, t, 0)),
                   pl.BlockSpec((None, g3, wpt, HEAD_DIM, WIN_TILE), lambda b, t: (b, 0, t, 0, 0))],
        out_shape=[jax.ShapeDtypeStruct((nb, g3, t_len, 2 * LANES), BF16),
                   jax.ShapeDtypeStruct((nb, g3, nt, HEAD_DIM, tm), BF16),
                   jax.ShapeDtypeStruct((nb, g3, t_len, LANES), BF16),
                   jax.ShapeDtypeStruct((nb, g3, t_len // WIN_TILE, HEAD_DIM, WIN_TILE), BF16)],
        compiler_params=_params(),
        name="kv_prep",
    )(slc_rows, win_rows, tab)


CMP_PGROUP = 4


def _cmp_first_layer(get_cols, kv, pe_ref, w1_ref):
    acc_a = None
    acc_b = None
    kw = CMP_PGROUP * HEAD_DIM
    for pg in range(CMP_STRIDE // CMP_PGROUP):
        xs = jnp.concatenate(
            [jnp.concatenate([get_cols(p * KV_ROW + (kv * N_KV_GROUPS + g) * HEAD_DIM)
                              for p in range(pg * CMP_PGROUP, (pg + 1) * CMP_PGROUP)], axis=1)
             for g in range(N_KV_GROUPS)], axis=0)
        qg = pg + CMP_STRIDE // CMP_PGROUP
        da = _dot((xs + pe_ref[kv, pg:pg + 1, :]).astype(BF16), w1_ref[kv, pg * kw:(pg + 1) * kw, :])
        db = _dot((xs + pe_ref[kv, qg:qg + 1, :]).astype(BF16), w1_ref[kv, qg * kw:(qg + 1) * kw, :])
        acc_a = da if acc_a is None else acc_a + da
        acc_b = db if acc_b is None else acc_b + db
    return acc_a, acc_b


def _cmp_second_layer(kv, nch, a_ref, b_ref, w2_ref, emit):
    for g in range(N_KV_GROUPS):
        hid = a_ref[g, 0:nch, :] + b_ref[g, pl.ds(1, nch), :]
        hid = hid * jax.nn.sigmoid(hid)
        emit(kv, g, _dot(hid.astype(BF16), w2_ref[kv]))


def _compress_prompt_kernel(x_ref, pe_ref, w1_ref, w2_ref, cke_ref, cka_ref, cvt_ref, a_ref, b_ref, *, nch, tc):
    def emit(kv, g, out):
        if kv == 0:
            cka_ref[g] = jnp.concatenate([out.astype(BF16), cke_ref[...]], axis=1)
        else:
            cvt_ref[g] = out.T.astype(BF16)

    for kv in range(2):
        for ct in range(nch // tc):
            rows = slice(ct * tc, (ct + 1) * tc)
            acc_a, acc_b = _cmp_first_layer(lambda c0: x_ref[rows, c0:c0 + HEAD_DIM], kv, pe_ref, w1_ref)
            for g in range(N_KV_GROUPS):
                a_ref[g, rows, :] = acc_a[g * tc:(g + 1) * tc]
                b_ref[g, rows, :] = acc_b[g * tc:(g + 1) * tc]
        b_ref[:, nch:nch + 8, :] = jnp.zeros((N_KV_GROUPS, 8, b_ref.shape[2]), F32)
        _cmp_second_layer(kv, nch, a_ref, b_ref, w2_ref, emit)


def compress_prompt(cmp_rows, pe, w1, w2, cke, nb, t_len):
    nch = t_len // CMP_STRIDE
    tc = min(nch, 128)
    hid = w1.shape[2]
    xc = cmp_rows.reshape(nb, nch, CHUNK_ROW)
    full = lambda a: pl.BlockSpec(a.shape, lambda b: (0,) * a.ndim)
    return pl.pallas_call(
        functools.partial(_compress_prompt_kernel, nch=nch, tc=tc),
        grid=(nb,),
        in_specs=[pl.BlockSpec((None, nch, CHUNK_ROW), lambda b: (b, 0, 0)), full(pe), full(w1), full(w2), full(cke)],
        out_specs=[pl.BlockSpec((None, N_KV_GROUPS, nch, LANES), lambda b: (b, 0, 0, 0)),
                   pl.BlockSpec((None, N_KV_GROUPS, HEAD_DIM, nch), lambda b: (b, 0, 0, 0))],
        out_shape=[jax.ShapeDtypeStruct((nb, N_KV_GROUPS, nch, LANES), BF16),
                   jax.ShapeDtypeStruct((nb, N_KV_GROUPS, HEAD_DIM, nch), BF16)],
        scratch_shapes=[pltpu.VMEM((N_KV_GROUPS, nch + 8, hid), F32)] * 2,
        compiler_params=_params(),
        name="compress_prompt",
    )(xc, pe, w1, w2, cke)


def _compress_sample_kernel(pt_ref, *refs, n_pages):
    page_refs = refs[:n_pages]
    new_ref, pe_ref, w1_ref, w2_ref, ck_ref, cv_ref, a_ref, b_ref = refs[n_pages:]
    nch = n_pages * (PAGE_SIZE // CMP_STRIDE)
    for kv in range(2):
        acc_a, acc_b = _cmp_first_layer(
            lambda c0: jnp.concatenate([pr[:, c0:c0 + HEAD_DIM] for pr in page_refs], axis=0), kv, pe_ref, w1_ref)
        _, new_b = _cmp_first_layer(
            lambda c0: jnp.broadcast_to(new_ref[:, c0:c0 + HEAD_DIM], (8, HEAD_DIM)), kv, pe_ref, w1_ref)
        for g in range(N_KV_GROUPS):
            a_ref[g, 0:nch, :] = acc_a[g * nch:(g + 1) * nch]
            b_ref[g, 0:nch, :] = acc_b[g * nch:(g + 1) * nch]
            b_ref[g, nch:nch + 8, :] = new_b[g * 8:(g + 1) * 8]

        def emit(kv, g, out):
            (ck_ref, cv_ref)[kv][g] = out.astype(BF16)

        _cmp_second_layer(kv, nch, a_ref, b_ref, w2_ref, emit)


def compress_sample(pool, page_table, cmp_new, pe, w1, w2):
    nseq, n_pages = page_table.shape
    dec = cmp_new.shape[1]
    cpp = PAGE_SIZE // CMP_STRIDE
    nch = n_pages * cpp
    hid = w1.shape[2]
    pool_c = pool.reshape(pool.shape[0], cpp, CHUNK_ROW)
    new_c = jnp.pad(cmp_new.reshape(nseq, 1, dec * KV_ROW), ((0, 0), (0, 0), (0, CHUNK_ROW - dec * KV_ROW)))
    page_spec = lambda p: pl.BlockSpec((None, cpp, CHUNK_ROW), lambda b, pt: (pt[b, p], 0, 0))
    full = lambda a: pl.BlockSpec(a.shape, lambda b, pt: (0,) * a.ndim)
    o_spec = pl.BlockSpec((None, N_KV_GROUPS, nch, HEAD_DIM), lambda b, pt: (b, 0, 0, 0))
    o_shape = jax.ShapeDtypeStruct((nseq, N_KV_GROUPS, nch, HEAD_DIM), BF16)
    return pl.pallas_call(
        functools.partial(_compress_sample_kernel, n_pages=n_pages),
        grid_spec=pltpu.PrefetchScalarGridSpec(
            num_scalar_prefetch=1,
            grid=(nseq,),
            in_specs=[page_spec(p) for p in range(n_pages)]
            + [pl.BlockSpec((None, 1, CHUNK_ROW), lambda b, pt: (b, 0, 0)), full(pe), full(w1), full(w2)],
            out_specs=[o_spec, o_spec],
            scratch_shapes=[pltpu.VMEM((N_KV_GROUPS, nch + 8, hid), F32)] * 2,
        ),
        out_shape=[o_shape, o_shape],
        compiler_params=_params(),
        name="compress_sample",
    )(page_table, *([pool_c] * n_pages), new_c, pe, w1, w2)


def _softmax_parts(s, axis=-1):
    m = jnp.maximum(jnp.max(s, axis=axis, keepdims=True), M_INIT)
    e = jnp.exp(s - m)
    return e, jnp.sum(e, axis=axis, keepdims=True)


def _masked_softmax(s, axis=-1):
    e, d = _softmax_parts(s, axis)
    return e * (1.0 / jnp.where(d > 0, d, 1.0))


def _select_blocks(imp, cur, axis=-1):
    axis = axis % imp.ndim
    j = lax.broadcasted_iota(jnp.int32, imp.shape, axis)
    valid = j <= cur
    forced = valid & ((j == 0) | (j == cur) | (j == cur - 1))
    v = jnp.where(forced, jnp.inf, jnp.where(valid, imp, -jnp.inf))
    sel = jnp.zeros(imp.shape, F32)
    for _ in range(N_SEL):
        m = jnp.max(v, axis=axis, keepdims=True)
        jm = jnp.min(jnp.where(v == m, j, LANES), axis=axis, keepdims=True)
        pick = j == jm
        v = jnp.where(pick, -jnp.inf, v)
        sel = jnp.where(pick, 1.0, sel)
    return (sel > 0.0) & valid


def _stack_heads(q):
    return jnp.concatenate([q[:, r * HEAD_DIM:(r + 1) * HEAD_DIM] for r in range(HEADS_PER_GROUP)], axis=0)


def _importance(p_c, tq, mimp_ref):
    pcs = p_c[0:tq]
    for r in range(1, HEADS_PER_GROUP):
        pcs = pcs + p_c[r * tq:(r + 1) * tq]
    hi, lo = _split_hi_lo(pcs)
    return _dot(hi, mimp_ref[...]) + _dot(lo, mimp_ref[...])


def _gated_merge(gsig, g, o_c, o_s, o_w, tq, store):
    for r in range(HEADS_PER_GROUP):
        c = 3 * (g * HEADS_PER_GROUP + r)
        rows = slice(r * tq, (r + 1) * tq)
        store(r, gsig[:, c:c + 1] * o_c[rows] + gsig[:, c + 1:c + 2] * o_s[rows] + gsig[:, c + 2:c + 3] * o_w[rows])


def _nsa_prompt_kernel(q_ref, glt_ref, qe_ref, cka_ref, cvt_ref, ska_ref, svt_ref, wka_ref, wvt_ref,
                       mimpt_ref, o_ref, *, t_len):
    i = pl.program_id(2)
    tq = Q_BLOCK
    rows = HEADS_PER_GROUP * tq
    ncp = t_len // CMP_STRIDE
    s0 = i * tq
    q4 = (_stack_heads(q_ref[...]) * (HEAD_DIM ** -0.5)).astype(BF16)
    qb = jnp.concatenate([q4, qe_ref[...]], axis=1)
    qpos = s0 + (lax.broadcasted_iota(jnp.int32, (1, rows), 1) & (tq - 1))

    cpos = lax.broadcasted_iota(jnp.int32, (ncp, 1), 0) * CMP_STRIDE + (CMP_BLOCK - 1)
    mask_c = cpos <= qpos
    p_c = _masked_softmax(jnp.where(mask_c, _dot_nt(cka_ref[...], qb), NEG), axis=0)
    o_c = _dot(cvt_ref[...], p_c.astype(BF16))

    nwt = (WINDOW + tq) // WIN_TILE
    wb = jnp.maximum(i - WINDOW // WIN_TILE, 0)
    kw = pl.ds(pl.multiple_of(wb * WIN_TILE, WIN_TILE), nwt * WIN_TILE)
    dist = qpos - (wb * WIN_TILE + lax.broadcasted_iota(jnp.int32, (nwt * WIN_TILE, 1), 0))
    mask_w = (dist >= 0) & (dist < WINDOW)
    e_w, d_w = _softmax_parts(jnp.where(mask_w, _dot_nt(wka_ref[kw, :], qb), NEG), axis=0)
    e_w = e_w.astype(BF16)
    o_w = _dot(wvt_ref[wb], e_w[0:WIN_TILE])
    for w in range(1, nwt):
        o_w = o_w + _dot(wvt_ref[wb + w], e_w[w * WIN_TILE:(w + 1) * WIN_TILE])
    o_w = o_w / d_w

    pcs = p_c[:, 0:tq]
    for r in range(1, HEADS_PER_GROUP):
        pcs = pcs + p_c[:, r * tq:(r + 1) * tq]
    hi, lo = _split_hi_lo(pcs)
    imp = _dot(mimpt_ref[...], hi) + _dot(mimpt_ref[...], lo)
    sel = _select_blocks(imp, qpos[:, 0:tq] >> SLC_SHIFT, axis=0)
    selbias = jnp.where(sel, 0.0, NEG).T.astype(BF16)
    qa = jnp.concatenate([jnp.concatenate([selbias] * HEADS_PER_GROUP, axis=0), qb], axis=1)

    def scores(kt):
        return _dot_nt(ska_ref[pl.ds(pl.multiple_of(kt * SLC_TILE, SLC_TILE), SLC_TILE), :], qa)

    def flash_update(stats, s, vt):
        m, l, acc = stats
        m_new = jnp.maximum(m, jnp.max(s, axis=0, keepdims=True))
        a = jnp.exp(m - m_new)
        p = jnp.exp(s - m_new)
        return m_new, a * l + jnp.sum(p, axis=0, keepdims=True), a * acc + _dot(vt, p.astype(BF16))

    def bulk(kt, carry):
        s_cur, stats = carry
        s_next = scores(kt + 1)
        return s_next, flash_update(stats, s_cur, svt_ref[kt])

    stats = (jnp.full((1, rows), M_INIT, F32), jnp.zeros((1, rows), F32), jnp.zeros((HEAD_DIM, rows), F32))
    kt_last = s0 // SLC_TILE
    s_last, stats = lax.fori_loop(0, kt_last, bulk, (scores(0), stats))
    kpos = kt_last * SLC_TILE + lax.broadcasted_iota(jnp.int32, (SLC_TILE, 1), 0)
    _, l_s, acc_s = flash_update(stats, jnp.where(kpos <= qpos, s_last, NEG), svt_ref[kt_last])
    o_s = acc_s / l_s

    gsig = jax.nn.sigmoid(glt_ref[...])
    lane_gate = lambda c: jnp.concatenate([gsig[3 * r + c:3 * r + c + 1, :] for r in range(HEADS_PER_GROUP)], axis=1)
    out_t = lane_gate(0) * o_c + lane_gate(1) * o_s + lane_gate(2) * o_w
    for r in range(HEADS_PER_GROUP):
        o_ref[:, r * HEAD_DIM:(r + 1) * HEAD_DIM] = out_t[:, r * tq:(r + 1) * tq].T.astype(o_ref.dtype)


def nsa_prompt(q, gl_t, cka, cvt, ska, svt, wka, wvt, consts, nb, t_len):
    nqb = t_len // Q_BLOCK
    rows = HEADS_PER_GROUP * Q_BLOCK
    gw = 3 * HEADS_PER_GROUP
    qe, mimp_t = consts
    per_bg = lambda a: pl.BlockSpec((None, None) + a.shape[2:], lambda b, g, i: (b, g) + (0,) * (a.ndim - 2))
    return pl.pallas_call(
        functools.partial(_nsa_prompt_kernel, t_len=t_len),
        grid=(nb, N_KV_GROUPS, nqb),
        in_specs=[pl.BlockSpec((Q_BLOCK, HEADS_PER_GROUP * HEAD_DIM), lambda b, g, i: (b * nqb + i, g)),
                  pl.BlockSpec((None, gw, Q_BLOCK), lambda b, g, i: (g, 0, b * nqb + i)),
                  pl.BlockSpec((None, rows, HEAD_DIM), lambda b, g, i: (g, 0, 0)),
                  per_bg(cka), per_bg(cvt), per_bg(ska), per_bg(svt), per_bg(wka), per_bg(wvt),
                  pl.BlockSpec(mimp_t.shape, lambda b, g, i: (0, 0))],
        out_specs=pl.BlockSpec((Q_BLOCK, HEADS_PER_GROUP * HEAD_DIM), lambda b, g, i: (b * nqb + i, g)),
        out_shape=jax.ShapeDtypeStruct((nb * t_len, NSA_W), BF16),
        compiler_params=_params(),
        name="nsa_prompt",
    )(q, gl_t, qe, cka, cvt, ska, svt, wka, wvt, mimp_t)


def _nsa_sample_kernel(pt_ref, *refs, n_pages, dec, win_buf):
    page_refs = refs[:n_pages]
    (q_ref, gl_ref, slp_ref, ck_ref, cv_ref, snew_ref, wc_ref, wnew_ref, es_ref, mimp_ref, o_ref) = refs[n_pages:]
    tq = dec
    rows = HEADS_PER_GROUP * tq
    past = n_pages * PAGE_SIZE
    ncs = ck_ref.shape[1]
    t_row = lax.broadcasted_iota(jnp.int32, (rows, 1), 0) & (tq - 1)
    qpos = past + t_row
    gsig = jax.nn.sigmoid(gl_ref[...])
    q_all = q_ref[...]
    snew = snew_ref[...]
    wc = wc_ref[...]
    wnew = wnew_ref[...]
    tnew = lax.broadcasted_iota(jnp.int32, (1, tq), 1)
    mask_new = tnew <= t_row
    dist_new = (t_row - tnew).astype(F32)

    q4s, o_cs, imps = [], [], []
    for g in range(N_KV_GROUPS):
        q4 = (_stack_heads(q_all[:, g * HEADS_PER_GROUP * HEAD_DIM:(g + 1) * HEADS_PER_GROUP * HEAD_DIM])
              * (HEAD_DIM ** -0.5)).astype(BF16)
        cpos = lax.broadcasted_iota(jnp.int32, (1, ncs), 1) * CMP_STRIDE + (CMP_BLOCK - 1)
        mask_c = cpos <= qpos
        s_c = _dot_nt(q4, ck_ref[g]) - slp_ref[g] * (qpos - cpos).astype(F32)
        p_c = _masked_softmax(jnp.where(mask_c, s_c, NEG))
        q4s.append(q4)
        o_cs.append(_dot(p_c.astype(BF16), cv_ref[g]))
        imps.append(_importance(p_c, tq, mimp_ref))
    cur = (past + (lax.broadcasted_iota(jnp.int32, (1, N_KV_GROUPS * tq), 1) & (tq - 1))) >> SLC_SHIFT
    sel_t = _select_blocks(jnp.concatenate(imps, axis=0).T, cur, axis=0)
    selbias_all = jnp.where(sel_t, 0.0, NEG).T

    for g in range(N_KV_GROUPS):
        kc = slice(g * HEAD_DIM, (g + 1) * HEAD_DIM)
        vc = slice((N_KV_GROUPS + g) * HEAD_DIM, (N_KV_GROUPS + g + 1) * HEAD_DIM)
        q4, o_c = q4s[g], o_cs[g]
        slope = slp_ref[g]
        sb4 = jnp.concatenate([selbias_all[g * tq:(g + 1) * tq].astype(BF16)] * HEADS_PER_GROUP, axis=0)

        k_past = jnp.concatenate([pr[:, kc] for pr in page_refs], axis=0).astype(BF16)
        v_past = jnp.concatenate([pr[:, vc] for pr in page_refs], axis=0).astype(BF16)
        kpos = lax.broadcasted_iota(jnp.int32, (1, past), 1)
        s_p = _dot_nt(q4, k_past) + _dot_nt(sb4, es_ref[...]) - slope * (qpos - kpos).astype(F32)
        s_n = _dot_nt(q4, snew[:, kc].astype(BF16)) - slope * dist_new
        s_n = jnp.where(mask_new, s_n, NEG)
        m = jnp.maximum(jnp.max(s_p, axis=-1, keepdims=True), jnp.max(s_n, axis=-1, keepdims=True))
        e_p = jnp.exp(s_p - m)
        e_n = jnp.exp(s_n - m)
        l = jnp.sum(e_p, axis=-1, keepdims=True) + jnp.sum(e_n, axis=-1, keepdims=True)
        o_s = (_dot(e_p.astype(BF16), v_past) + _dot(e_n.astype(BF16), snew[:, vc].astype(BF16))) / l

        wdist = qpos - (past - win_buf + lax.broadcasted_iota(jnp.int32, (1, win_buf), 1))
        mask_w = wdist < WINDOW
        s_w = _dot_nt(q4, wc[:, kc].astype(BF16)) - slope * wdist.astype(F32)
        s_w = jnp.where(mask_w, s_w, NEG)
        s_wn = _dot_nt(q4, wnew[:, kc].astype(BF16)) - slope * dist_new
        s_wn = jnp.where(mask_new, s_wn, NEG)
        m = jnp.maximum(jnp.max(s_w, axis=-1, keepdims=True), jnp.max(s_wn, axis=-1, keepdims=True))
        e_w = jnp.exp(s_w - m)
        e_wn = jnp.exp(s_wn - m)
        l = jnp.sum(e_w, axis=-1, keepdims=True) + jnp.sum(e_wn, axis=-1, keepdims=True)
        o_w = (_dot(e_w.astype(BF16), wc[:, vc].astype(BF16)) + _dot(e_wn.astype(BF16), wnew[:, vc].astype(BF16))) / l

        def store(r, val, g=g):
            c = (g * HEADS_PER_GROUP + r) * HEAD_DIM
            o_ref[:, c:c + HEAD_DIM] = val.astype(o_ref.dtype)

        _gated_merge(gsig, g, o_c, o_s, o_w, tq, store)


def nsa_sample(q, gl, ck, cv, pool, page_table, slc_rows, win_cache, win_rows, consts, row_off, dec):
    nseq, n_pages = page_table.shape
    win_buf = win_cache.shape[1]
    slp, es, mimp = consts
    full = lambda a: pl.BlockSpec(a.shape, lambda b, pt: (0,) * a.ndim)
    row = lambda c: pl.BlockSpec((dec, c), lambda b, pt: (row_off + b, 0))
    seq4 = lambda a: pl.BlockSpec((None,) + a.shape[1:], lambda b, pt: (b, 0, 0, 0))
    page_spec = lambda p: pl.BlockSpec((None, PAGE_SIZE, KV_ROW), lambda b, pt: (pt[b, p], 0, 0))
    return pl.pallas_call(
        functools.partial(_nsa_sample_kernel, n_pages=n_pages, dec=dec, win_buf=win_buf),
        grid_spec=pltpu.PrefetchScalarGridSpec(
            num_scalar_prefetch=1,
            grid=(nseq,),
            in_specs=[page_spec(p) for p in range(n_pages)]
            + [row(NSA_W), row(GATE_W), full(slp), seq4(ck), seq4(cv), row(KV_ROW),
               pl.BlockSpec((None, win_buf, KV_ROW), lambda b, pt: (b, 0, 0)), row(KV_ROW), full(es), full(mimp)],
            out_specs=pl.BlockSpec((dec, NSA_W), lambda b, pt: (b, 0)),
        ),
        out_shape=jax.ShapeDtypeStruct((nseq * dec, NSA_W), F32),
        compiler_params=_params(),
        name="nsa_sample",
    )(page_table, *([pool] * n_pages), q, gl, slp, ck, cv, slc_rows, win_cache, win_rows, es, mimp)


def _alibi_slopes():
    return 2.0 ** (-ALIBI_MAX_BIAS * jnp.arange(1, N_HEADS + 1, dtype=F32) / N_HEADS)


def _slope_rows(tq):
    return jnp.repeat(_alibi_slopes().reshape(N_KV_GROUPS, HEADS_PER_GROUP), tq, axis=1)[..., None]


def _importance_matrix(n_cmp):
    n = jnp.arange(n_cmp)[:, None]
    j = jnp.arange(LANES)[None, :]
    per_sel = SLC_BLOCK // CMP_STRIDE
    inside = (SLC_BLOCK - CMP_BLOCK) // CMP_STRIDE + 1
    return ((n // per_sel == j) & (n % per_sel < inside)).astype(BF16)


def _block_onehot(n_keys):
    k = jnp.arange(n_keys)[:, None]
    return (k // SLC_BLOCK == jnp.arange(LANES)[None, :]).astype(BF16)


def _position_pieces(pos):
    pos = pos[:, None]
    hi = ((pos // SLC_BLOCK) * SLC_BLOCK).astype(BF16)
    lo = (pos % SLC_BLOCK).astype(BF16)
    pieces = jnp.concatenate([hi, hi, hi, lo, lo, lo], axis=-1)
    return jnp.pad(pieces, ((0, 0), (0, HEAD_DIM - pieces.shape[-1])))


def _prompt_tables(t_len):
    slp = _slope_rows(Q_BLOCK)
    s1 = slp.astype(BF16)
    s2 = (slp - s1.astype(F32)).astype(BF16)
    s3 = (slp - s1.astype(F32) - s2.astype(F32)).astype(BF16)
    qe = jnp.concatenate([s1, s2, s3, s1, s2, s3], axis=-1)
    qe = jnp.pad(qe, ((0, 0), (0, 0), (0, HEAD_DIM - qe.shape[-1])))
    key_tab = jnp.concatenate([_block_onehot(t_len), _position_pieces(jnp.arange(t_len))], axis=-1)
    ncp = t_len // CMP_STRIDE
    cmp_tab = _position_pieces(jnp.arange(ncp) * CMP_STRIDE + (CMP_BLOCK - 1))
    return (qe, jnp.transpose(_importance_matrix(ncp))), key_tab, cmp_tab


def kernel(x_prompt, x_sample, state_conv, cache_mem_kv, cache_cmp_kv, cache_slc_kv, cache_win_kv, page_table,
           mem_prompt, norm1, norm2, norm_final, w_in_a, conv_w, conv_b, conv_ln_g, conv_ln_b, w_in_b, kv_norm,
           w_kv_shared, cmp_pos, cmp_w1, cmp_w2, w_mem_kv, w_out, ffn_w_gate, ffn_w_up, ffn_w_down, moe_router,
           moe_w_gate, moe_w_up, moe_w_down):
    bp, t_len, d = x_prompt.shape
    db, dec, _ = x_sample.shape
    depth = norm1.shape[0]
    n_a = w_in_a.shape[0]
    n_mem = mem_prompt.shape[1]
    n_p = bp * t_len
    n_s = db * dec
    n_pages = page_table.shape[1]
    past = n_pages * PAGE_SIZE
    win_buf = cache_win_kv.shape[1]
    bf = lambda a: a.astype(BF16)

    x = jnp.concatenate([x_prompt.reshape(n_p, d), x_sample.reshape(n_s, d)], axis=0)

    mem_kv_p = mem_kv_proj(mem_prompt.reshape(bp * n_mem, d), w_mem_kv)
    mem_kv_p = mem_kv_p.reshape(depth, bp, n_mem, 2 * MEM_W)
    mem_kv_s = cache_mem_kv.reshape(depth, db, n_mem, 2 * MEM_W)
    moe_wg, moe_wu, moe_wd = bf(moe_w_gate), bf(moe_w_up), bf(moe_w_down)

    conv_p, conv_s = [], []
    zero_state = jnp.zeros((bp, CONV_STATE, C_CONV), F32)
    cmp_rows = slc_rows = win_rows = None
    nsa_ctx = None

    for l in range(depth):
        if l < n_a:
            uc, qm = norm_proj(x, norm1[l], bf(w_in_a[l]), (2 * C_CONV, MEM_W))
            mix_p, st_p = conv_mixer(uc, zero_state, conv_w[l], conv_b[l], conv_ln_g[l], conv_ln_b[l],
                                     bp, t_len, Q_BLOCK)
            mix_s, st_s = conv_mixer(uc, state_conv[l], conv_w[l], conv_b[l], conv_ln_g[l], conv_ln_b[l],
                                     db, dec, dec, row0=n_p)
            conv_p.append(st_p)
            conv_s.append(st_s)
        else:
            if nsa_ctx is None:
                cmp_rows, slc_rows, win_rows = norm_proj(x, kv_norm, bf(w_kv_shared), (KV_ROW,) * 3)
                w1 = bf(cmp_w1)
                w2 = bf(cmp_w2)
                prompt_consts, key_tab, cmp_tab = _prompt_tables(t_len)
                pe = cmp_pos.reshape(2, CMP_BLOCK // CMP_PGROUP, CMP_PGROUP * HEAD_DIM)
                cka, cvt = compress_prompt(cmp_rows[:n_p], pe, w1, w2, cmp_tab, bp, t_len)
                ck_s, cv_s = compress_sample(cache_cmp_kv.reshape(-1, PAGE_SIZE, KV_ROW), page_table,
                                             cmp_rows[n_p:].reshape(db, dec, KV_ROW), pe, w1, w2)
                ska, svt, wka, wvt = kv_prep(slc_rows, win_rows, key_tab, bp, t_len)
                nsa_ctx = dict(
                    prompt=prompt_consts,
                    sample=(_slope_rows(dec), _block_onehot(past), _importance_matrix(ck_s.shape[2])),
                    pool=cache_slc_kv.reshape(-1, PAGE_SIZE, KV_ROW),
                    win_cache=cache_win_kv.reshape(db, win_buf, KV_ROW))
            w_in = w_in_b[l - n_a]
            w_in = jnp.concatenate([w_in[:, :NSA_W], w_in[:, NSA_W + GATE_W:], w_in[:, NSA_W:NSA_W + GATE_W]], axis=1)
            q, qm, gl = norm_proj(x, norm1[l], bf(w_in), (NSA_W, MEM_W, GATE_W))
            gw = 3 * HEADS_PER_GROUP
            gl_t = jnp.transpose(gl[:n_p].reshape(n_p, N_KV_GROUPS, gw), (1, 2, 0))
            mix_p = nsa_prompt(q, gl_t, cka, cvt, ska, svt, wka, wvt, nsa_ctx["prompt"], bp, t_len)
            mix_s = nsa_sample(q, gl, ck_s, cv_s, nsa_ctx["pool"], page_table, slc_rows, nsa_ctx["win_cache"],
                               win_rows, nsa_ctx["sample"], n_p // dec, dec)
        mo_p = mem_attend(qm, mem_kv_p, l, bp, t_len, 512)
        mo_s = mem_attend(qm, mem_kv_s, l, db, dec, dec, spb=MEM_SEQS_PER_STEP, row0=n_p)
        if l % 2 == 0:
            xn, h2 = outproj_norm(x, mix_p, mix_s, mo_p, mo_s, bf(w_out[l]), norm2[l])
            x = ffn_dense(h2, xn, bf(ffn_w_gate[l // 2]), bf(ffn_w_up[l // 2]), bf(ffn_w_down[l // 2]))
        else:
            xn, h2, gate = outproj_norm(x, mix_p, mix_s, mo_p, mo_s, bf(w_out[l]), norm2[l],
                                        router=moe_router[l // 2])
            x = moe_ffn(h2, xn, gate, moe_wg, moe_wu, moe_wd, l // 2)

    y_p = final_norm(x, norm_final, 0, n_p)
    y_s = final_norm(x, norm_final, n_p, n_s)
    kv5 = lambda a, b, t: a.reshape(b, t, 2, N_KV_GROUPS, HEAD_DIM)
    keep_p = min(WINDOW, t_len)
    win_p = kv5(win_rows[:n_p], bp, t_len)[:, t_len - keep_p:]
    win_all = jnp.concatenate([cache_win_kv, kv5(win_rows[n_p:], db, dec)], axis=1)
    keep_s = min(WINDOW, past + dec)
    return (y_p.reshape(bp, t_len, d), y_s.reshape(db, dec, d),
            jnp.stack(conv_p), jnp.stack(conv_s),
            mem_kv_p.reshape(depth, bp, n_mem, 2, MEM_HEADS, MEM_HEAD_DIM),
            kv5(cmp_rows[:n_p], bp, t_len), kv5(cmp_rows[n_p:], db, dec),
            kv5(slc_rows[:n_p], bp, t_len), kv5(slc_rows[n_p:], db, dec),
            win_p, win_all[:, win_all.shape[1] - keep_s:])
```

```python
import functools

import jax
import jax.numpy as jnp
from jax import lax
from jax.experimental import pallas as pl
from jax.experimental.pallas import tpu as pltpu

F32 = jnp.float32
BF16 = jnp.bfloat16

HEAD_DIM = 64
N_KV_GROUPS = 3
HEADS_PER_GROUP = 4
N_HEADS = N_KV_GROUPS * HEADS_PER_GROUP
NSA_W = N_HEADS * HEAD_DIM
GATE_W = 3 * N_HEADS
CMP_STRIDE = 16
CMP_BLOCK = 32
SLC_BLOCK = 64
SLC_SHIFT = 6
N_SEL = 16
WINDOW = 512
Q_BLOCK = 128
ALIBI_MAX_BIAS = 8.0
MEM_HEADS = 4
MEM_HEAD_DIM = 64
MEM_W = MEM_HEADS * MEM_HEAD_DIM
MEM_SEQS_PER_STEP = 8
C_CONV = 768
CONV_K = 31
CONV_STATE = CONV_K - 1
N_EXPERTS = 8
EPS = 1e-6
PAGE_SIZE = 128
KV_ROW = 2 * N_KV_GROUPS * HEAD_DIM
CHUNK_ROW = CMP_STRIDE * KV_ROW
LANES = 128
SUBLANES = 8
NEG = -1e30
M_INIT = -1e29
VMEM_LIMIT = 56 * 1024 * 1024


def _dot(a, b):
    return jnp.dot(a, b, preferred_element_type=F32)


def _dot_nt(a, b):
    return lax.dot_general(a, b, (((1,), (1,)), ((), ())), preferred_element_type=F32)


def _split_hi_lo(x):
    hi = x.astype(BF16)
    lo = (x - hi.astype(F32)).astype(BF16)
    return hi, lo


def _rms(x, g):
    return x * lax.rsqrt(jnp.mean(x * x, axis=-1, keepdims=True) + EPS) * g


def _params(**kw):
    return pltpu.CompilerParams(vmem_limit_bytes=VMEM_LIMIT, **kw)


def _log2(n):
    assert n > 0 and n & (n - 1) == 0, n
    return n.bit_length() - 1


def _row_tile(n, pref):
    tm = pref
    while n % tm:
        tm //= 2
    assert tm >= 8, (n, pref)
    return tm


def _norm_proj_kernel(x_ref, g_ref, w_ref, *o_refs, splits):
    h = _rms(x_ref[...], g_ref[...])
    y = _dot(h.astype(BF16), w_ref[...])
    off = 0
    for o_ref, n in zip(o_refs, splits):
        o_ref[...] = y[:, off:off + n]
        off += n


def norm_proj(x, g, w, splits):
    n, d = x.shape
    tm = _row_tile(n, 512)
    return pl.pallas_call(
        functools.partial(_norm_proj_kernel, splits=splits),
        grid=(n // tm,),
        in_specs=[pl.BlockSpec((tm, d), lambda i: (i, 0)),
                  pl.BlockSpec((1, d), lambda i: (0, 0)),
                  pl.BlockSpec(w.shape, lambda i: (0, 0))],
        out_specs=[pl.BlockSpec((tm, s), lambda i: (i, 0)) for s in splits],
        out_shape=[jax.ShapeDtypeStruct((n, s), F32) for s in splits],
        compiler_params=_params(),
        name="norm_proj",
    )(x, g.reshape(1, d), w)


def _mem_kv_kernel(x_ref, w_ref, o_ref):
    o_ref[...] = _dot(x_ref[...].astype(BF16), w_ref[...].astype(BF16))


def mem_kv_proj(x, w):
    depth, d, e = w.shape
    n = x.shape[0]
    return pl.pallas_call(
        _mem_kv_kernel,
        grid=(depth,),
        in_specs=[pl.BlockSpec((n, d), lambda l: (0, 0)),
                  pl.BlockSpec((None, d, e), lambda l: (l, 0, 0))],
        out_specs=pl.BlockSpec((None, n, e), lambda l: (l, 0, 0)),
        out_shape=jax.ShapeDtypeStruct((depth, n, e), F32),
        compiler_params=_params(),
        name="mem_kv_proj",
    )(x, w)


CONV_PAD = 32


def _conv_kernel(u_ref, st_ref, w_ref, b_ref, lg_ref, lb_ref, mix_ref, ns_ref, vp_ref, *, tt, rc):
    t = pl.program_id(1)
    lo = CONV_PAD - CONV_STATE

    @pl.when(t == 0)
    def _():
        vp_ref[lo:CONV_PAD, :] = st_ref[...]

    u = u_ref[...]
    vp_ref[CONV_PAD:CONV_PAD + tt, :] = u[:, :C_CONV] * jax.nn.sigmoid(u[:, C_CONV:])
    vp_ref[CONV_PAD + tt:CONV_PAD + tt + SUBLANES, :] = jnp.zeros((SUBLANES, C_CONV), F32)
    for c in range(tt // rc):
        acc = None
        for b in range(SUBLANES):
            part = None
            for k in range(CONV_K):
                a, phase = divmod(lo + k, SUBLANES)
                if phase == b:
                    term = w_ref[k:k + 1, :] * vp_ref[pl.ds(c * rc + SUBLANES * a, rc + SUBLANES), :]
                    part = term if part is None else part + term
            acc = part[b:b + rc] if acc is None else acc + part[b:b + rc]
        h = acc + b_ref[...]
        mu = jnp.mean(h, axis=-1, keepdims=True)
        hc = h - mu
        var = jnp.mean(hc * hc, axis=-1, keepdims=True)
        y = hc * lax.rsqrt(var + EPS) * lg_ref[...] + lb_ref[...]
        mix_ref[c * rc:(c + 1) * rc, :] = y * jax.nn.sigmoid(y)
    new_state = vp_ref[pl.ds(lo + tt, CONV_STATE), :]
    ns_ref[...] = new_state
    vp_ref[lo:CONV_PAD, :] = new_state


def conv_mixer(u, state, w, b, lg, lb, nseq, t_len, tt, row0=0):
    rc = min(tt, 32)
    nt = t_len // tt
    blk0 = row0 // tt
    vec = lambda a: a.reshape(1, C_CONV)
    return pl.pallas_call(
        functools.partial(_conv_kernel, tt=tt, rc=rc),
        grid=(nseq, nt),
        in_specs=[pl.BlockSpec((tt, 2 * C_CONV), lambda s, t: (blk0 + s * nt + t, 0)),
                  pl.BlockSpec((None, CONV_STATE, C_CONV), lambda s, t: (s, 0, 0)),
                  pl.BlockSpec((CONV_K, C_CONV), lambda s, t: (0, 0)),
                  pl.BlockSpec((1, C_CONV), lambda s, t: (0, 0)),
                  pl.BlockSpec((1, C_CONV), lambda s, t: (0, 0)),
                  pl.BlockSpec((1, C_CONV), lambda s, t: (0, 0))],
        out_specs=[pl.BlockSpec((tt, C_CONV), lambda s, t: (s * nt + t, 0)),
                   pl.BlockSpec((None, CONV_STATE, C_CONV), lambda s, t: (s, 0, 0))],
        out_shape=[jax.ShapeDtypeStruct((nseq * t_len, C_CONV), F32),
                   jax.ShapeDtypeStruct((nseq, CONV_STATE, C_CONV), F32)],
        scratch_shapes=[pltpu.VMEM((CONV_PAD + tt + SUBLANES, C_CONV), F32)],
        compiler_params=_params(),
        name="conv_mixer",
    )(u, state, w, vec(b), vec(lg), vec(lb))


def _mem_attn_kernel(q_ref, kv_ref, o_ref, *, spb, tq):
    n_mem = kv_ref.shape[1]
    q = q_ref[...]
    kv = kv_ref[...].reshape(spb * n_mem, 2 * MEM_W).astype(BF16)
    own = None
    if spb > 1:
        q_seq = lax.broadcasted_iota(jnp.int32, (spb * tq, 1), 0) >> _log2(tq)
        k_seq = lax.broadcasted_iota(jnp.int32, (1, spb * n_mem), 1) >> _log2(n_mem)
        own = q_seq == k_seq
    for h in range(MEM_HEADS):
        c = h * MEM_HEAD_DIM
        qh = q[:, c:c + MEM_HEAD_DIM].astype(BF16)
        s = _dot_nt(qh, kv[:, c:c + MEM_HEAD_DIM]) * (MEM_HEAD_DIM ** -0.5)
        if own is not None:
            s = jnp.where(own, s, NEG)
        e = jnp.exp(s - jnp.max(s, axis=-1, keepdims=True))
        o = _dot(e.astype(BF16), kv[:, MEM_W + c:MEM_W + c + MEM_HEAD_DIM])
        o_ref[:, c:c + MEM_HEAD_DIM] = o / jnp.sum(e, axis=-1, keepdims=True)


def mem_attend(qm, kv, layer, nseq, t_len, tq, spb=1, row0=0):
    nt = t_len // tq
    assert spb == 1 or nt == 1
    n_mem = kv.shape[2]
    rows = spb * tq
    blk0 = row0 // rows
    return pl.pallas_call(
        functools.partial(_mem_attn_kernel, spb=spb, tq=tq),
        grid=(nseq // spb, nt),
        in_specs=[pl.BlockSpec((rows, MEM_W), lambda s, t: (blk0 + s * nt + t, 0)),
                  pl.BlockSpec((None, spb, n_mem, 2 * MEM_W), lambda s, t: (layer, s, 0, 0))],
        out_specs=pl.BlockSpec((rows, MEM_W), lambda s, t: (s * nt + t, 0)),
        out_shape=jax.ShapeDtypeStruct((nseq * t_len, MEM_W), F32),
        compiler_params=_params(),
        name="mem_attend",
    )(qm, kv)


def _outproj_kernel(x_ref, mixp_ref, mixs_ref, mop_ref, mos_ref, w_ref, g_ref, *rest, moe, prompt_tiles):
    if moe:
        r_ref, xn_ref, h_ref, gate_ref = rest
    else:
        xn_ref, h_ref = rest
    is_prompt = pl.program_id(0) < prompt_tiles
    mix = jnp.where(is_prompt, mixp_ref[...].astype(BF16), mixs_ref[...].astype(BF16))
    mo = jnp.where(is_prompt, mop_ref[...].astype(BF16), mos_ref[...].astype(BF16))
    d_mix = mix.shape[1]
    y = _dot(mix, w_ref[0:d_mix, :]) + _dot(mo, w_ref[d_mix:, :])
    xn = x_ref[...] + y
    xn_ref[...] = xn
    h = _rms(xn, g_ref[...])
    h_ref[...] = h.astype(BF16)
    if moe:
        h_hi, h_lo = _split_hi_lo(h)
        r_hi, r_lo = _split_hi_lo(r_ref[...])
        logits = _dot(h_hi, r_hi) + _dot(h_lo, r_hi) + _dot(h_hi, r_lo)
        lane = lax.broadcasted_iota(jnp.int32, logits.shape, 1)
        logits = jnp.where(lane < N_EXPERTS, logits, -jnp.inf)
        m1 = jnp.max(logits, axis=-1, keepdims=True)
        i1 = jnp.min(jnp.where(logits == m1, lane, LANES), axis=-1, keepdims=True)
        rest_l = jnp.where(lane == i1, -jnp.inf, logits)
        m2 = jnp.max(rest_l, axis=-1, keepdims=True)
        i2 = jnp.min(jnp.where(rest_l == m2, lane, LANES), axis=-1, keepdims=True)
        e2 = jnp.exp(m2 - m1)
        den = 1.0 + e2
        gate_ref[...] = jnp.where(lane == i1, 1.0 / den, 0.0) + jnp.where(lane == i2, e2 / den, 0.0)


def outproj_norm(x, mix_p, mix_s, mo_p, mo_s, w, g, router=None):
    n, d = x.shape
    tm = _row_tile(n, 512)
    assert mix_p.shape[0] % tm == 0 and mix_s.shape[0] % tm == 0
    pt = mix_p.shape[0] // tm
    moe = router is not None
    row = lambda c: pl.BlockSpec((tm, c), lambda i: (i, 0))
    row_p = lambda c: pl.BlockSpec((tm, c), lambda i: (jnp.minimum(i, pt - 1), 0))
    row_s = lambda c: pl.BlockSpec((tm, c), lambda i: (jnp.maximum(i - pt, 0), 0))
    full = lambda a: pl.BlockSpec(a.shape, lambda i: (0, 0))
    ins = [x, mix_p, mix_s, mo_p, mo_s, w, g.reshape(1, d)]
    in_specs = [row(d), row_p(mix_p.shape[1]), row_s(mix_s.shape[1]), row_p(mo_p.shape[1]), row_s(mo_s.shape[1]),
                full(w), pl.BlockSpec((1, d), lambda i: (0, 0))]
    out_specs = [row(d), row(d)]
    out_shape = [jax.ShapeDtypeStruct((n, d), F32), jax.ShapeDtypeStruct((n, d), BF16)]
    if moe:
        rp = jnp.pad(router, ((0, 0), (0, LANES - router.shape[1])))
        ins.append(rp)
        in_specs.append(full(rp))
        out_specs.append(row(LANES))
        out_shape.append(jax.ShapeDtypeStruct((n, LANES), F32))
    return pl.pallas_call(
        functools.partial(_outproj_kernel, moe=moe, prompt_tiles=pt),
        grid=(n // tm,),
        in_specs=in_specs, out_specs=out_specs, out_shape=out_shape,
        compiler_params=_params(),
        name="outproj_norm",
    )(*ins)


def _ffn_kernel(h_ref, x_ref, wg_ref, wu_ref, wd_ref, o_ref):
    @pl.when(pl.program_id(1) == 0)
    def _():
        o_ref[...] = x_ref[...]

    h = h_ref[...]
    a = _dot(h, wg_ref[...])
    u = _dot(h, wu_ref[...])
    act = (a * jax.nn.sigmoid(a) * u).astype(BF16)
    o_ref[...] += _dot(act, wd_ref[...])


def ffn_dense(h, x, wg, wu, wd, nf=2):
    n, d = x.shape
    tm = _row_tile(n, 512)
    tf = wg.shape[1] // nf
    return pl.pallas_call(
        _ffn_kernel,
        grid=(n // tm, nf),
        in_specs=[pl.BlockSpec((tm, d), lambda i, j: (i, 0)),
                  pl.BlockSpec((tm, d), lambda i, j: (i, 0)),
                  pl.BlockSpec((d, tf), lambda i, j: (0, j)),
                  pl.BlockSpec((d, tf), lambda i, j: (0, j)),
                  pl.BlockSpec((tf, d), lambda i, j: (j, 0))],
        out_specs=pl.BlockSpec((tm, d), lambda i, j: (i, 0)),
        out_shape=jax.ShapeDtypeStruct((n, d), F32),
        compiler_params=_params(),
        name="ffn_dense",
    )(h, x, wg, wu, wd)


MOE_RB = 128
MOE_RB_SHIFT = 7


def _moe_kernel(cnt_ref, h_ref, x_ref, gate_ref, gt_ref, wg_ref, wu_ref, wd_ref, o_ref,
                xs_ref, y_ref, srow_ref, scol_ref):
    i, e, j = pl.program_id(0), pl.program_id(1), pl.program_id(2)
    tm = h_ref.shape[0]
    nblk = (cnt_ref[i, e] + (MOE_RB - 1)) >> MOE_RB_SHIFT

    @pl.when((e == 0) & (j == 0))
    def _():
        o_ref[...] = x_ref[...]
        r = lax.broadcasted_iota(jnp.int32, (tm, tm), 0)
        c = lax.broadcasted_iota(jnp.int32, (tm, tm), 1)
        srow_ref[...] = _dot((gt_ref[...] != 0.0).astype(BF16), (r < c).astype(BF16))
        scol_ref[...] = _dot((c < r).astype(BF16), (gate_ref[...] != 0.0).astype(BF16))

    def block_rows(k):
        return pl.ds(pl.multiple_of(k * MOE_RB, MOE_RB), MOE_RB)

    def pick_row(a):
        return jnp.sum(jnp.where(lax.broadcasted_iota(jnp.int32, a.shape, 0) == e, a, 0.0), axis=0, keepdims=True)

    def pick_col(a):
        return jnp.sum(jnp.where(lax.broadcasted_iota(jnp.int32, a.shape, 1) == e, a, 0.0), axis=1, keepdims=True)

    @pl.when(j == 0)
    def _():
        mrow = pick_row(gt_ref[...]) != 0.0
        slot = pick_row(srow_ref[...])

        def pack(k, _):
            rid = (k * MOE_RB + lax.broadcasted_iota(jnp.int32, (MOE_RB, 1), 0)).astype(F32)
            onehot = (mrow & (slot == rid)).astype(BF16)
            xs_ref[block_rows(k), :] = _dot(onehot, h_ref[...]).astype(BF16)
            y_ref[block_rows(k), :] = jnp.zeros((MOE_RB, y_ref.shape[1]), F32)
            return 0

        lax.fori_loop(0, nblk, pack, 0)

    def expert(rows):
        xs = xs_ref[rows, :]
        a = _dot(xs, wg_ref[...])
        u = _dot(xs, wu_ref[...])
        act = (a * jax.nn.sigmoid(a) * u).astype(BF16)
        y_ref[rows, :] += _dot(act, wd_ref[...])

    def expert_pair(k2, _):
        expert(pl.ds(pl.multiple_of(k2 * (2 * MOE_RB), 2 * MOE_RB), 2 * MOE_RB))
        return 0

    if tm >= 2 * MOE_RB:
        lax.fori_loop(0, nblk >> 1, expert_pair, 0)

        @pl.when((nblk & 1) == 1)
        def _():
            expert(block_rows(nblk - 1))
    else:
        lax.fori_loop(0, nblk, lambda k, _: (expert(block_rows(k)), 0)[1], 0)

    @pl.when(j == pl.num_programs(2) - 1)
    def _():
        ge = pick_col(gate_ref[...])
        mcol = ge != 0.0
        slot = pick_col(scol_ref[...])

        def unpack(row0, width):
            cid = (row0 + lax.broadcasted_iota(jnp.int32, (1, width), 1)).astype(F32)
            onehot_t = (mcol & (slot == cid)).astype(BF16)
            y_hi, y_lo = _split_hi_lo(y_ref[pl.ds(pl.multiple_of(row0, MOE_RB), width), :])
            o_ref[...] += ge * _dot(jnp.concatenate([onehot_t, onehot_t], axis=1),
                                    jnp.concatenate([y_hi, y_lo], axis=0))

        if tm >= 2 * MOE_RB:
            lax.fori_loop(0, nblk >> 1, lambda k2, _: (unpack(k2 * (2 * MOE_RB), 2 * MOE_RB), 0)[1], 0)

            @pl.when((nblk & 1) == 1)
            def _():
                unpack((nblk - 1) * MOE_RB, MOE_RB)
        else:
            lax.fori_loop(0, nblk, lambda k, _: (unpack(k * MOE_RB, MOE_RB), 0)[1], 0)


def moe_ffn(h, x, gate, wg, wu, wd, layer):
    n, d = x.shape
    tm = _row_tile(n, 1024)
    nt = n // tm
    n_exp, d_exp = wg.shape[1], wg.shape[3]
    tf = next(t for t in (896, 512, 256, 128) if d_exp % t == 0)
    nf = d_exp // tf
    routed = gate[:, :n_exp] != 0.0
    counts = jnp.sum(routed.reshape(nt, tm, n_exp), axis=1, dtype=jnp.int32)
    gate_t = jnp.transpose(gate[:, :n_exp])
    return pl.pallas_call(
        _moe_kernel,
        grid_spec=pltpu.PrefetchScalarGridSpec(
            num_scalar_prefetch=1,
            grid=(nt, n_exp, nf),
            in_specs=[pl.BlockSpec((tm, d), lambda i, e, j, c: (i, 0)),
                      pl.BlockSpec((tm, d), lambda i, e, j, c: (i, 0)),
                      pl.BlockSpec((tm, LANES), lambda i, e, j, c: (i, 0)),
                      pl.BlockSpec((n_exp, tm), lambda i, e, j, c: (0, i)),
                      pl.BlockSpec((None, None, d, tf), lambda i, e, j, c: (layer, e, 0, j)),
                      pl.BlockSpec((None, None, d, tf), lambda i, e, j, c: (layer, e, 0, j)),
                      pl.BlockSpec((None, None, tf, d), lambda i, e, j, c: (layer, e, j, 0))],
            out_specs=pl.BlockSpec((tm, d), lambda i, e, j, c: (i, 0)),
            scratch_shapes=[pltpu.VMEM((tm, d), BF16), pltpu.VMEM((tm, d), F32),
                            pltpu.VMEM((n_exp, tm), F32), pltpu.VMEM((tm, LANES), F32)],
        ),
        out_shape=jax.ShapeDtypeStruct((n, d), F32),
        compiler_params=_params(),
        name="moe_ffn",
    )(counts, h, x, gate, gate_t, wg, wu, wd)


def _final_norm_kernel(x_ref, g_ref, o_ref):
    o_ref[...] = _rms(x_ref[...], g_ref[...])


def final_norm(x, g, row0, n):
    d = x.shape[1]
    tm = _row_tile(n, 1024)
    assert row0 % tm == 0
    blk0 = row0 // tm
    return pl.pallas_call(
        _final_norm_kernel,
        grid=(n // tm,),
        in_specs=[pl.BlockSpec((tm, d), lambda i: (blk0 + i, 0)), pl.BlockSpec((1, d), lambda i: (0, 0))],
        out_specs=pl.BlockSpec((tm, d), lambda i: (i, 0)),
        out_shape=jax.ShapeDtypeStruct((n, d), F32),
        compiler_params=_params(),
        name="final_norm",
    )(x, g.reshape(1, d))


SLC_TILE = 512
WIN_TILE = 128


def _kv_prep_kernel(s_ref, w_ref, tab_ref, ska_ref, svt_ref, wka_ref, wvt_ref):
    tab = tab_ref[...]
    onehot, pieces = tab[:, :LANES], tab[:, LANES:]
    s = s_ref[...]
    w = w_ref[...]
    for g in range(N_KV_GROUPS):
        kc = slice(g * HEAD_DIM, (g + 1) * HEAD_DIM)
        vc = slice((N_KV_GROUPS + g) * HEAD_DIM, (N_KV_GROUPS + g + 1) * HEAD_DIM)
        ska_ref[g] = jnp.concatenate([onehot, s[:, kc].astype(BF16), pieces], axis=1)
        svt_ref[g] = s[:, vc].T.astype(BF16)
        wka_ref[g] = jnp.concatenate([w[:, kc].astype(BF16), pieces], axis=1)
        wv_t = w[:, vc].T.astype(BF16)
        for j in range(SLC_TILE // WIN_TILE):
            wvt_ref[g, j] = wv_t[:, j * WIN_TILE:(j + 1) * WIN_TILE]


def kv_prep(slc_rows, win_rows, tab, nb, t_len):
    tm = SLC_TILE
    nt = t_len // tm
    wpt = tm // WIN_TILE
    g3 = N_KV_GROUPS
    return pl.pallas_call(
        _kv_prep_kernel,
        grid=(nb, nt),
        in_specs=[pl.BlockSpec((tm, KV_ROW), lambda b, t: (b * nt + t, 0)),
                  pl.BlockSpec((tm, KV_ROW), lambda b, t: (b * nt + t, 0)),
                  pl.BlockSpec((tm, tab.shape[1]), lambda b, t: (t, 0))],
        out_specs=[pl.BlockSpec((None, g3, tm, 2 * LANES), lambda b, t: (b, 0, t, 0)),
                   pl.BlockSpec((None, g3, None, HEAD_DIM, tm), lambda b, t: (b, 0, t, 0, 0)),
                   pl.BlockSpec((None, g3, tm, LANES), lambda b, t: (b, 0, t, 0)),
                   pl.BlockSpec((None, g3, wpt, HEAD_DIM, WIN_TILE), lambda b, t: (b, 0, t, 0, 0))],
        out_shape=[jax.ShapeDtypeStruct((nb, g3, t_len, 2 * LANES), BF16),
                   jax.ShapeDtypeStruct((nb, g3, nt, HEAD_DIM, tm), BF16),
                   jax.ShapeDtypeStruct((nb, g3, t_len, LANES), BF16),
                   jax.ShapeDtypeStruct((nb, g3, t_len // WIN_TILE, HEAD_DIM, WIN_TILE), BF16)],
        compiler_params=_params(),
        name="kv_prep",
    )(slc_rows, win_rows, tab)


CMP_PGROUP = 4


def _cmp_first_layer(get_cols, kv, pe_ref, w1_ref):
    acc_a = None
    acc_b = None
    kw = CMP_PGROUP * HEAD_DIM
    for pg in range(CMP_STRIDE // CMP_PGROUP):
        xs = jnp.concatenate(
            [jnp.concatenate([get_cols(p * KV_ROW + (kv * N_KV_GROUPS + g) * HEAD_DIM)
                              for p in range(pg * CMP_PGROUP, (pg + 1) * CMP_PGROUP)], axis=1)
             for g in range(N_KV_GROUPS)], axis=0)
        qg = pg + CMP_STRIDE // CMP_PGROUP
        da = _dot((xs + pe_ref[kv, pg:pg + 1, :]).astype(BF16), w1_ref[kv, pg * kw:(pg + 1) * kw, :])
        db = _dot((xs + pe_ref[kv, qg:qg + 1, :]).astype(BF16), w1_ref[kv, qg * kw:(qg + 1) * kw, :])
        acc_a = da if acc_a is None else acc_a + da
        acc_b = db if acc_b is None else acc_b + db
    return acc_a, acc_b


def _cmp_second_layer(kv, nch, a_ref, b_ref, w2_ref, emit):
    for g in range(N_KV_GROUPS):
        hid = a_ref[g, 0:nch, :] + b_ref[g, pl.ds(1, nch), :]
        hid = hid * jax.nn.sigmoid(hid)
        emit(kv, g, _dot(hid.astype(BF16), w2_ref[kv]))


def _compress_prompt_kernel(x_ref, pe_ref, w1_ref, w2_ref, cke_ref, cka_ref, cvt_ref, a_ref, b_ref, *, nch, tc):
    def emit(kv, g, out):
        if kv == 0:
            cka_ref[g] = jnp.concatenate([out.astype(BF16), cke_ref[...]], axis=1)
        else:
            cvt_ref[g] = out.T.astype(BF16)

    for kv in range(2):
        for ct in range(nch // tc):
            rows = slice(ct * tc, (ct + 1) * tc)
            acc_a, acc_b = _cmp_first_layer(lambda c0: x_ref[rows, c0:c0 + HEAD_DIM], kv, pe_ref, w1_ref)
            for g in range(N_KV_GROUPS):
                a_ref[g, rows, :] = acc_a[g * tc:(g + 1) * tc]
                b_ref[g, rows, :] = acc_b[g * tc:(g + 1) * tc]
        b_ref[:, nch:nch + 8, :] = jnp.zeros((N_KV_GROUPS, 8, b_ref.shape[2]), F32)
        _cmp_second_layer(kv, nch, a_ref, b_ref, w2_ref, emit)


def compress_prompt(cmp_rows, pe, w1, w2, cke, nb, t_len):
    nch = t_len // CMP_STRIDE
    tc = min(nch, 128)
    hid = w1.shape[2]
    xc = cmp_rows.reshape(nb, nch, CHUNK_ROW)
    full = lambda a: pl.BlockSpec(a.shape, lambda b: (0,) * a.ndim)
    return pl.pallas_call(
        functools.partial(_compress_prompt_kernel, nch=nch, tc=tc),
        grid=(nb,),
        in_specs=[pl.BlockSpec((None, nch, CHUNK_ROW), lambda b: (b, 0, 0)), full(pe), full(w1), full(w2), full(cke)],
        out_specs=[pl.BlockSpec((None, N_KV_GROUPS, nch, LANES), lambda b: (b, 0, 0, 0)),
                   pl.BlockSpec((None, N_KV_GROUPS, HEAD_DIM, nch), lambda b: (b, 0, 0, 0))],
        out_shape=[jax.ShapeDtypeStruct((nb, N_KV_GROUPS, nch, LANES), BF16),
                   jax.ShapeDtypeStruct((nb, N_KV_GROUPS, HEAD_DIM, nch), BF16)],
        scratch_shapes=[pltpu.VMEM((N_KV_GROUPS, nch + 8, hid), F32)] * 2,
        compiler_params=_params(),
        name="compress_prompt",
    )(xc, pe, w1, w2, cke)


def _compress_sample_kernel(pt_ref, *refs, n_pages):
    page_refs = refs[:n_pages]
    new_ref, pe_ref, w1_ref, w2_ref, ck_ref, cv_ref, a_ref, b_ref = refs[n_pages:]
    nch = n_pages * (PAGE_SIZE // CMP_STRIDE)
    for kv in range(2):
        acc_a, acc_b = _cmp_first_layer(
            lambda c0: jnp.concatenate([pr[:, c0:c0 + HEAD_DIM] for pr in page_refs], axis=0), kv, pe_ref, w1_ref)
        _, new_b = _cmp_first_layer(
            lambda c0: jnp.broadcast_to(new_ref[:, c0:c0 + HEAD_DIM], (8, HEAD_DIM)), kv, pe_ref, w1_ref)
        for g in range(N_KV_GROUPS):
            a_ref[g, 0:nch, :] = acc_a[g * nch:(g + 1) * nch]
            b_ref[g, 0:nch, :] = acc_b[g * nch:(g + 1) * nch]
            b_ref[g, nch:nch + 8, :] = new_b[g * 8:(g + 1) * 8]

        def emit(kv, g, out):
            (ck_ref, cv_ref)[kv][g] = out.astype(BF16)

        _cmp_second_layer(kv, nch, a_ref, b_ref, w2_ref, emit)


def compress_sample(pool, page_table, cmp_new, pe, w1, w2):
    nseq, n_pages = page_table.shape
    dec = cmp_new.shape[1]
    cpp = PAGE_SIZE // CMP_STRIDE
    nch = n_pages * cpp
    hid = w1.shape[2]
    pool_c = pool.reshape(pool.shape[0], cpp, CHUNK_ROW)
    new_c = jnp.pad(cmp_new.reshape(nseq, 1, dec * KV_ROW), ((0, 0), (0, 0), (0, CHUNK_ROW - dec * KV_ROW)))
    page_spec = lambda p: pl.BlockSpec((None, cpp, CHUNK_ROW), lambda b, pt: (pt[b, p], 0, 0))
    full = lambda a: pl.BlockSpec(a.shape, lambda b, pt: (0,) * a.ndim)
    o_spec = pl.BlockSpec((None, N_KV_GROUPS, nch, HEAD_DIM), lambda b, pt: (b, 0, 0, 0))
    o_shape = jax.ShapeDtypeStruct((nseq, N_KV_GROUPS, nch, HEAD_DIM), BF16)
    return pl.pallas_call(
        functools.partial(_compress_sample_kernel, n_pages=n_pages),
        grid_spec=pltpu.PrefetchScalarGridSpec(
            num_scalar_prefetch=1,
            grid=(nseq,),
            in_specs=[page_spec(p) for p in range(n_pages)]
            + [pl.BlockSpec((None, 1, CHUNK_ROW), lambda b, pt: (b, 0, 0)), full(pe), full(w1), full(w2)],
            out_specs=[o_spec, o_spec],
            scratch_shapes=[pltpu.VMEM((N_KV_GROUPS, nch + 8, hid), F32)] * 2,
        ),
        out_shape=[o_shape, o_shape],
        compiler_params=_params(),
        name="compress_sample",
    )(page_table, *([pool_c] * n_pages), new_c, pe, w1, w2)


def _softmax_parts(s, axis=-1):
    m = jnp.maximum(jnp.max(s, axis=axis, keepdims=True), M_INIT)
    e = jnp.exp(s - m)
    return e, jnp.sum(e, axis=axis, keepdims=True)


def _masked_softmax(s, axis=-1):
    e, d = _softmax_parts(s, axis)
    return e * (1.0 / jnp.where(d > 0, d, 1.0))


def _select_blocks(imp, cur, axis=-1):
    axis = axis % imp.ndim
    j = lax.broadcasted_iota(jnp.int32, imp.shape, axis)
    valid = j <= cur
    forced = valid & ((j == 0) | (j == cur) | (j == cur - 1))
    v = jnp.where(forced, jnp.inf, jnp.where(valid, imp, -jnp.inf))
    sel = jnp.zeros(imp.shape, F32)
    for _ in range(N_SEL):
        m = jnp.max(v, axis=axis, keepdims=True)
        jm = jnp.min(jnp.where(v == m, j, LANES), axis=axis, keepdims=True)
        pick = j == jm
        v = jnp.where(pick, -jnp.inf, v)
        sel = jnp.where(pick, 1.0, sel)
    return (sel > 0.0) & valid


def _stack_heads(q):
    return jnp.concatenate([q[:, r * HEAD_DIM:(r + 1) * HEAD_DIM] for r in range(HEADS_PER_GROUP)], axis=0)


def _importance(p_c, tq, mimp_ref):
    pcs = p_c[0:tq]
    for r in range(1, HEADS_PER_GROUP):
        pcs = pcs + p_c[r * tq:(r + 1) * tq]
    hi, lo = _split_hi_lo(pcs)
    return _dot(hi, mimp_ref[...]) + _dot(lo, mimp_ref[...])


def _gated_merge(gsig, g, o_c, o_s, o_w, tq, store):
    for r in range(HEADS_PER_GROUP):
        c = 3 * (g * HEADS_PER_GROUP + r)
        rows = slice(r * tq, (r + 1) * tq)
        store(r, gsig[:, c:c + 1] * o_c[rows] + gsig[:, c + 1:c + 2] * o_s[rows] + gsig[:, c + 2:c + 3] * o_w[rows])


def _nsa_prompt_kernel(q_ref, glt_ref, qe_ref, cka_ref, cvt_ref, ska_ref, svt_ref, wka_ref, wvt_ref,
                       mimpt_ref, o_ref, *, t_len):
    i = pl.program_id(2)
    tq = Q_BLOCK
    rows = HEADS_PER_GROUP * tq
    ncp = t_len // CMP_STRIDE
    s0 = i * tq
    q4 = (_stack_heads(q_ref[...]) * (HEAD_DIM ** -0.5)).astype(BF16)
    qb = jnp.concatenate([q4, qe_ref[...]], axis=1)
    qpos = s0 + (lax.broadcasted_iota(jnp.int32, (1, rows), 1) & (tq - 1))

    cpos = lax.broadcasted_iota(jnp.int32, (ncp, 1), 0) * CMP_STRIDE + (CMP_BLOCK - 1)
    mask_c = cpos <= qpos
    p_c = _masked_softmax(jnp.where(mask_c, _dot_nt(cka_ref[...], qb), NEG), axis=0)
    o_c = _dot(cvt_ref[...], p_c.astype(BF16))

    nwt = (WINDOW + tq) // WIN_TILE
    wb = jnp.maximum(i - WINDOW // WIN_TILE, 0)
    kw = pl.ds(pl.multiple_of(wb * WIN_TILE, WIN_TILE), nwt * WIN_TILE)
    dist = qpos - (wb * WIN_TILE + lax.broadcasted_iota(jnp.int32, (nwt * WIN_TILE, 1), 0))
    mask_w = (dist >= 0) & (dist < WINDOW)
    e_w, d_w = _softmax_parts(jnp.where(mask_w, _dot_nt(wka_ref[kw, :], qb), NEG), axis=0)
    e_w = e_w.astype(BF16)
    o_w = _dot(wvt_ref[wb], e_w[0:WIN_TILE])
    for w in range(1, nwt):
        o_w = o_w + _dot(wvt_ref[wb + w], e_w[w * WIN_TILE:(w + 1) * WIN_TILE])
    o_w = o_w / d_w

    pcs = p_c[:, 0:tq]
    for r in range(1, HEADS_PER_GROUP):
        pcs = pcs + p_c[:, r * tq:(r + 1) * tq]
    hi, lo = _split_hi_lo(pcs)
    imp = _dot(mimpt_ref[...], hi) + _dot(mimpt_ref[...], lo)
    sel = _select_blocks(imp, qpos[:, 0:tq] >> SLC_SHIFT, axis=0)
    selbias = jnp.where(sel, 0.0, NEG).T.astype(BF16)
    qa = jnp.concatenate([jnp.concatenate([selbias] * HEADS_PER_GROUP, axis=0), qb], axis=1)

    def scores(kt):
        return _dot_nt(ska_ref[pl.ds(pl.multiple_of(kt * SLC_TILE, SLC_TILE), SLC_TILE), :], qa)

    def flash_update(stats, s, vt):
        m, l, acc = stats
        m_new = jnp.maximum(m, jnp.max(s, axis=0, keepdims=True))
        a = jnp.exp(m - m_new)
        p = jnp.exp(s - m_new)
        return m_new, a * l + jnp.sum(p, axis=0, keepdims=True), a * acc + _dot(vt, p.astype(BF16))

    def bulk(kt, carry):
        s_cur, stats = carry
        s_next = scores(kt + 1)
        return s_next, flash_update(stats, s_cur, svt_ref[kt])

    stats = (jnp.full((1, rows), M_INIT, F32), jnp.zeros((1, rows), F32), jnp.zeros((HEAD_DIM, rows), F32))
    kt_last = s0 // SLC_TILE
    s_last, stats = lax.fori_loop(0, kt_last, bulk, (scores(0), stats))
    kpos = kt_last * SLC_TILE + lax.broadcasted_iota(jnp.int32, (SLC_TILE, 1), 0)
    _, l_s, acc_s = flash_update(stats, jnp.where(kpos <= qpos, s_last, NEG), svt_ref[kt_last])
    o_s = acc_s / l_s

    gsig = jax.nn.sigmoid(glt_ref[...])
    lane_gate = lambda c: jnp.concatenate([gsig[3 * r + c:3 * r + c + 1, :] for r in range(HEADS_PER_GROUP)], axis=1)
    out_t = lane_gate(0) * o_c + lane_gate(1) * o_s + lane_gate(2) * o_w
    for r in range(HEADS_PER_GROUP):
        o_ref[:, r * HEAD_DIM:(r + 1) * HEAD_DIM] = out_t[:, r * tq:(r + 1) * tq].T.astype(o_ref.dtype)


def nsa_prompt(q, gl_t, cka, cvt, ska, svt, wka, wvt, consts, nb, t_len):
    nqb = t_len // Q_BLOCK
    rows = HEADS_PER_GROUP * Q_BLOCK
    gw = 3 * HEADS_PER_GROUP
    qe, mimp_t = consts
    per_bg = lambda a: pl.BlockSpec((None, None) + a.shape[2:], lambda b, g, i: (b, g) + (0,) * (a.ndim - 2))
    return pl.pallas_call(
        functools.partial(_nsa_prompt_kernel, t_len=t_len),
        grid=(nb, N_KV_GROUPS, nqb),
        in_specs=[pl.BlockSpec((Q_BLOCK, HEADS_PER_GROUP * HEAD_DIM), lambda b, g, i: (b * nqb + i, g)),
                  pl.BlockSpec((None, gw, Q_BLOCK), lambda b, g, i: (g, 0, b * nqb + i)),
                  pl.BlockSpec((None, rows, HEAD_DIM), lambda b, g, i: (g, 0, 0)),
                  per_bg(cka), per_bg(cvt), per_bg(ska), per_bg(svt), per_bg(wka), per_bg(wvt),
                  pl.BlockSpec(mimp_t.shape, lambda b, g, i: (0, 0))],
        out_specs=pl.BlockSpec((Q_BLOCK, HEADS_PER_GROUP * HEAD_DIM), lambda b, g, i: (b * nqb + i, g)),
        out_shape=jax.ShapeDtypeStruct((nb * t_len, NSA_W), BF16),
        compiler_params=_params(),
        name="nsa_prompt",
    )(q, gl_t, qe, cka, cvt, ska, svt, wka, wvt, mimp_t)


def _nsa_sample_kernel(pt_ref, *refs, n_pages, dec, win_buf):
    page_refs = refs[:n_pages]
    (q_ref, gl_ref, slp_ref, ck_ref, cv_ref, snew_ref, wc_ref, wnew_ref, es_ref, mimp_ref, o_ref) = refs[n_pages:]
    tq = dec
    rows = HEADS_PER_GROUP * tq
    past = n_pages * PAGE_SIZE
    ncs = ck_ref.shape[1]
    t_row = lax.broadcasted_iota(jnp.int32, (rows, 1), 0) & (tq - 1)
    qpos = past + t_row
    gsig = jax.nn.sigmoid(gl_ref[...])
    q_all = q_ref[...]
    snew = snew_ref[...]
    wc = wc_ref[...]
    wnew = wnew_ref[...]
    tnew = lax.broadcasted_iota(jnp.int32, (1, tq), 1)
    mask_new = tnew <= t_row
    dist_new = (t_row - tnew).astype(F32)

    q4s, o_cs, imps = [], [], []
    for g in range(N_KV_GROUPS):
        q4 = (_stack_heads(q_all[:, g * HEADS_PER_GROUP * HEAD_DIM:(g + 1) * HEADS_PER_GROUP * HEAD_DIM])
              * (HEAD_DIM ** -0.5)).astype(BF16)
        cpos = lax.broadcasted_iota(jnp.int32, (1, ncs), 1) * CMP_STRIDE + (CMP_BLOCK - 1)
        mask_c = cpos <= qpos
        s_c = _dot_nt(q4, ck_ref[g]) - slp_ref[g] * (qpos - cpos).astype(F32)
        p_c = _masked_softmax(jnp.where(mask_c, s_c, NEG))
        q4s.append(q4)
        o_cs.append(_dot(p_c.astype(BF16), cv_ref[g]))
        imps.append(_importance(p_c, tq, mimp_ref))
    cur = (past + (lax.broadcasted_iota(jnp.int32, (1, N_KV_GROUPS * tq), 1) & (tq - 1))) >> SLC_SHIFT
    sel_t = _select_blocks(jnp.concatenate(imps, axis=0).T, cur, axis=0)
    selbias_all = jnp.where(sel_t, 0.0, NEG).T

    for g in range(N_KV_GROUPS):
        kc = slice(g * HEAD_DIM, (g + 1) * HEAD_DIM)
        vc = slice((N_KV_GROUPS + g) * HEAD_DIM, (N_KV_GROUPS + g + 1) * HEAD_DIM)
        q4, o_c = q4s[g], o_cs[g]
        slope = slp_ref[g]
        sb4 = jnp.concatenate([selbias_all[g * tq:(g + 1) * tq].astype(BF16)] * HEADS_PER_GROUP, axis=0)

        k_past = jnp.concatenate([pr[:, kc] for pr in page_refs], axis=0).astype(BF16)
        v_past = jnp.concatenate([pr[:, vc] for pr in page_refs], axis=0).astype(BF16)
        kpos = lax.broadcasted_iota(jnp.int32, (1, past), 1)
        s_p = _dot_nt(q4, k_past) + _dot_nt(sb4, es_ref[...]) - slope * (qpos - kpos).astype(F32)
        s_n = _dot_nt(q4, snew[:, kc].astype(BF16)) - slope * dist_new
        s_n = jnp.where(mask_new, s_n, NEG)
        m = jnp.maximum(jnp.max(s_p, axis=-1, keepdims=True), jnp.max(s_n, axis=-1, keepdims=True))
        e_p = jnp.exp(s_p - m)
        e_n = jnp.exp(s_n - m)
        l = jnp.sum(e_p, axis=-1, keepdims=True) + jnp.sum(e_n, axis=-1, keepdims=True)
        o_s = (_dot(e_p.astype(BF16), v_past) + _dot(e_n.astype(BF16), snew[:, vc].astype(BF16))) / l

        wdist = qpos - (past - win_buf + lax.broadcasted_iota(jnp.int32, (1, win_buf), 1))
        mask_w = wdist < WINDOW
        s_w = _dot_nt(q4, wc[:, kc].astype(BF16)) - slope * wdist.astype(F32)
        s_w = jnp.where(mask_w, s_w, NEG)
        s_wn = _dot_nt(q4, wnew[:, kc].astype(BF16)) - slope * dist_new
        s_wn = jnp.where(mask_new, s_wn, NEG)
        m = jnp.maximum(jnp.max(s_w, axis=-1, keepdims=True), jnp.max(s_wn, axis=-1, keepdims=True))
        e_w = jnp.exp(s_w - m)
        e_wn = jnp.exp(s_wn - m)
        l = jnp.sum(e_w, axis=-1, keepdims=True) + jnp.sum(e_wn, axis=-1, keepdims=True)
        o_w = (_dot(e_w.astype(BF16), wc[:, vc].astype(BF16)) + _dot(e_wn.astype(BF16), wnew[:, vc].astype(BF16))) / l

        def store(r, val, g=g):
            c = (g * HEADS_PER_GROUP + r) * HEAD_DIM
            o_ref[:, c:c + HEAD_DIM] = val.astype(o_ref.dtype)

        _gated_merge(gsig, g, o_c, o_s, o_w, tq, store)


def nsa_sample(q, gl, ck, cv, pool, page_table, slc_rows, win_cache, win_rows, consts, row_off, dec):
    nseq, n_pages = page_table.shape
    win_buf = win_cache.shape[1]
    slp, es, mimp = consts
    full = lambda a: pl.BlockSpec(a.shape, lambda b, pt: (0,) * a.ndim)
    row = lambda c: pl.BlockSpec((dec, c), lambda b, pt: (row_off + b, 0))
    seq4 = lambda a: pl.BlockSpec((None,) + a.shape[1:], lambda b, pt: (b, 0, 0, 0))
    page_spec = lambda p: pl.BlockSpec((None, PAGE_SIZE, KV_ROW), lambda b, pt: (pt[b, p], 0, 0))
    return pl.pallas_call(
        functools.partial(_nsa_sample_kernel, n_pages=n_pages, dec=dec, win_buf=win_buf),
        grid_spec=pltpu.PrefetchScalarGridSpec(
            num_scalar_prefetch=1,
            grid=(nseq,),
            in_specs=[page_spec(p) for p in range(n_pages)]
            + [row(NSA_W), row(GATE_W), full(slp), seq4(ck), seq4(cv), row(KV_ROW),
               pl.BlockSpec((None, win_buf, KV_ROW), lambda b, pt: (b, 0, 0)), row(KV_ROW), full(es), full(mimp)],
            out_specs=pl.BlockSpec((dec, NSA_W), lambda b, pt: (b, 0)),
        ),
        out_shape=jax.ShapeDtypeStruct((nseq * dec, NSA_W), F32),
        compiler_params=_params(),
        name="nsa_sample",
    )(page_table, *([pool] * n_pages), q, gl, slp, ck, cv, slc_rows, win_cache, win_rows, es, mimp)


def _alibi_slopes():
    return 2.0 ** (-ALIBI_MAX_BIAS * jnp.arange(1, N_HEADS + 1, dtype=F32) / N_HEADS)


def _slope_rows(tq):
    return jnp.repeat(_alibi_slopes().reshape(N_KV_GROUPS, HEADS_PER_GROUP), tq, axis=1)[..., None]


def _importance_matrix(n_cmp):
    n = jnp.arange(n_cmp)[:, None]
    j = jnp.arange(LANES)[None, :]
    per_sel = SLC_BLOCK // CMP_STRIDE
    inside = (SLC_BLOCK - CMP_BLOCK) // CMP_STRIDE + 1
    return ((n // per_sel == j) & (n % per_sel < inside)).astype(BF16)


def _block_onehot(n_keys):
    k = jnp.arange(n_keys)[:, None]
    return (k // SLC_BLOCK == jnp.arange(LANES)[None, :]).astype(BF16)


def _position_pieces(pos):
    pos = pos[:, None]
    hi = ((pos // SLC_BLOCK) * SLC_BLOCK).astype(BF16)
    lo = (pos % SLC_BLOCK).astype(BF16)
    pieces = jnp.concatenate([hi, hi, hi, lo, lo, lo], axis=-1)
    return jnp.pad(pieces, ((0, 0), (0, HEAD_DIM - pieces.shape[-1])))


def _prompt_tables(t_len):
    slp = _slope_rows(Q_BLOCK)
    s1 = slp.astype(BF16)
    s2 = (slp - s1.astype(F32)).astype(BF16)
    s3 = (slp - s1.astype(F32) - s2.astype(F32)).astype(BF16)
    qe = jnp.concatenate([s1, s2, s3, s1, s2, s3], axis=-1)
    qe = jnp.pad(qe, ((0, 0), (0, 0), (0, HEAD_DIM - qe.shape[-1])))
    key_tab = jnp.concatenate([_block_onehot(t_len), _position_pieces(jnp.arange(t_len))], axis=-1)
    ncp = t_len // CMP_STRIDE
    cmp_tab = _position_pieces(jnp.arange(ncp) * CMP_STRIDE + (CMP_BLOCK - 1))
    return (qe, jnp.transpose(_importance_matrix(ncp))), key_tab, cmp_tab


def kernel(x_prompt, x_sample, state_conv, cache_mem_kv, cache_cmp_kv, cache_slc_kv, cache_win_kv, page_table,
           mem_prompt, norm1, norm2, norm_final, w_in_a, conv_w, conv_b, conv_ln_g, conv_ln_b, w_in_b, kv_norm,
           w_kv_shared, cmp_pos, cmp_w1, cmp_w2, w_mem_kv, w_out, ffn_w_gate, ffn_w_up, ffn_w_down, moe_router,
           moe_w_gate, moe_w_up, moe_w_down):
    bp, t_len, d = x_prompt.shape
    db, dec, _ = x_sample.shape
    depth = norm1.shape[0]
    n_a = w_in_a.shape[0]
    n_mem = mem_prompt.shape[1]
    n_p = bp * t_len
    n_s = db * dec
    n_pages = page_table.shape[1]
    past = n_pages * PAGE_SIZE
    win_buf = cache_win_kv.shape[1]
    bf = lambda a: a.astype(BF16)

    x = jnp.concatenate([x_prompt.reshape(n_p, d), x_sample.reshape(n_s, d)], axis=0)

    mem_kv_p = mem_kv_proj(mem_prompt.reshape(bp * n_mem, d), w_mem_kv)
    mem_kv_p = mem_kv_p.reshape(depth, bp, n_mem, 2 * MEM_W)
    mem_kv_s = cache_mem_kv.reshape(depth, db, n_mem, 2 * MEM_W)
    moe_wg, moe_wu, moe_wd = bf(moe_w_gate), bf(moe_w_up), bf(moe_w_down)

    conv_p, conv_s = [], []
    zero_state = jnp.zeros((bp, CONV_STATE, C_CONV), F32)
    cmp_rows = slc_rows = win_rows = None
    nsa_ctx = None

    for l in range(depth):
        if l < n_a:
            uc, qm = norm_proj(x, norm1[l], bf(w_in_a[l]), (2 * C_CONV, MEM_W))
            mix_p, st_p = conv_mixer(uc, zero_state, conv_w[l], conv_b[l], conv_ln_g[l], conv_ln_b[l],
                                     bp, t_len, Q_BLOCK)
            mix_s, st_s = conv_mixer(uc, state_conv[l], conv_w[l], conv_b[l], conv_ln_g[l], conv_ln_b[l],
                                     db, dec, dec, row0=n_p)
            conv_p.append(st_p)
            conv_s.append(st_s)
        else:
            if nsa_ctx is None:
                cmp_rows, slc_rows, win_rows = norm_proj(x, kv_norm, bf(w_kv_shared), (KV_ROW,) * 3)
                w1 = bf(cmp_w1)
                w2 = bf(cmp_w2)
                prompt_consts, key_tab, cmp_tab = _prompt_tables(t_len)
                pe = cmp_pos.reshape(2, CMP_BLOCK // CMP_PGROUP, CMP_PGROUP * HEAD_DIM)
                cka, cvt = compress_prompt(cmp_rows[:n_p], pe, w1, w2, cmp_tab, bp, t_len)
                ck_s, cv_s = compress_sample(cache_cmp_kv.reshape(-1, PAGE_SIZE, KV_ROW), page_table,
                                             cmp_rows[n_p:].reshape(db, dec, KV_ROW), pe, w1, w2)
                ska, svt, wka, wvt = kv_prep(slc_rows, win_rows, key_tab, bp, t_len)
                nsa_ctx = dict(
                    prompt=prompt_consts,
                    sample=(_slope_rows(dec), _block_onehot(past), _importance_matrix(ck_s.shape[2])),
                    pool=cache_slc_kv.reshape(-1, PAGE_SIZE, KV_ROW),
                    win_cache=cache_win_kv.reshape(db, win_buf, KV_ROW))
            w_in = w_in_b[l - n_a]
            w_in = jnp.concatenate([w_in[:, :NSA_W], w_in[:, NSA_W + GATE_W:], w_in[:, NSA_W:NSA_W + GATE_W]], axis=1)
            q, qm, gl = norm_proj(x, norm1[l], bf(w_in), (NSA_W, MEM_W, GATE_W))
            gw = 3 * HEADS_PER_GROUP
            gl_t = jnp.transpose(gl[:n_p].reshape(n_p, N_KV_GROUPS, gw), (1, 2, 0))
            mix_p = nsa_prompt(q, gl_t, cka, cvt, ska, svt, wka, wvt, nsa_ctx["prompt"], bp, t_len)
            mix_s = nsa_sample(q, gl, ck_s, cv_s, nsa_ctx["pool"], page_table, slc_rows, nsa_ctx["win_cache"],
                               win_rows, nsa_ctx["sample"], n_p // dec, dec)
        mo_p = mem_attend(qm, mem_kv_p, l, bp, t_len, 512)
        mo_s = mem_attend(qm, mem_kv_s, l, db, dec, dec, spb=MEM_SEQS_PER_STEP, row0=n_p)
        if l % 2 == 0:
            xn, h2 = outproj_norm(x, mix_p, mix_s, mo_p, mo_s, bf(w_out[l]), norm2[l])
            x = ffn_dense(h2, xn, bf(ffn_w_gate[l // 2]), bf(ffn_w_up[l // 2]), bf(ffn_w_down[l // 2]))
        else:
            xn, h2, gate = outproj_norm(x, mix_p, mix_s, mo_p, mo_s, bf(w_out[l]), norm2[l],
                                        router=moe_router[l // 2])
            x = moe_ffn(h2, xn, gate, moe_wg, moe_wu, moe_wd, l // 2)

    y_p = final_norm(x, norm_final, 0, n_p)
    y_s = final_norm(x, norm_final, n_p, n_s)
    kv5 = lambda a, b, t: a.reshape(b, t, 2, N_KV_GROUPS, HEAD_DIM)
    keep_p = min(WINDOW, t_len)
    win_p = kv5(win_rows[:n_p], bp, t_len)[:, t_len - keep_p:]
    win_all = jnp.concatenate([cache_win_kv, kv5(win_rows[n_p:], db, dec)], axis=1)
    keep_s = min(WINDOW, past + dec)
    return (y_p.reshape(bp, t_len, d), y_s.reshape(db, dec, d),
            jnp.stack(conv_p), jnp.stack(conv_s),
            mem_kv_p.reshape(depth, bp, n_mem, 2, MEM_HEADS, MEM_HEAD_DIM),
            kv5(cmp_rows[:n_p], bp, t_len), kv5(cmp_rows[n_p:], db, dec),
            kv5(slc_rows[:n_p], bp, t_len), kv5(slc_rows[n_p:], db, dec),
            win_p, win_all[:, win_all.shape[1] - keep_s:])
```
